```python
import jax, jax.numpy as jnp
from jax import lax
import numpy as np

D_MODEL = 1024
BATCH = 4
SEQ = 4096
DEPTH = 1

GRID_W = 64

NA_HEADS = 8
NA_HEAD_DIM = 64
NA_WIDTH = NA_HEADS * NA_HEAD_DIM
NA_WIN_ROWS_MAX = 8
NA_WIN_COLS = 16

MLA_HEADS = 8
MLA_Q_LORA = 256
MLA_KV_LORA = 128
MLA_NOPE_DIM = 64
MLA_ROPE_DIM = 32
MLA_QK_DIM = MLA_NOPE_DIM + MLA_ROPE_DIM
MLA_V_DIM = 64
MLA_WIDTH = MLA_HEADS * MLA_V_DIM
MLA_Q_BLOCK = 128
ROPE_THETA = 10000.0

IN_SIZES = (NA_WIDTH, NA_WIDTH, NA_WIDTH, MLA_Q_LORA, MLA_KV_LORA, MLA_ROPE_DIM, D_MODEL, D_MODEL)
IN_TOTAL = 3 * NA_WIDTH + MLA_Q_LORA + MLA_KV_LORA + MLA_ROPE_DIM + 2 * D_MODEL

FFN_HIDDEN = -(-8 * D_MODEL // (3 * 256)) * 256

DN_ALPHA = (2.0 * DEPTH) ** 0.25
DN_BETA = (8.0 * DEPTH) ** -0.25
LN_EPS = 1e-5
RMS_EPS = 1e-6

kernel_name = "hybrid_na2d_mla_gated_deepnorm_encoder"


def layer_norm(x, g, b):
    xf = x.astype(jnp.float32)
    mu = jnp.mean(xf, axis=-1, keepdims=True)
    var = jnp.mean(jnp.square(xf - mu), axis=-1, keepdims=True)
    y = (xf - mu) * lax.rsqrt(var + LN_EPS) * g.astype(jnp.float32) + b.astype(jnp.float32)
    return y.astype(x.dtype)


def rms_norm(x, g):
    xf = x.astype(jnp.float32)
    y = xf * lax.rsqrt(jnp.mean(jnp.square(xf), axis=-1, keepdims=True) + RMS_EPS) * g.astype(jnp.float32)
    return y.astype(x.dtype)


def axial_rope_tables(seq_len):
    t = jnp.arange(seq_len)
    rows = (t // GRID_W).astype(jnp.float32)
    cols = (t % GRID_W).astype(jnp.float32)
    half = MLA_ROPE_DIM // 2
    inv_freq = ROPE_THETA ** (-jnp.arange(0, half, 2, dtype=jnp.float32) / half)
    ang_r = rows[:, None, None] * inv_freq
    ang_c = cols[:, None, None] * inv_freq
    return jnp.cos(ang_r), jnp.sin(ang_r), jnp.cos(ang_c), jnp.sin(ang_c)


def rotate_pairs(x, cos, sin):
    x1, x2 = jnp.split(x, 2, axis=-1)
    return jnp.concatenate([x1 * cos - x2 * sin, x2 * cos + x1 * sin], axis=-1)


def apply_axial_rope(x, tables):
    cos_r, sin_r, cos_c, sin_c = tables
    xf = x.astype(jnp.float32)
    xr, xc = jnp.split(xf, 2, axis=-1)
    out = jnp.concatenate([rotate_pairs(xr, cos_r, sin_r), rotate_pairs(xc, cos_c, sin_c)], axis=-1)
    return out.astype(x.dtype)


def neighbourhood_attention_2d(q, k, v, rpb):
    B, S, H, dh = q.shape
    rows = S // GRID_W
    kr = min(NA_WIN_ROWS_MAX, rows)
    kc = NA_WIN_COLS
    qg = q.reshape(B, rows, GRID_W, H, dh).transpose(1, 0, 2, 3, 4)
    kg = k.reshape(B, rows, GRID_W, H, dh)
    vg = v.reshape(B, rows, GRID_W, H, dh)
    col = jnp.arange(GRID_W)
    col_start = jnp.clip(col - kc // 2, 0, GRID_W - kc)
    col_idx = col_start[:, None] + jnp.arange(kc)[None, :]
    col_bias_idx = col_idx - col[:, None] + (NA_WIN_COLS - 1)
    rpb_cols = rpb.astype(jnp.float32)[:, :, col_bias_idx]
    scale = dh ** -0.5

    def one_row(args):
        r, q_row = args
        r_start = jnp.clip(r - kr // 2, 0, rows - kr)
        k_band = lax.dynamic_slice_in_dim(kg, r_start, kr, axis=1)
        v_band = lax.dynamic_slice_in_dim(vg, r_start, kr, axis=1)
        k_nb = k_band[:, :, col_idx]
        v_nb = v_band[:, :, col_idx]
        row_bias_idx = r_start + jnp.arange(kr) - r + (NA_WIN_ROWS_MAX - 1)
        bias = rpb_cols[:, row_bias_idx].transpose(0, 2, 1, 3)
        s = jnp.einsum('bqhd,brqchd->bhqrc', q_row, k_nb).astype(jnp.float32) * scale + bias[None]
        p = jax.nn.softmax(s.reshape(B, H, GRID_W, kr * kc), axis=-1).reshape(B, H, GRID_W, kr, kc)
        return jnp.einsum('bhqrc,brqchd->bqhd', p.astype(v.dtype), v_nb)

    out = lax.map(one_row, (jnp.arange(rows), qg))
    return out.transpose(1, 0, 2, 3, 4).reshape(B, S, H * dh)


def latent_attention(q_lat, kv_lat, k_rope, q_norm_g, w_uq, kv_norm_g, w_ukv, tables):
    B, S, _ = q_lat.shape
    H = MLA_HEADS
    q = (rms_norm(q_lat, q_norm_g) @ w_uq).reshape(B, S, H, MLA_QK_DIM)
    q_nope, q_pe = jnp.split(q, [MLA_NOPE_DIM], axis=-1)
    q = jnp.concatenate([q_nope, apply_axial_rope(q_pe, tables)], axis=-1)
    kv = (rms_norm(kv_lat, kv_norm_g) @ w_ukv).reshape(B, S, H, MLA_NOPE_DIM + MLA_V_DIM)
    k_nope, v = jnp.split(kv, [MLA_NOPE_DIM], axis=-1)
    k_pe = apply_axial_rope(k_rope[:, :, None, :], tables)
    k = jnp.concatenate([k_nope, jnp.broadcast_to(k_pe, (B, S, H, MLA_ROPE_DIM))], axis=-1)
    scale = MLA_QK_DIM ** -0.5
    n_blk = S // MLA_Q_BLOCK
    qb = q.reshape(B, n_blk, MLA_Q_BLOCK, H, MLA_QK_DIM).transpose(1, 0, 2, 3, 4)

    def one_block(q_blk):
        s = jnp.einsum('bqhd,bkhd->bhqk', q_blk, k).astype(jnp.float32) * scale
        p = jax.nn.softmax(s, axis=-1)
        return jnp.einsum('bhqk,bkhd->bqhd', p.astype(v.dtype), v)

    o = lax.map(one_block, qb)
    return o.transpose(1, 0, 2, 3, 4).reshape(B, S, H * MLA_V_DIM)


def hybrid_mixer(x, w_in, b_gate, na_rpb, mla_q_norm, mla_w_uq, mla_kv_norm, mla_w_ukv,
                 w_branch_na, w_branch_mla, w_out, tables):
    B, S, _ = x.shape
    h = x @ w_in
    offs = [int(o) for o in np.cumsum(IN_SIZES)[:-1]]
    q_na, k_na, v_na, q_lat, kv_lat, k_rope, g_a, g_b = jnp.split(h, offs, axis=-1)
    shp = (B, S, NA_HEADS, NA_HEAD_DIM)
    y_a = neighbourhood_attention_2d(q_na.reshape(shp), k_na.reshape(shp), v_na.reshape(shp), na_rpb) @ w_branch_na
    y_b = latent_attention(q_lat, kv_lat, k_rope, mla_q_norm, mla_w_uq, mla_kv_norm, mla_w_ukv, tables) @ w_branch_mla
    b_a, b_b = jnp.split(b_gate, 2, axis=-1)
    merged = jax.nn.sigmoid(g_a + b_a) * y_a + jax.nn.sigmoid(g_b + b_b) * y_b
    return merged @ w_out


def swiglu(x, w_ffn_in, w_ffn_out):
    gate, up = jnp.split(x @ w_ffn_in, 2, axis=-1)
    return (jax.nn.silu(gate) * up) @ w_ffn_out


def setup_inputs(seed: int = 0) -> dict:
    key = jax.random.key(seed)
    ks = jax.random.split(key, 18)
    nrm = jax.random.normal
    f32 = jnp.float32
    L = DEPTH
    return {
        "x": nrm(ks[0], (BATCH, SEQ, D_MODEL), f32),
        "w_in": nrm(ks[1], (L, D_MODEL, IN_TOTAL), f32) * D_MODEL ** -0.5,
        "b_gate": nrm(ks[2], (L, 2 * D_MODEL), f32) * 0.01,
        "na_rpb": nrm(ks[3], (L, NA_HEADS, 2 * NA_WIN_ROWS_MAX - 1, 2 * NA_WIN_COLS - 1), f32) * 0.02,
        "mla_q_norm": 1.0 + 0.01 * nrm(ks[4], (L, MLA_Q_LORA), f32),
        "mla_w_uq": nrm(ks[5], (L, MLA_Q_LORA, MLA_HEADS * MLA_QK_DIM), f32) * MLA_Q_LORA ** -0.5,
        "mla_kv_norm": 1.0 + 0.01 * nrm(ks[6], (L, MLA_KV_LORA), f32),
        "mla_w_ukv": nrm(ks[7], (L, MLA_KV_LORA, MLA_HEADS * (MLA_NOPE_DIM + MLA_V_DIM)), f32) * MLA_KV_LORA ** -0.5,
        "w_branch_na": nrm(ks[8], (L, NA_WIDTH, D_MODEL), f32) * (NA_WIDTH ** -0.5 * DN_BETA),
        "w_branch_mla": nrm(ks[9], (L, MLA_WIDTH, D_MODEL), f32) * (MLA_WIDTH ** -0.5 * DN_BETA),
        "w_out": nrm(ks[10], (L, D_MODEL, D_MODEL), f32) * (D_MODEL ** -0.5 * DN_BETA),
        "ln1_g": 1.0 + 0.01 * nrm(ks[11], (L, D_MODEL), f32),
        "ln1_b": 0.01 * nrm(ks[12], (L, D_MODEL), f32),
        "w_ffn_in": nrm(ks[13], (L, D_MODEL, 2 * FFN_HIDDEN), f32) * D_MODEL ** -0.5,
        "w_ffn_out": nrm(ks[14], (L, FFN_HIDDEN, D_MODEL), f32) * (FFN_HIDDEN ** -0.5 * DN_BETA),
        "ln2_g": 1.0 + 0.01 * nrm(ks[15], (L, D_MODEL), f32),
        "ln2_b": 0.01 * nrm(ks[16], (L, D_MODEL), f32),
    }


def reference(x, w_in, b_gate, na_rpb, mla_q_norm, mla_w_uq, mla_kv_norm, mla_w_ukv,
              w_branch_na, w_branch_mla, w_out, ln1_g, ln1_b, w_ffn_in, w_ffn_out, ln2_g, ln2_b):
    tables = axial_rope_tables(x.shape[1])
    for l in range(DEPTH):
        m = hybrid_mixer(x, w_in[l], b_gate[l], na_rpb[l], mla_q_norm[l], mla_w_uq[l],
                         mla_kv_norm[l], mla_w_ukv[l], w_branch_na[l], w_branch_mla[l], w_out[l], tables)
        x = layer_norm(DN_ALPHA * x + m, ln1_g[l], ln1_b[l])
        x = layer_norm(DN_ALPHA * x + swiglu(x, w_ffn_in[l], w_ffn_out[l]), ln2_g[l], ln2_b[l])
    return x
```

```python
import functools

import numpy as np
import jax
import jax.numpy as jnp
from jax import lax
from jax.experimental import pallas as pl
from jax.experimental.pallas import tpu as pltpu

F32 = jnp.float32
BF16 = jnp.bfloat16

D_MODEL = 1024
GRID_W = 64
NA_HEADS = 8
NA_HEAD_DIM = 64
NA_WIDTH = NA_HEADS * NA_HEAD_DIM
NA_WIN_ROWS = 8
NA_WIN_COLS = 16
MLA_HEADS = 8
MLA_Q_LORA = 256
MLA_KV_LORA = 128
MLA_NOPE = 64
MLA_ROPE = 32
MLA_QK = MLA_NOPE + MLA_ROPE
MLA_V = 64
MLA_WIDTH = MLA_HEADS * MLA_V
ROPE_THETA = 10000.0
FFN_HIDDEN = 2816
LN_EPS = 1e-5
RMS_EPS = 1e-6

LANES = 128
MLA_HEAD_PAD = LANES
VMEM_LIMIT = 56 * 1024 * 1024

PROJ_TM = 512
NA_PAIR_ROWS = 2
NA_KEY_ROWS = 10
NA_BLOCK_ROWS = 8
MLA_TQ = 256
MLA_KC = 512
MERGE_TM = 512
FFN_TM = 256
FFN_TF = 256

NEG_BIG = -1e30

NT = (((1,), (1,)), ((), ()))
TN = (((0,), (0,)), ((), ()))


def _dot(a, b):
    return jnp.dot(a, b, preferred_element_type=F32)


def _dot_nt(a, b):
    return lax.dot_general(a, b, NT, preferred_element_type=F32)


def _dot_tn(a, b):
    return lax.dot_general(a, b, TN, preferred_element_type=F32)


def _resident(shape):
    nd = len(shape)
    return pl.BlockSpec(shape, lambda *_: (0,) * nd, pipeline_mode=pl.Buffered(1))


def _rms(v, g):
    return v * lax.rsqrt(jnp.mean(v * v, axis=-1, keepdims=True) + RMS_EPS) * g


def _proj_kernel(x_ref, wqT_ref, wk_ref, wvT_ref, wlat_ref, wg_ref, qg_ref, kvg_ref,
                 wuqT_ref, wuk_ref, e_ref, wuvT_ref, cn_ref, sn_ref, ctf_ref, stf_ref,
                 qT_na_ref, k_na_ref, vT_na_ref, qT_mla_ref, k_mla_ref, vT_mla_ref, g_ref):
    xb = x_ref[...].astype(BF16)
    qT_na_ref[...] = _dot_nt(wqT_ref[...], xb).astype(BF16)
    k_na_ref[...] = _dot(xb, wk_ref[...]).astype(BF16)
    vT_na_ref[...] = _dot_nt(wvT_ref[...], xb).astype(BF16)
    g_ref[...] = _dot(xb, wg_ref[...]).astype(BF16)

    lat = _dot(xb, wlat_ref[...])
    qn = _rms(lat[:, :MLA_Q_LORA], qg_ref[...]).astype(BF16)
    kvn = _rms(lat[:, MLA_Q_LORA:MLA_Q_LORA + MLA_KV_LORA], kvg_ref[...]).astype(BF16)
    kpe = lat[:, 384:512] * cn_ref[...] + lat[:, 512:640] * sn_ref[...]
    k_mla = _dot(kvn, wuk_ref[...]) + _dot(kpe.astype(BF16), e_ref[...])
    k_mla_ref[...] = k_mla.astype(BF16)
    vT_mla_ref[...] = _dot_nt(wuvT_ref[...], kvn).astype(BF16)

    qT = _dot_nt(wuqT_ref[...], qn)
    ctf = ctf_ref[...]
    stf = stf_ref[...]
    for h in range(MLA_HEADS):
        qh = qT[h * MLA_HEAD_PAD:(h + 1) * MLA_HEAD_PAD]
        sw = jnp.concatenate([qh[MLA_ROPE:], qh[:MLA_ROPE]], axis=0)
        qT_mla_ref[h * MLA_HEAD_PAD:(h + 1) * MLA_HEAD_PAD, :] = (qh * ctf + sw * stf).astype(BF16)


def _proj_call(x2, w, n_tok, seq):
    tm = PROJ_TM
    steps_per_seq = seq // tm
    tok = lambda i: (i, 0)
    tokT = lambda i: (0, i)
    tab = lambda i: (i % steps_per_seq, 0)
    tabT = lambda i: (0, i % steps_per_seq)
    in_specs = [
        pl.BlockSpec((tm, D_MODEL), tok),
        _resident(w["wqT"].shape), _resident(w["wk"].shape), _resident(w["wvT"].shape),
        _resident(w["wlat"].shape), _resident(w["wg"].shape),
        _resident(w["qg"].shape), _resident(w["kvg"].shape),
        _resident(w["wuqT"].shape), _resident(w["wuk"].shape), _resident(w["e"].shape),
        _resident(w["wuvT"].shape),
        pl.BlockSpec((tm, LANES), tab), pl.BlockSpec((tm, LANES), tab),
        pl.BlockSpec((MLA_HEAD_PAD, tm), tabT), pl.BlockSpec((MLA_HEAD_PAD, tm), tabT),
    ]
    out_shape = [
        jax.ShapeDtypeStruct((NA_WIDTH, n_tok), BF16),
        jax.ShapeDtypeStruct((n_tok, NA_WIDTH), BF16),
        jax.ShapeDtypeStruct((NA_WIDTH, n_tok), BF16),
        jax.ShapeDtypeStruct((MLA_HEADS * MLA_HEAD_PAD, n_tok), BF16),
        jax.ShapeDtypeStruct((n_tok, MLA_HEADS * MLA_HEAD_PAD), BF16),
        jax.ShapeDtypeStruct((MLA_WIDTH, n_tok), BF16),
        jax.ShapeDtypeStruct((n_tok, 2 * D_MODEL), BF16),
    ]
    out_specs = [
        pl.BlockSpec((NA_WIDTH, tm), tokT),
        pl.BlockSpec((tm, NA_WIDTH), tok),
        pl.BlockSpec((NA_WIDTH, tm), tokT),
        pl.BlockSpec((MLA_HEADS * MLA_HEAD_PAD, tm), tokT),
        pl.BlockSpec((tm, MLA_HEADS * MLA_HEAD_PAD), tok),
        pl.BlockSpec((MLA_WIDTH, tm), tokT),
        pl.BlockSpec((tm, 2 * D_MODEL), tok),
    ]
    return pl.pallas_call(
        _proj_kernel,
        grid=(n_tok // tm,),
        in_specs=in_specs,
        out_specs=out_specs,
        out_shape=out_shape,
        compiler_params=pltpu.CompilerParams(
            dimension_semantics=("arbitrary",), vmem_limit_bytes=VMEM_LIMIT),
        name="proj",
    )(x2, w["wqT"], w["wk"], w["wvT"], w["wlat"], w["wg"], w["qg"], w["kvg"],
      w["wuqT"], w["wuk"], w["e"], w["wuvT"], w["cn"], w["sn"], w["ctf"], w["stf"])


NA_PAIR_TYPES = 5


def _na_pair_geometry(rows):
    last = rows - NA_KEY_ROWS
    return [(0, 0), (2, 0), (4, 0), (rows - 4, last), (rows - 2, last)]


def _na_bias_indices(rows):
    nk = NA_KEY_ROWS * GRID_W
    nq = NA_PAIR_ROWS * GRID_W
    ridx = np.zeros((NA_PAIR_TYPES, nk, nq), np.int32)
    cidx = np.zeros((NA_PAIR_TYPES, nk, nq), np.int32)
    valid = np.zeros((NA_PAIR_TYPES, nk, nq), bool)
    kk = np.arange(nk)
    qq = np.arange(nq)
    k_row_rel, k_col = kk // GRID_W, kk % GRID_W
    q_row_rel, q_col = qq // GRID_W, qq % GRID_W
    col_start = np.clip(q_col - NA_WIN_COLS // 2, 0, GRID_W - NA_WIN_COLS)
    for t, (r0, ws) in enumerate(_na_pair_geometry(rows)):
        q_row = r0 + q_row_rel
        k_row = ws + k_row_rel
        r_start = np.clip(q_row - NA_WIN_ROWS // 2, 0, rows - NA_WIN_ROWS)
        v_row = (k_row[:, None] >= r_start[None, :]) & (k_row[:, None] < r_start[None, :] + NA_WIN_ROWS)
        v_col = (k_col[:, None] >= col_start[None, :]) & (k_col[:, None] < col_start[None, :] + NA_WIN_COLS)
        valid[t] = v_row & v_col
        ridx[t] = np.clip(k_row[:, None] - q_row[None, :] + (NA_WIN_ROWS - 1), 0, 2 * NA_WIN_ROWS - 2)
        cidx[t] = np.clip(k_col[:, None] - q_col[None, :] + (NA_WIN_COLS - 1), 0, 2 * NA_WIN_COLS - 2)
    return ridx, cidx, valid


def _na_kernel(qT_ref, k_ref, vT_ref, bias_ref, oT_ref, *, rows):
    blk = pl.program_id(1)
    pairs = NA_BLOCK_ROWS // NA_PAIR_ROWS
    nk = NA_KEY_ROWS * GRID_W
    nq = NA_PAIR_ROWS * GRID_W
    row_id = lax.broadcasted_iota(jnp.int32, (2 * NA_HEAD_DIM, nq), 0)
    for p in range(pairs):
        r0 = blk * NA_BLOCK_ROWS + p * NA_PAIR_ROWS
        ws = jnp.clip(r0 - NA_WIN_ROWS // 2, 0, rows - NA_KEY_ROWS)
        typ = jnp.where(r0 == 0, 0, jnp.where(r0 == 2, 1, jnp.where(
            r0 == rows - 4, 3, jnp.where(r0 == rows - 2, 4, 2))))
        tok0 = pl.multiple_of(ws * GRID_W, LANES)
        for hp in range(NA_HEADS // 2):
            lanes = slice(hp * LANES, (hp + 1) * LANES)
            k_pair = k_ref[pl.ds(tok0, nk), lanes]
            q_pair = qT_ref[lanes, p * nq:(p + 1) * nq]
            for sub in range(2):
                h = 2 * hp + sub
                keep = (row_id >= sub * NA_HEAD_DIM) & (row_id < (sub + 1) * NA_HEAD_DIM)
                q_h = jnp.where(keep, q_pair, jnp.zeros_like(q_pair))
                s = _dot(k_pair, q_h) + bias_ref[h * NA_PAIR_TYPES + typ]
                m = jnp.max(s, axis=0, keepdims=True)
                e = jnp.exp(s - m)
                l = jnp.sum(e, axis=0, keepdims=True)
                vT = vT_ref[h * NA_HEAD_DIM:(h + 1) * NA_HEAD_DIM, pl.ds(tok0, nk)]
                o = _dot(vT, e.astype(BF16)) * (1.0 / l)
                oT_ref[h * NA_HEAD_DIM:(h + 1) * NA_HEAD_DIM, p * nq:(p + 1) * nq] = o.astype(BF16)


def _na_call(qT, k, vT, bias, batch, seq):
    rows = seq // GRID_W
    blocks = rows // NA_BLOCK_ROWS
    tq = NA_BLOCK_ROWS * GRID_W
    n_tok = batch * seq
    return pl.pallas_call(
        functools.partial(_na_kernel, rows=rows),
        grid=(batch, blocks),
        in_specs=[
            pl.BlockSpec((NA_WIDTH, tq), lambda b, i: (0, b * blocks + i)),
            pl.BlockSpec((seq, NA_WIDTH), lambda b, i: (b, 0)),
            pl.BlockSpec((NA_WIDTH, seq), lambda b, i: (0, b)),
            _resident(bias.shape),
        ],
        out_specs=pl.BlockSpec((NA_WIDTH, tq), lambda b, i: (0, b * blocks + i)),
        out_shape=jax.ShapeDtypeStruct((NA_WIDTH, n_tok), BF16),
        compiler_params=pltpu.CompilerParams(
            dimension_semantics=("arbitrary", "arbitrary"), vmem_limit_bytes=VMEM_LIMIT),
        name="na",
    )(qT, k, vT, bias)


def _mla_kernel(qT_ref, k_ref, vT_ref, oT_ref, *, seq):
    q = qT_ref[...]
    tq = q.shape[1]
    m = jnp.full((1, tq), -jnp.inf, F32)
    l = jnp.zeros((1, tq), F32)
    acc = jnp.zeros((MLA_V, tq), F32)
    for c in range(seq // MLA_KC):
        ks = slice(c * MLA_KC, (c + 1) * MLA_KC)
        s = _dot(k_ref[ks, :], q)
        m_new = jnp.maximum(m, jnp.max(s, axis=0, keepdims=True))
        alpha = jnp.exp(m - m_new)
        e = jnp.exp(s - m_new)
        l = alpha * l + jnp.sum(e, axis=0, keepdims=True)
        acc = alpha * acc + _dot(vT_ref[:, ks], e.astype(BF16))
        m = m_new
    oT_ref[...] = (acc * (1.0 / l)).astype(BF16)


def _mla_call(qT, k, vT, batch, seq):
    nq = seq // MLA_TQ
    n_tok = batch * seq
    return pl.pallas_call(
        functools.partial(_mla_kernel, seq=seq),
        grid=(batch, MLA_HEADS, nq),
        in_specs=[
            pl.BlockSpec((MLA_HEAD_PAD, MLA_TQ), lambda b, h, i: (h, b * nq + i)),
            pl.BlockSpec((seq, MLA_HEAD_PAD), lambda b, h, i: (b, h)),
            pl.BlockSpec((MLA_V, seq), lambda b, h, i: (h, b)),
        ],
        out_specs=pl.BlockSpec((MLA_V, MLA_TQ), lambda b, h, i: (h, b * nq + i)),
        out_shape=jax.ShapeDtypeStruct((MLA_WIDTH, n_tok), BF16),
        compiler_params=pltpu.CompilerParams(
            dimension_semantics=("arbitrary", "arbitrary", "arbitrary"), vmem_limit_bytes=VMEM_LIMIT),
        name="mla",
    )(qT, k, vT)


def _layer_norm(v, g, b):
    mu = jnp.mean(v, axis=-1, keepdims=True)
    c = v - mu
    var = jnp.mean(c * c, axis=-1, keepdims=True)
    return c * lax.rsqrt(var + LN_EPS) * g + b


def _merge_kernel(x_ref, oa_ref, ob_ref, g_ref, wa_ref, wb_ref, wo_ref, bg_ref, lg_ref, lb_ref,
                  o_ref, *, alpha):
    y_a = _dot_tn(oa_ref[...], wa_ref[...])
    y_b = _dot_tn(ob_ref[...], wb_ref[...])
    g = g_ref[...].astype(F32) + bg_ref[...]
    merged = jax.nn.sigmoid(g[:, :D_MODEL]) * y_a + jax.nn.sigmoid(g[:, D_MODEL:]) * y_b
    mix = _dot(merged.astype(BF16), wo_ref[...])
    o_ref[...] = _layer_norm(alpha * x_ref[...] + mix, lg_ref[...], lb_ref[...])


def _merge_call(x2, oTa, oTb, gates, w, alpha):
    n_tok = x2.shape[0]
    tm = MERGE_TM
    tok = lambda i: (i, 0)
    tokT = lambda i: (0, i)
    return pl.pallas_call(
        functools.partial(_merge_kernel, alpha=alpha),
        grid=(n_tok // tm,),
        in_specs=[
            pl.BlockSpec((tm, D_MODEL), tok),
            pl.BlockSpec((NA_WIDTH, tm), tokT),
            pl.BlockSpec((MLA_WIDTH, tm), tokT),
            pl.BlockSpec((tm, 2 * D_MODEL), tok),
            _resident(w["wa"].shape), _resident(w["wb"].shape), _resident(w["wo"].shape),
            _resident(w["bg"].shape), _resident(w["ln1g"].shape), _resident(w["ln1b"].shape),
        ],
        out_specs=pl.BlockSpec((tm, D_MODEL), tok),
        out_shape=jax.ShapeDtypeStruct((n_tok, D_MODEL), F32),
        compiler_params=pltpu.CompilerParams(
            dimension_semantics=("arbitrary",), vmem_limit_bytes=VMEM_LIMIT),
        name="merge",
    )(x2, oTa, oTb, gates, w["wa"], w["wb"], w["wo"], w["bg"], w["ln1g"], w["ln1b"])


def _ffn_kernel(x_ref, wgate_ref, wup_ref, wdown_ref, lg_ref, lb_ref, o_ref, *, alpha):
    x = x_ref[...]
    xb = x.astype(BF16)
    acc = alpha * x
    for c in range(FFN_HIDDEN // FFN_TF):
        cs = slice(c * FFN_TF, (c + 1) * FFN_TF)
        gate = _dot(xb, wgate_ref[:, cs])
        up = _dot(xb, wup_ref[:, cs])
        act = (gate * jax.nn.sigmoid(gate) * up).astype(BF16)
        acc = acc + _dot(act, wdown_ref[cs, :])
    o_ref[...] = _layer_norm(acc, lg_ref[...], lb_ref[...])


def _ffn_call(x1, w, alpha):
    n_tok = x1.shape[0]
    tm = FFN_TM
    tok = lambda i: (i, 0)
    return pl.pallas_call(
        functools.partial(_ffn_kernel, alpha=alpha),
        grid=(n_tok // tm,),
        in_specs=[
            pl.BlockSpec((tm, D_MODEL), tok),
            _resident(w["wgate"].shape), _resident(w["wup"].shape), _resident(w["wdown"].shape),
            _resident(w["ln2g"].shape), _resident(w["ln2b"].shape),
        ],
        out_specs=pl.BlockSpec((tm, D_MODEL), tok),
        out_shape=jax.ShapeDtypeStruct((n_tok, D_MODEL), F32),
        compiler_params=pltpu.CompilerParams(
            dimension_semantics=("arbitrary",), vmem_limit_bytes=VMEM_LIMIT),
        name="ffn",
    )(x1, w["wgate"], w["wup"], w["wdown"], w["ln2g"], w["ln2b"])


def _rope_swap(cols):
    q = MLA_ROPE // 4
    r1, r2, c1, c2 = cols[..., :q], cols[..., q:2 * q], cols[..., 2 * q:3 * q], cols[..., 3 * q:]
    return jnp.concatenate([-r2, r1, -c2, c1], axis=-1)


def _rope_tables(seq):
    t = jnp.arange(seq)
    rows = (t // GRID_W).astype(F32)
    cols = (t % GRID_W).astype(F32)
    half = MLA_ROPE // 2
    inv_freq = ROPE_THETA ** (-jnp.arange(0, half, 2, dtype=F32) / half)
    ang_r = rows[:, None] * inv_freq
    ang_c = cols[:, None] * inv_freq
    cos = jnp.concatenate([jnp.cos(ang_r)] * 2 + [jnp.cos(ang_c)] * 2, axis=-1)
    sin = jnp.concatenate([jnp.sin(ang_r)] * 2 + [jnp.sin(ang_c)] * 2, axis=-1)
    return cos, sin


def _prep_layer(seq, w_in, b_gate, na_rpb, q_norm, w_uq, kv_norm, w_ukv, w_bna, w_bmla, w_out,
                ln1_g, ln1_b, w_ffn_in, w_ffn_out, ln2_g, ln2_b):
    o_q, o_k, o_v = 0, NA_WIDTH, 2 * NA_WIDTH
    o_ql = 3 * NA_WIDTH
    o_kvl = o_ql + MLA_Q_LORA
    o_kr = o_kvl + MLA_KV_LORA
    o_g = o_kr + MLA_ROPE
    w = {}
    na_scale = NA_HEAD_DIM ** -0.5
    w["wqT"] = (w_in[:, o_q:o_k] * na_scale).T.astype(BF16)
    w["wk"] = w_in[:, o_k:o_v].astype(BF16)
    w["wvT"] = w_in[:, o_v:o_ql].T.astype(BF16)
    k_rope_w = w_in[:, o_kr:o_g]
    pad = jnp.zeros((D_MODEL, LANES - MLA_ROPE), F32)
    w["wlat"] = jnp.concatenate(
        [w_in[:, o_ql:o_kr], k_rope_w, pad, _rope_swap(k_rope_w), pad], axis=1).astype(BF16)
    w["wg"] = w_in[:, o_g:].astype(BF16)
    w["qg"] = q_norm.reshape(1, MLA_Q_LORA)
    w["kvg"] = kv_norm.reshape(1, MLA_KV_LORA)

    uq = w_uq.reshape(MLA_Q_LORA, MLA_HEADS, MLA_QK)
    uq_pe = uq[:, :, MLA_NOPE:]
    uq_arr = jnp.concatenate([uq[:, :, :MLA_NOPE], uq_pe, _rope_swap(uq_pe)], axis=-1)
    w["wuqT"] = uq_arr.reshape(MLA_Q_LORA, MLA_HEADS * MLA_HEAD_PAD).T.astype(BF16)
    ukv = w_ukv.reshape(MLA_KV_LORA, MLA_HEADS, MLA_NOPE + MLA_V)
    uk_arr = jnp.concatenate(
        [ukv[:, :, :MLA_NOPE], jnp.zeros((MLA_KV_LORA, MLA_HEADS, MLA_HEAD_PAD - MLA_NOPE), F32)], axis=-1)
    w["wuk"] = uk_arr.reshape(MLA_KV_LORA, MLA_HEADS * MLA_HEAD_PAD).astype(BF16)
    w["wuvT"] = ukv[:, :, MLA_NOPE:].reshape(MLA_KV_LORA, MLA_WIDTH).T.astype(BF16)
    e = np.zeros((LANES, MLA_HEADS * MLA_HEAD_PAD), np.float32)
    for h in range(MLA_HEADS):
        e[np.arange(MLA_ROPE), h * MLA_HEAD_PAD + MLA_NOPE + np.arange(MLA_ROPE)] = 1.0
    w["e"] = jnp.asarray(e, BF16)

    cos, sin = _rope_tables(seq)
    zpad = jnp.zeros((seq, LANES - MLA_ROPE), F32)
    w["cn"] = jnp.concatenate([cos, zpad], axis=1)
    w["sn"] = jnp.concatenate([sin, zpad], axis=1)
    q_scale = MLA_QK ** -0.5
    ones = jnp.ones((MLA_NOPE, seq), F32)
    zer = jnp.zeros((MLA_NOPE, seq), F32)
    zer_r = jnp.zeros((MLA_ROPE, seq), F32)
    w["ctf"] = jnp.concatenate([ones, cos.T, zer_r], axis=0) * q_scale
    w["stf"] = jnp.concatenate([zer, sin.T, zer_r], axis=0) * q_scale

    rows = seq // GRID_W
    ridx, cidx, valid = _na_bias_indices(rows)
    bias = jnp.where(valid[None], na_rpb[:, ridx, cidx], NEG_BIG)
    w["na_bias"] = bias.reshape(NA_HEADS * NA_PAIR_TYPES, NA_KEY_ROWS * GRID_W, NA_PAIR_ROWS * GRID_W)

    w["wa"] = w_bna.astype(BF16)
    w["wb"] = w_bmla.astype(BF16)
    w["wo"] = w_out.astype(BF16)
    w["bg"] = b_gate.reshape(1, 2 * D_MODEL)
    w["ln1g"] = ln1_g.reshape(1, D_MODEL)
    w["ln1b"] = ln1_b.reshape(1, D_MODEL)
    w["wgate"] = w_ffn_in[:, :FFN_HIDDEN].astype(BF16)
    w["wup"] = w_ffn_in[:, FFN_HIDDEN:].astype(BF16)
    w["wdown"] = w_ffn_out.astype(BF16)
    w["ln2g"] = ln2_g.reshape(1, D_MODEL)
    w["ln2b"] = ln2_b.reshape(1, D_MODEL)
    return w


def kernel(x, w_in, b_gate, na_rpb, mla_q_norm, mla_w_uq, mla_kv_norm, mla_w_ukv, w_branch_na, w_branch_mla, w_out, ln1_g, ln1_b, w_ffn_in, w_ffn_out, ln2_g, ln2_b):
    batch, seq, d = x.shape
    depth = w_in.shape[0]
    alpha = (2.0 * depth) ** 0.25
    n_tok = batch * seq
    x2 = x.reshape(n_tok, d)
    for l in range(depth):
        w = _prep_layer(seq, w_in[l], b_gate[l], na_rpb[l], mla_q_norm[l], mla_w_uq[l], mla_kv_norm[l],
                        mla_w_ukv[l], w_branch_na[l], w_branch_mla[l], w_out[l], ln1_g[l], ln1_b[l],
                        w_ffn_in[l], w_ffn_out[l], ln2_g[l], ln2_b[l])
        qT_na, k_na, vT_na, qT_mla, k_mla, vT_mla, gates = _proj_call(x2, w, n_tok, seq)
        oT_na = _na_call(qT_na, k_na, vT_na, w["na_bias"], batch, seq)
        oT_mla = _mla_call(qT_mla, k_mla, vT_mla, batch, seq)
        x1 = _merge_call(x2, oT_na, oT_mla, gates, w, alpha)
        x2 = _ffn_call(x1, w, alpha)
    return x2.reshape(batch, seq, d)
```

```python
import functools

import numpy as np
import jax
import jax.numpy as jnp
from jax import lax
from jax.experimental import pallas as pl
from jax.experimental.pallas import tpu as pltpu

F32 = jnp.float32
BF16 = jnp.bfloat16

D_MODEL = 1024
GRID_W = 64
NA_HEADS = 8
NA_HEAD_DIM = 64
NA_WIDTH = NA_HEADS * NA_HEAD_DIM
NA_WIN_ROWS = 8
NA_WIN_COLS = 16
MLA_HEADS = 8
MLA_Q_LORA = 256
MLA_KV_LORA = 128
MLA_NOPE = 64
MLA_ROPE = 32
MLA_QK = MLA_NOPE + MLA_ROPE
MLA_V = 64
MLA_WIDTH = MLA_HEADS * MLA_V
ROPE_THETA = 10000.0
FFN_HIDDEN = 2816
LN_EPS = 1e-5
RMS_EPS = 1e-6

LANES = 128
MLA_HEAD_PAD = LANES
VMEM_LIMIT = 56 * 1024 * 1024

PROJ_TM = 512
NA_PAIR_ROWS = 2
NA_KEY_ROWS = 10
NA_BLOCK_ROWS = 8
MLA_TQ = 256
MLA_KC = 512
SUM_ROWS = 16
LOG2_E = 1.4426950408889634
MERGE_TM = 512
FFN_TM = 256
FFN_TF = 256

NEG_BIG = -1e30

NT = (((1,), (1,)), ((), ()))
TN = (((0,), (0,)), ((), ()))


def _dot(a, b):
    return jnp.dot(a, b, preferred_element_type=F32)


def _dot_nt(a, b):
    return lax.dot_general(a, b, NT, preferred_element_type=F32)


def _dot_tn(a, b):
    return lax.dot_general(a, b, TN, preferred_element_type=F32)


def _resident(shape):
    nd = len(shape)
    return pl.BlockSpec(shape, lambda *_: (0,) * nd, pipeline_mode=pl.Buffered(1))


def _rms(v, g):
    return v * lax.rsqrt(jnp.mean(v * v, axis=-1, keepdims=True) + RMS_EPS) * g


def _proj_kernel(x_ref, wqT_ref, wk_ref, wvT_ref, wlat_ref, wg_ref, qg_ref, kvg_ref,
                 wuqT_ref, wuk_ref, e_ref, wuvT_ref, cn_ref, sn_ref, ctf_ref, stf_ref,
                 qT_na_ref, k_na_ref, vT_na_ref, qT_mla_ref, k_mla_ref, vT_mla_ref, g_ref):
    xb = x_ref[...].astype(BF16)
    qT_na_ref[...] = _dot_nt(wqT_ref[...], xb).astype(BF16)
    k_na_ref[...] = _dot(xb, wk_ref[...]).astype(BF16)
    vT_na_ref[...] = _dot_nt(wvT_ref[...], xb).astype(BF16)
    g_ref[...] = _dot(xb, wg_ref[...]).astype(BF16)

    lat = _dot(xb, wlat_ref[...])
    qn = _rms(lat[:, :MLA_Q_LORA], qg_ref[...]).astype(BF16)
    kvn = _rms(lat[:, MLA_Q_LORA:MLA_Q_LORA + MLA_KV_LORA], kvg_ref[...]).astype(BF16)
    kpe = lat[:, 384:512] * cn_ref[...] + lat[:, 512:640] * sn_ref[...]
    k_mla = _dot(kvn, wuk_ref[...]) + _dot(kpe.astype(BF16), e_ref[...])
    k_mla_ref[...] = k_mla.astype(BF16)
    vT_mla_ref[...] = _dot_nt(wuvT_ref[...], kvn).astype(BF16)

    qT = _dot_nt(wuqT_ref[...], qn)
    ctf = ctf_ref[...]
    stf = stf_ref[...]
    for h in range(MLA_HEADS):
        qh = qT[h * MLA_HEAD_PAD:(h + 1) * MLA_HEAD_PAD]
        sw = jnp.concatenate([qh[MLA_ROPE:], qh[:MLA_ROPE]], axis=0)
        qT_mla_ref[h * MLA_HEAD_PAD:(h + 1) * MLA_HEAD_PAD, :] = (qh * ctf + sw * stf).astype(BF16)


def _proj_call(x2, w, n_tok, seq):
    tm = PROJ_TM
    steps_per_seq = seq // tm
    tok = lambda i: (i, 0)
    tokT = lambda i: (0, i)
    tab = lambda i: (i % steps_per_seq, 0)
    tabT = lambda i: (0, i % steps_per_seq)
    in_specs = [
        pl.BlockSpec((tm, D_MODEL), tok),
        _resident(w["wqT"].shape), _resident(w["wk"].shape), _resident(w["wvT"].shape),
        _resident(w["wlat"].shape), _resident(w["wg"].shape),
        _resident(w["qg"].shape), _resident(w["kvg"].shape),
        _resident(w["wuqT"].shape), _resident(w["wuk"].shape), _resident(w["e"].shape),
        _resident(w["wuvT"].shape),
        pl.BlockSpec((tm, LANES), tab), pl.BlockSpec((tm, LANES), tab),
        pl.BlockSpec((MLA_HEAD_PAD, tm), tabT), pl.BlockSpec((MLA_HEAD_PAD, tm), tabT),
    ]
    out_shape = [
        jax.ShapeDtypeStruct((NA_WIDTH, n_tok), BF16),
        jax.ShapeDtypeStruct((n_tok, NA_WIDTH), BF16),
        jax.ShapeDtypeStruct((NA_WIDTH, n_tok), BF16),
        jax.ShapeDtypeStruct((MLA_HEADS * MLA_HEAD_PAD, n_tok), BF16),
        jax.ShapeDtypeStruct((n_tok, MLA_HEADS * MLA_HEAD_PAD), BF16),
        jax.ShapeDtypeStruct((MLA_WIDTH, n_tok), BF16),
        jax.ShapeDtypeStruct((n_tok, 2 * D_MODEL), BF16),
    ]
    out_specs = [
        pl.BlockSpec((NA_WIDTH, tm), tokT),
        pl.BlockSpec((tm, NA_WIDTH), tok),
        pl.BlockSpec((NA_WIDTH, tm), tokT),
        pl.BlockSpec((MLA_HEADS * MLA_HEAD_PAD, tm), tokT),
        pl.BlockSpec((tm, MLA_HEADS * MLA_HEAD_PAD), tok),
        pl.BlockSpec((MLA_WIDTH, tm), tokT),
        pl.BlockSpec((tm, 2 * D_MODEL), tok),
    ]
    return pl.pallas_call(
        _proj_kernel,
        grid=(n_tok // tm,),
        in_specs=in_specs,
        out_specs=out_specs,
        out_shape=out_shape,
        compiler_params=pltpu.CompilerParams(
            dimension_semantics=("arbitrary",), vmem_limit_bytes=VMEM_LIMIT),
        name="proj",
    )(x2, w["wqT"], w["wk"], w["wvT"], w["wlat"], w["wg"], w["qg"], w["kvg"],
      w["wuqT"], w["wuk"], w["e"], w["wuvT"], w["cn"], w["sn"], w["ctf"], w["stf"])


NA_PAIR_TYPES = 5


def _na_pair_geometry(rows):
    last = rows - NA_KEY_ROWS
    return [(0, 0), (2, 0), (4, 0), (rows - 4, last), (rows - 2, last)]


def _na_bias_table(rpb, rows):
    n_dr = 2 * NA_WIN_ROWS - 1
    n_dc = 2 * NA_WIN_COLS - 1
    w = GRID_W
    lead = w - NA_WIN_COLS
    v = jnp.pad(rpb, ((0, 0), (0, 0), (lead, 2 * w - lead - n_dc)))
    a = jnp.tile(v, (1, 1, w))[:, :, :w * (2 * w - 1)].reshape(NA_HEADS, n_dr, w, 2 * w - 1)
    toe = jnp.swapaxes(a[:, :, :, w - 1:], 2, 3)
    col = np.arange(w)
    col_start = np.clip(col - NA_WIN_COLS // 2, 0, w - NA_WIN_COLS)
    v_col = (col[:, None] >= col_start[None, :]) & (col[:, None] < col_start[None, :] + NA_WIN_COLS)
    toe = jnp.where(v_col[None, None], toe, NEG_BIG)
    masked = jnp.full((NA_HEADS, w, w), NEG_BIG, F32)
    types = []
    for r0, ws in _na_pair_geometry(rows):
        key_rows = []
        for kr in range(NA_KEY_ROWS):
            halves = []
            for qi in range(NA_PAIR_ROWS):
                q_row, k_row = r0 + qi, ws + kr
                r_start = min(max(q_row - NA_WIN_ROWS // 2, 0), rows - NA_WIN_ROWS)
                if r_start <= k_row < r_start + NA_WIN_ROWS:
                    halves.append(toe[:, k_row - q_row + NA_WIN_ROWS - 1])
                else:
                    halves.append(masked)
            key_rows.append(jnp.concatenate(halves, axis=-1))
        types.append(jnp.concatenate(key_rows, axis=1))
    bias = jnp.stack(types, axis=1)
    return bias.reshape(NA_HEADS * NA_PAIR_TYPES, NA_KEY_ROWS * w, NA_PAIR_ROWS * w)


def _na_kernel(qT_ref, k_ref, vT_ref, bias_ref, oT_ref, *, rows):
    blk = pl.program_id(1)
    pairs = NA_BLOCK_ROWS // NA_PAIR_ROWS
    nk = NA_KEY_ROWS * GRID_W
    nq = NA_PAIR_ROWS * GRID_W
    row_id = lax.broadcasted_iota(jnp.int32, (2 * NA_HEAD_DIM, nq), 0)
    for p in range(pairs):
        r0 = blk * NA_BLOCK_ROWS + p * NA_PAIR_ROWS
        ws = jnp.clip(r0 - NA_WIN_ROWS // 2, 0, rows - NA_KEY_ROWS)
        typ = jnp.where(r0 == 0, 0, jnp.where(r0 == 2, 1, jnp.where(
            r0 == rows - 4, 3, jnp.where(r0 == rows - 2, 4, 2))))
        tok0 = pl.multiple_of(ws * GRID_W, LANES)
        for hp in range(NA_HEADS // 2):
            lanes = slice(hp * LANES, (hp + 1) * LANES)
            k_pair = k_ref[pl.ds(tok0, nk), lanes]
            q_pair = qT_ref[lanes, p * nq:(p + 1) * nq]
            for sub in range(2):
                h = 2 * hp + sub
                keep = (row_id >= sub * NA_HEAD_DIM) & (row_id < (sub + 1) * NA_HEAD_DIM)
                q_h = jnp.where(keep, q_pair, jnp.zeros_like(q_pair))
                s = _dot(k_pair, q_h) + bias_ref[h * NA_PAIR_TYPES + typ]
                m = jnp.max(s, axis=0, keepdims=True)
                e = jnp.exp(s - m)
                l = jnp.sum(e, axis=0, keepdims=True)
                vT = vT_ref[h * NA_HEAD_DIM:(h + 1) * NA_HEAD_DIM, pl.ds(tok0, nk)]
                o = _dot(vT, e.astype(BF16)) * (1.0 / l)
                oT_ref[h * NA_HEAD_DIM:(h + 1) * NA_HEAD_DIM, p * nq:(p + 1) * nq] = o.astype(BF16)


def _na_call(qT, k, vT, bias, batch, seq):
    rows = seq // GRID_W
    blocks = rows // NA_BLOCK_ROWS
    tq = NA_BLOCK_ROWS * GRID_W
    n_tok = batch * seq
    return pl.pallas_call(
        functools.partial(_na_kernel, rows=rows),
        grid=(batch, blocks),
        in_specs=[
            pl.BlockSpec((NA_WIDTH, tq), lambda b, i: (0, b * blocks + i)),
            pl.BlockSpec((seq, NA_WIDTH), lambda b, i: (b, 0)),
            pl.BlockSpec((NA_WIDTH, seq), lambda b, i: (0, b)),
            _resident(bias.shape),
        ],
        out_specs=pl.BlockSpec((NA_WIDTH, tq), lambda b, i: (0, b * blocks + i)),
        out_shape=jax.ShapeDtypeStruct((NA_WIDTH, n_tok), BF16),
        compiler_params=pltpu.CompilerParams(
            dimension_semantics=("arbitrary", "arbitrary"), vmem_limit_bytes=VMEM_LIMIT),
        name="na",
    )(qT, k, vT, bias)


def _mla_kernel(qT_ref, k_ref, vT_ref, oT_ref, s_ref, *, seq):
    n_chunks = seq // MLA_KC
    tq = oT_ref.shape[1]
    ones = jnp.ones((SUM_ROWS, MLA_KC), BF16)

    def score_chunk(h, c, slot, m):
        hk = slice(h * MLA_HEAD_PAD, (h + 1) * MLA_HEAD_PAD)
        s = _dot(k_ref[c * MLA_KC:(c + 1) * MLA_KC, hk], qT_ref[hk, :])
        s_ref[slot, c * MLA_KC:(c + 1) * MLA_KC, :] = s
        return jnp.maximum(m, jnp.max(s, axis=0, keepdims=True))

    neg = jnp.full((1, tq), -jnp.inf, F32)
    m_next = neg
    for c in range(n_chunks):
        m_next = score_chunk(0, c, 0, m_next)
    for h in range(MLA_HEADS):
        slot = h % 2
        m_cur, m_next = m_next, neg
        hv = slice(h * MLA_V, (h + 1) * MLA_V)
        acc = jnp.zeros((MLA_V + SUM_ROWS, tq), F32)
        for c in range(n_chunks):
            if h + 1 < MLA_HEADS:
                m_next = score_chunk(h + 1, c, 1 - slot, m_next)
            cs = slice(c * MLA_KC, (c + 1) * MLA_KC)
            p = jnp.exp2(s_ref[slot, cs, :] - m_cur).astype(BF16)
            v_ext = jnp.concatenate([vT_ref[hv, cs], ones], axis=0)
            acc = acc + _dot(v_ext, p)
        oT_ref[hv, :] = (acc[:MLA_V] * (1.0 / acc[MLA_V:MLA_V + 1])).astype(BF16)


def _mla_call(qT, k, vT, batch, seq):
    nq = seq // MLA_TQ
    n_tok = batch * seq
    return pl.pallas_call(
        functools.partial(_mla_kernel, seq=seq),
        grid=(batch, nq),
        in_specs=[
            pl.BlockSpec((MLA_HEADS * MLA_HEAD_PAD, MLA_TQ), lambda b, i: (0, b * nq + i)),
            pl.BlockSpec((seq, MLA_HEADS * MLA_HEAD_PAD), lambda b, i: (b, 0)),
            pl.BlockSpec((MLA_WIDTH, seq), lambda b, i: (0, b)),
        ],
        out_specs=pl.BlockSpec((MLA_WIDTH, MLA_TQ), lambda b, i: (0, b * nq + i)),
        out_shape=jax.ShapeDtypeStruct((MLA_WIDTH, n_tok), BF16),
        scratch_shapes=[pltpu.VMEM((2, seq, MLA_TQ), F32)],
        compiler_params=pltpu.CompilerParams(
            dimension_semantics=("arbitrary", "arbitrary"), vmem_limit_bytes=VMEM_LIMIT),
        name="mla",
    )(qT, k, vT)


def _layer_norm(v, g, b):
    mu = jnp.mean(v, axis=-1, keepdims=True)
    c = v - mu
    var = jnp.mean(c * c, axis=-1, keepdims=True)
    return c * lax.rsqrt(var + LN_EPS) * g + b


def _merge_kernel(x_ref, oa_ref, ob_ref, g_ref, wa_ref, wb_ref, wo_ref, bg_ref, lg_ref, lb_ref,
                  o_ref, *, alpha):
    y_a = _dot_tn(oa_ref[...], wa_ref[...])
    y_b = _dot_tn(ob_ref[...], wb_ref[...])
    g = g_ref[...].astype(F32) + bg_ref[...]
    merged = jax.nn.sigmoid(g[:, :D_MODEL]) * y_a + jax.nn.sigmoid(g[:, D_MODEL:]) * y_b
    mix = _dot(merged.astype(BF16), wo_ref[...])
    o_ref[...] = _layer_norm(alpha * x_ref[...] + mix, lg_ref[...], lb_ref[...])


def _merge_call(x2, oTa, oTb, gates, w, alpha):
    n_tok = x2.shape[0]
    tm = MERGE_TM
    tok = lambda i: (i, 0)
    tokT = lambda i: (0, i)
    return pl.pallas_call(
        functools.partial(_merge_kernel, alpha=alpha),
        grid=(n_tok // tm,),
        in_specs=[
            pl.BlockSpec((tm, D_MODEL), tok),
            pl.BlockSpec((NA_WIDTH, tm), tokT),
            pl.BlockSpec((MLA_WIDTH, tm), tokT),
            pl.BlockSpec((tm, 2 * D_MODEL), tok),
            _resident(w["wa"].shape), _resident(w["wb"].shape), _resident(w["wo"].shape),
            _resident(w["bg"].shape), _resident(w["ln1g"].shape), _resident(w["ln1b"].shape),
        ],
        out_specs=pl.BlockSpec((tm, D_MODEL), tok),
        out_shape=jax.ShapeDtypeStruct((n_tok, D_MODEL), F32),
        compiler_params=pltpu.CompilerParams(
            dimension_semantics=("arbitrary",), vmem_limit_bytes=VMEM_LIMIT),
        name="merge",
    )(x2, oTa, oTb, gates, w["wa"], w["wb"], w["wo"], w["bg"], w["ln1g"], w["ln1b"])


def _ffn_kernel(x_ref, wgate_ref, wup_ref, wdown_ref, lg_ref, lb_ref, o_ref, *, alpha):
    x = x_ref[...]
    xb = x.astype(BF16)
    acc = alpha * x
    for c in range(FFN_HIDDEN // FFN_TF):
        cs = slice(c * FFN_TF, (c + 1) * FFN_TF)
        gate = _dot(xb, wgate_ref[:, cs])
        up = _dot(xb, wup_ref[:, cs])
        act = (gate * jax.nn.sigmoid(gate) * up).astype(BF16)
        acc = acc + _dot(act, wdown_ref[cs, :])
    o_ref[...] = _layer_norm(acc, lg_ref[...], lb_ref[...])


def _ffn_call(x1, w, alpha):
    n_tok = x1.shape[0]
    tm = FFN_TM
    tok = lambda i: (i, 0)
    return pl.pallas_call(
        functools.partial(_ffn_kernel, alpha=alpha),
        grid=(n_tok // tm,),
        in_specs=[
            pl.BlockSpec((tm, D_MODEL), tok),
            _resident(w["wgate"].shape), _resident(w["wup"].shape), _resident(w["wdown"].shape),
            _resident(w["ln2g"].shape), _resident(w["ln2b"].shape),
        ],
        out_specs=pl.BlockSpec((tm, D_MODEL), tok),
        out_shape=jax.ShapeDtypeStruct((n_tok, D_MODEL), F32),
        compiler_params=pltpu.CompilerParams(
            dimension_semantics=("arbitrary",), vmem_limit_bytes=VMEM_LIMIT),
        name="ffn",
    )(x1, w["wgate"], w["wup"], w["wdown"], w["ln2g"], w["ln2b"])


def _rope_swap(cols):
    q = MLA_ROPE // 4
    r1, r2, c1, c2 = cols[..., :q], cols[..., q:2 * q], cols[..., 2 * q:3 * q], cols[..., 3 * q:]
    return jnp.concatenate([-r2, r1, -c2, c1], axis=-1)


def _rope_tables(seq):
    t = jnp.arange(seq)
    rows = (t // GRID_W).astype(F32)
    cols = (t % GRID_W).astype(F32)
    half = MLA_ROPE // 2
    inv_freq = ROPE_THETA ** (-jnp.arange(0, half, 2, dtype=F32) / half)
    ang_r = rows[:, None] * inv_freq
    ang_c = cols[:, None] * inv_freq
    cos = jnp.concatenate([jnp.cos(ang_r)] * 2 + [jnp.cos(ang_c)] * 2, axis=-1)
    sin = jnp.concatenate([jnp.sin(ang_r)] * 2 + [jnp.sin(ang_c)] * 2, axis=-1)
    return cos, sin


def _prep_layer(seq, w_in, b_gate, na_rpb, q_norm, w_uq, kv_norm, w_ukv, w_bna, w_bmla, w_out,
                ln1_g, ln1_b, w_ffn_in, w_ffn_out, ln2_g, ln2_b):
    o_q, o_k, o_v = 0, NA_WIDTH, 2 * NA_WIDTH
    o_ql = 3 * NA_WIDTH
    o_kvl = o_ql + MLA_Q_LORA
    o_kr = o_kvl + MLA_KV_LORA
    o_g = o_kr + MLA_ROPE
    w = {}
    na_scale = NA_HEAD_DIM ** -0.5
    w["wqT"] = (w_in[:, o_q:o_k] * na_scale).T.astype(BF16)
    w["wk"] = w_in[:, o_k:o_v].astype(BF16)
    w["wvT"] = w_in[:, o_v:o_ql].T.astype(BF16)
    k_rope_w = w_in[:, o_kr:o_g]
    pad = jnp.zeros((D_MODEL, LANES - MLA_ROPE), F32)
    w["wlat"] = jnp.concatenate(
        [w_in[:, o_ql:o_kr], k_rope_w, pad, _rope_swap(k_rope_w), pad], axis=1).astype(BF16)
    w["wg"] = w_in[:, o_g:].astype(BF16)
    w["qg"] = q_norm.reshape(1, MLA_Q_LORA)
    w["kvg"] = kv_norm.reshape(1, MLA_KV_LORA)

    uq = w_uq.reshape(MLA_Q_LORA, MLA_HEADS, MLA_QK)
    uq_pe = uq[:, :, MLA_NOPE:]
    uq_arr = jnp.concatenate([uq[:, :, :MLA_NOPE], uq_pe, _rope_swap(uq_pe)], axis=-1)
    w["wuqT"] = uq_arr.reshape(MLA_Q_LORA, MLA_HEADS * MLA_HEAD_PAD).T.astype(BF16)
    ukv = w_ukv.reshape(MLA_KV_LORA, MLA_HEADS, MLA_NOPE + MLA_V)
    uk_arr = jnp.concatenate(
        [ukv[:, :, :MLA_NOPE], jnp.zeros((MLA_KV_LORA, MLA_HEADS, MLA_HEAD_PAD - MLA_NOPE), F32)], axis=-1)
    w["wuk"] = uk_arr.reshape(MLA_KV_LORA, MLA_HEADS * MLA_HEAD_PAD).astype(BF16)
    w["wuvT"] = ukv[:, :, MLA_NOPE:].reshape(MLA_KV_LORA, MLA_WIDTH).T.astype(BF16)
    e = np.zeros((LANES, MLA_HEADS * MLA_HEAD_PAD), np.float32)
    for h in range(MLA_HEADS):
        e[np.arange(MLA_ROPE), h * MLA_HEAD_PAD + MLA_NOPE + np.arange(MLA_ROPE)] = 1.0
    w["e"] = jnp.asarray(e, BF16)

    cos, sin = _rope_tables(seq)
    zpad = jnp.zeros((seq, LANES - MLA_ROPE), F32)
    w["cn"] = jnp.concatenate([cos, zpad], axis=1)
    w["sn"] = jnp.concatenate([sin, zpad], axis=1)
    q_scale = MLA_QK ** -0.5 * LOG2_E
    ones = jnp.ones((MLA_NOPE, seq), F32)
    zer = jnp.zeros((MLA_NOPE, seq), F32)
    zer_r = jnp.zeros((MLA_ROPE, seq), F32)
    w["ctf"] = jnp.concatenate([ones, cos.T, zer_r], axis=0) * q_scale
    w["stf"] = jnp.concatenate([zer, sin.T, zer_r], axis=0) * q_scale

    w["na_bias"] = _na_bias_table(na_rpb, seq // GRID_W)

    w["wa"] = w_bna.astype(BF16)
    w["wb"] = w_bmla.astype(BF16)
    w["wo"] = w_out.astype(BF16)
    w["bg"] = b_gate.reshape(1, 2 * D_MODEL)
    w["ln1g"] = ln1_g.reshape(1, D_MODEL)
    w["ln1b"] = ln1_b.reshape(1, D_MODEL)
    w["wgate"] = w_ffn_in[:, :FFN_HIDDEN].astype(BF16)
    w["wup"] = w_ffn_in[:, FFN_HIDDEN:].astype(BF16)
    w["wdown"] = w_ffn_out.astype(BF16)
    w["ln2g"] = ln2_g.reshape(1, D_MODEL)
    w["ln2b"] = ln2_b.reshape(1, D_MODEL)
    return w


def kernel(x, w_in, b_gate, na_rpb, mla_q_norm, mla_w_uq, mla_kv_norm, mla_w_ukv, w_branch_na, w_branch_mla, w_out, ln1_g, ln1_b, w_ffn_in, w_ffn_out, ln2_g, ln2_b):
    batch, seq, d = x.shape
    depth = w_in.shape[0]
    alpha = (2.0 * depth) ** 0.25
    n_tok = batch * seq
    x2 = x.reshape(n_tok, d)
    for l in range(depth):
        w = _prep_layer(seq, w_in[l], b_gate[l], na_rpb[l], mla_q_norm[l], mla_w_uq[l], mla_kv_norm[l],
                        mla_w_ukv[l], w_branch_na[l], w_branch_mla[l], w_out[l], ln1_g[l], ln1_b[l],
                        w_ffn_in[l], w_ffn_out[l], ln2_g[l], ln2_b[l])
        qT_na, k_na, vT_na, qT_mla, k_mla, vT_mla, gates = _proj_call(x2, w, n_tok, seq)
        oT_na = _na_call(qT_na, k_na, vT_na, w["na_bias"], batch, seq)
        oT_mla = _mla_call(qT_mla, k_mla, vT_mla, batch, seq)
        x1 = _merge_call(x2, oT_na, oT_mla, gates, w, alpha)
        x2 = _ffn_call(x1, w, alpha)
    return x2.reshape(batch, seq, d)
```

```python
import functools

import numpy as np
import jax
import jax.numpy as jnp
from jax import lax
from jax.experimental import pallas as pl
from jax.experimental.pallas import tpu as pltpu

F32 = jnp.float32
BF16 = jnp.bfloat16

D_MODEL = 1024
GRID_W = 64
NA_HEADS = 8
NA_HEAD_DIM = 64
NA_WIDTH = NA_HEADS * NA_HEAD_DIM
NA_WIN_ROWS = 8
NA_WIN_COLS = 16
MLA_HEADS = 8
MLA_Q_LORA = 256
MLA_KV_LORA = 128
MLA_NOPE = 64
MLA_ROPE = 32
MLA_QK = MLA_NOPE + MLA_ROPE
MLA_V = 64
MLA_WIDTH = MLA_HEADS * MLA_V
ROPE_THETA = 10000.0
FFN_HIDDEN = 2816
LN_EPS = 1e-5
RMS_EPS = 1e-6

LANES = 128
MLA_HEAD_PAD = LANES
VMEM_LIMIT = 56 * 1024 * 1024

PROJ_TM = 512
NA_Q_ROWS = 4
NA_KEY_ROWS = 12
NA_BLOCK_ROWS = 8
MLA_TQ = 256
MLA_KC = 512
SUM_ROWS = 16
LOG2_E = 1.4426950408889634
MERGE_TM = 512
FFN_TM = 512
FFN_TF = 256

NEG_BIG = -1e30

NT = (((1,), (1,)), ((), ()))
TN = (((0,), (0,)), ((), ()))


def _dot(a, b):
    return jnp.dot(a, b, preferred_element_type=F32)


def _dot_nt(a, b):
    return lax.dot_general(a, b, NT, preferred_element_type=F32)


def _dot_tn(a, b):
    return lax.dot_general(a, b, TN, preferred_element_type=F32)


def _resident(shape):
    nd = len(shape)
    return pl.BlockSpec(shape, lambda *_: (0,) * nd, pipeline_mode=pl.Buffered(1))


def _rms(v, g):
    return v * lax.rsqrt(jnp.mean(v * v, axis=-1, keepdims=True) + RMS_EPS) * g


def _proj_kernel(x_ref, wqT_ref, wk_ref, wvT_ref, wlat_ref, wg_ref, qg_ref, kvg_ref,
                 wuqT_ref, wuk_ref, e_ref, wuvT_ref, cn_ref, sn_ref, ctf_ref, stf_ref,
                 qT_na_ref, k_na_ref, vT_na_ref, qT_mla_ref, k_mla_ref, vT_mla_ref, g_ref):
    xb = x_ref[...].astype(BF16)
    qT_na_ref[...] = _dot_nt(wqT_ref[...], xb).astype(BF16)
    k_na_ref[...] = _dot(xb, wk_ref[...]).astype(BF16)
    vT_na_ref[...] = _dot_nt(wvT_ref[...], xb).astype(BF16)
    g_ref[...] = _dot(xb, wg_ref[...]).astype(BF16)

    lat = _dot(xb, wlat_ref[...])
    qn = _rms(lat[:, :MLA_Q_LORA], qg_ref[...]).astype(BF16)
    kvn = _rms(lat[:, MLA_Q_LORA:MLA_Q_LORA + MLA_KV_LORA], kvg_ref[...]).astype(BF16)
    kpe = lat[:, 384:512] * cn_ref[...] + lat[:, 512:640] * sn_ref[...]
    k_mla = _dot(kvn, wuk_ref[...]) + _dot(kpe.astype(BF16), e_ref[...])
    k_mla_ref[...] = k_mla.astype(BF16)
    vT_mla_ref[...] = _dot_nt(wuvT_ref[...], kvn).astype(BF16)

    qT = _dot_nt(wuqT_ref[...], qn)
    ctf = ctf_ref[...]
    stf = stf_ref[...]
    for h in range(MLA_HEADS):
        qh = qT[h * MLA_HEAD_PAD:(h + 1) * MLA_HEAD_PAD]
        sw = jnp.concatenate([qh[MLA_ROPE:], qh[:MLA_ROPE]], axis=0)
        qT_mla_ref[h * MLA_HEAD_PAD:(h + 1) * MLA_HEAD_PAD, :] = (qh * ctf + sw * stf).astype(BF16)


def _proj_call(x2, w, n_tok, seq):
    tm = PROJ_TM
    steps_per_seq = seq // tm
    tok = lambda i: (i, 0)
    tokT = lambda i: (0, i)
    tab = lambda i: (i % steps_per_seq, 0)
    tabT = lambda i: (0, i % steps_per_seq)
    in_specs = [
        pl.BlockSpec((tm, D_MODEL), tok),
        _resident(w["wqT"].shape), _resident(w["wk"].shape), _resident(w["wvT"].shape),
        _resident(w["wlat"].shape), _resident(w["wg"].shape),
        _resident(w["qg"].shape), _resident(w["kvg"].shape),
        _resident(w["wuqT"].shape), _resident(w["wuk"].shape), _resident(w["e"].shape),
        _resident(w["wuvT"].shape),
        pl.BlockSpec((tm, LANES), tab), pl.BlockSpec((tm, LANES), tab),
        pl.BlockSpec((MLA_HEAD_PAD, tm), tabT), pl.BlockSpec((MLA_HEAD_PAD, tm), tabT),
    ]
    out_shape = [
        jax.ShapeDtypeStruct((NA_WIDTH, n_tok), BF16),
        jax.ShapeDtypeStruct((n_tok, NA_WIDTH), BF16),
        jax.ShapeDtypeStruct((NA_WIDTH, n_tok), BF16),
        jax.ShapeDtypeStruct((MLA_HEADS * MLA_HEAD_PAD, n_tok), BF16),
        jax.ShapeDtypeStruct((n_tok, MLA_HEADS * MLA_HEAD_PAD), BF16),
        jax.ShapeDtypeStruct((MLA_WIDTH, n_tok), BF16),
        jax.ShapeDtypeStruct((n_tok, 2 * D_MODEL), BF16),
    ]
    out_specs = [
        pl.BlockSpec((NA_WIDTH, tm), tokT),
        pl.BlockSpec((tm, NA_WIDTH), tok),
        pl.BlockSpec((NA_WIDTH, tm), tokT),
        pl.BlockSpec((MLA_HEADS * MLA_HEAD_PAD, tm), tokT),
        pl.BlockSpec((tm, MLA_HEADS * MLA_HEAD_PAD), tok),
        pl.BlockSpec((MLA_WIDTH, tm), tokT),
        pl.BlockSpec((tm, 2 * D_MODEL), tok),
    ]
    return pl.pallas_call(
        _proj_kernel,
        grid=(n_tok // tm,),
        in_specs=in_specs,
        out_specs=out_specs,
        out_shape=out_shape,
        compiler_params=pltpu.CompilerParams(
            dimension_semantics=("arbitrary",), vmem_limit_bytes=VMEM_LIMIT),
        name="proj",
    )(x2, w["wqT"], w["wk"], w["wvT"], w["wlat"], w["wg"], w["qg"], w["kvg"],
      w["wuqT"], w["wuk"], w["e"], w["wuvT"], w["cn"], w["sn"], w["ctf"], w["stf"])


NA_GROUP_TYPES = 3


def _na_group_geometry(rows):
    return [(0, 0), (NA_Q_ROWS, 0), (rows - NA_Q_ROWS, rows - NA_KEY_ROWS)]


def _na_bias_table(rpb, rows):
    n_dr = 2 * NA_WIN_ROWS - 1
    n_dc = 2 * NA_WIN_COLS - 1
    w = GRID_W
    lead = w - NA_WIN_COLS
    v = jnp.pad(rpb, ((0, 0), (0, 0), (lead, 2 * w - lead - n_dc)))
    a = jnp.tile(v, (1, 1, w))[:, :, :w * (2 * w - 1)].reshape(NA_HEADS, n_dr, w, 2 * w - 1)
    toe = jnp.swapaxes(a[:, :, :, w - 1:], 2, 3)
    col = np.arange(w)
    col_start = np.clip(col - NA_WIN_COLS // 2, 0, w - NA_WIN_COLS)
    v_col = (col[:, None] >= col_start[None, :]) & (col[:, None] < col_start[None, :] + NA_WIN_COLS)
    toe = jnp.where(v_col[None, None], toe * LOG2_E, NEG_BIG)
    masked = jnp.full((NA_HEADS, w, w), NEG_BIG, F32)
    types = []
    for r0, ws in _na_group_geometry(rows):
        key_rows = []
        for kr in range(NA_KEY_ROWS):
            parts = []
            for qi in range(NA_Q_ROWS):
                q_row, k_row = r0 + qi, ws + kr
                r_start = min(max(q_row - NA_WIN_ROWS // 2, 0), rows - NA_WIN_ROWS)
                if r_start <= k_row < r_start + NA_WIN_ROWS:
                    parts.append(toe[:, k_row - q_row + NA_WIN_ROWS - 1])
                else:
                    parts.append(masked)
            key_rows.append(jnp.concatenate(parts, axis=-1))
        types.append(jnp.concatenate(key_rows, axis=1))
    bias = jnp.stack(types, axis=1)
    return bias.reshape(NA_HEADS * NA_GROUP_TYPES, NA_KEY_ROWS * w, NA_Q_ROWS * w)


def _na_kernel(qT_ref, k_ref, vT_ref, bias_ref, oT_ref, *, rows):
    blk = pl.program_id(1)
    nk = NA_KEY_ROWS * GRID_W
    nq = NA_Q_ROWS * GRID_W
    row_id = lax.broadcasted_iota(jnp.int32, (2 * NA_HEAD_DIM, nq), 0)
    ones = jnp.ones((SUM_ROWS, nk), BF16)
    items = [(g, h) for g in range(NA_BLOCK_ROWS // NA_Q_ROWS) for h in range(NA_HEADS)]

    def window(g):
        r0 = blk * NA_BLOCK_ROWS + g * NA_Q_ROWS
        ws = jnp.clip(r0 - NA_WIN_ROWS // 2, 0, rows - NA_KEY_ROWS)
        typ = jnp.where(r0 == 0, 0, jnp.where(r0 == rows - NA_Q_ROWS, 2, 1))
        return pl.multiple_of(ws * GRID_W, 2 * LANES), typ

    def scores(g, h):
        tok0, typ = window(g)
        hp, sub = divmod(h, 2)
        lanes = slice(hp * LANES, (hp + 1) * LANES)
        k_pair = k_ref[pl.ds(tok0, nk), lanes]
        q_pair = qT_ref[lanes, g * nq:(g + 1) * nq]
        keep = (row_id >= sub * NA_HEAD_DIM) & (row_id < (sub + 1) * NA_HEAD_DIM)
        q_h = jnp.where(keep, q_pair, jnp.zeros_like(q_pair))
        s = _dot(k_pair, q_h) + bias_ref[h * NA_GROUP_TYPES + typ]
        return s, jnp.max(s, axis=0, keepdims=True)

    def finish(g, h, s, m):
        tok0, _ = window(g)
        hv = slice(h * NA_HEAD_DIM, (h + 1) * NA_HEAD_DIM)
        p = jnp.exp2(s - m).astype(BF16)
        v_ext = jnp.concatenate([vT_ref[hv, pl.ds(tok0, nk)], ones], axis=0)
        acc = _dot(v_ext, p)
        o = acc[:NA_HEAD_DIM] * (1.0 / acc[NA_HEAD_DIM:NA_HEAD_DIM + 1])
        oT_ref[hv, g * nq:(g + 1) * nq] = o.astype(BF16)

    cur = scores(*items[0])
    for i, item in enumerate(items):
        nxt = scores(*items[i + 1]) if i + 1 < len(items) else None
        finish(*item, *cur)
        cur = nxt


def _na_call(qT, k, vT, bias, batch, seq):
    rows = seq // GRID_W
    blocks = rows // NA_BLOCK_ROWS
    tq = NA_BLOCK_ROWS * GRID_W
    n_tok = batch * seq
    return pl.pallas_call(
        functools.partial(_na_kernel, rows=rows),
        grid=(batch, blocks),
        in_specs=[
            pl.BlockSpec((NA_WIDTH, tq), lambda b, i: (0, b * blocks + i)),
            pl.BlockSpec((seq, NA_WIDTH), lambda b, i: (b, 0)),
            pl.BlockSpec((NA_WIDTH, seq), lambda b, i: (0, b)),
            _resident(bias.shape),
        ],
        out_specs=pl.BlockSpec((NA_WIDTH, tq), lambda b, i: (0, b * blocks + i)),
        out_shape=jax.ShapeDtypeStruct((NA_WIDTH, n_tok), BF16),
        compiler_params=pltpu.CompilerParams(
            dimension_semantics=("arbitrary", "arbitrary"), vmem_limit_bytes=VMEM_LIMIT),
        name="na",
    )(qT, k, vT, bias)


def _mla_kernel(qT_ref, k_ref, vT_ref, oT_ref, s_ref, *, seq):
    n_chunks = seq // MLA_KC
    tq = oT_ref.shape[1]
    ones = jnp.ones((SUM_ROWS, MLA_KC), BF16)

    def score_chunk(h, c, slot, m):
        hk = slice(h * MLA_HEAD_PAD, (h + 1) * MLA_HEAD_PAD)
        s = _dot(k_ref[c * MLA_KC:(c + 1) * MLA_KC, hk], qT_ref[hk, :])
        s_ref[slot, c * MLA_KC:(c + 1) * MLA_KC, :] = s
        return jnp.maximum(m, jnp.max(s, axis=0, keepdims=True))

    neg = jnp.full((1, tq), -jnp.inf, F32)
    m_next = neg
    for c in range(n_chunks):
        m_next = score_chunk(0, c, 0, m_next)
    for h in range(MLA_HEADS):
        slot = h % 2
        m_cur, m_next = m_next, neg
        hv = slice(h * MLA_V, (h + 1) * MLA_V)
        acc = jnp.zeros((MLA_V + SUM_ROWS, tq), F32)
        for c in range(n_chunks):
            if h + 1 < MLA_HEADS:
                m_next = score_chunk(h + 1, c, 1 - slot, m_next)
            cs = slice(c * MLA_KC, (c + 1) * MLA_KC)
            p = jnp.exp2(s_ref[slot, cs, :] - m_cur).astype(BF16)
            v_ext = jnp.concatenate([vT_ref[hv, cs], ones], axis=0)
            acc = acc + _dot(v_ext, p)
        oT_ref[hv, :] = (acc[:MLA_V] * (1.0 / acc[MLA_V:MLA_V + 1])).astype(BF16)


def _mla_call(qT, k, vT, batch, seq):
    nq = seq // MLA_TQ
    n_tok = batch * seq
    return pl.pallas_call(
        functools.partial(_mla_kernel, seq=seq),
        grid=(batch, nq),
        in_specs=[
            pl.BlockSpec((MLA_HEADS * MLA_HEAD_PAD, MLA_TQ), lambda b, i: (0, b * nq + i)),
            pl.BlockSpec((seq, MLA_HEADS * MLA_HEAD_PAD), lambda b, i: (b, 0)),
            pl.BlockSpec((MLA_WIDTH, seq), lambda b, i: (0, b)),
        ],
        out_specs=pl.BlockSpec((MLA_WIDTH, MLA_TQ), lambda b, i: (0, b * nq + i)),
        out_shape=jax.ShapeDtypeStruct((MLA_WIDTH, n_tok), BF16),
        scratch_shapes=[pltpu.VMEM((2, seq, MLA_TQ), F32)],
        compiler_params=pltpu.CompilerParams(
            dimension_semantics=("arbitrary", "arbitrary"), vmem_limit_bytes=VMEM_LIMIT),
        name="mla",
    )(qT, k, vT)


def _layer_norm(v, g, b):
    mu = jnp.mean(v, axis=-1, keepdims=True)
    c = v - mu
    var = jnp.mean(c * c, axis=-1, keepdims=True)
    return c * lax.rsqrt(var + LN_EPS) * g + b


def _merge_kernel(x_ref, oa_ref, ob_ref, g_ref, wa_ref, wb_ref, wo_ref, bg_ref, lg_ref, lb_ref,
                  o_ref, *, alpha):
    y_a = _dot_tn(oa_ref[...], wa_ref[...])
    y_b = _dot_tn(ob_ref[...], wb_ref[...])
    g = g_ref[...].astype(F32) + bg_ref[...]
    merged = jax.nn.sigmoid(g[:, :D_MODEL]) * y_a + jax.nn.sigmoid(g[:, D_MODEL:]) * y_b
    mix = _dot(merged.astype(BF16), wo_ref[...])
    o_ref[...] = _layer_norm(alpha * x_ref[...] + mix, lg_ref[...], lb_ref[...])


def _merge_call(x2, oTa, oTb, gates, w, alpha):
    n_tok = x2.shape[0]
    tm = MERGE_TM
    tok = lambda i: (i, 0)
    tokT = lambda i: (0, i)
    return pl.pallas_call(
        functools.partial(_merge_kernel, alpha=alpha),
        grid=(n_tok // tm,),
        in_specs=[
            pl.BlockSpec((tm, D_MODEL), tok),
            pl.BlockSpec((NA_WIDTH, tm), tokT),
            pl.BlockSpec((MLA_WIDTH, tm), tokT),
            pl.BlockSpec((tm, 2 * D_MODEL), tok),
            _resident(w["wa"].shape), _resident(w["wb"].shape), _resident(w["wo"].shape),
            _resident(w["bg"].shape), _resident(w["ln1g"].shape), _resident(w["ln1b"].shape),
        ],
        out_specs=pl.BlockSpec((tm, D_MODEL), tok),
        out_shape=jax.ShapeDtypeStruct((n_tok, D_MODEL), F32),
        compiler_params=pltpu.CompilerParams(
            dimension_semantics=("arbitrary",), vmem_limit_bytes=VMEM_LIMIT),
        name="merge",
    )(x2, oTa, oTb, gates, w["wa"], w["wb"], w["wo"], w["bg"], w["ln1g"], w["ln1b"])


def _ffn_kernel(x_ref, wgate_ref, wup_ref, wdown_ref, lg_ref, lb_ref, o_ref, *, alpha):
    x = x_ref[...]
    xb = x.astype(BF16)
    acc = alpha * x
    for c in range(FFN_HIDDEN // FFN_TF):
        cs = slice(c * FFN_TF, (c + 1) * FFN_TF)
        gate = _dot(xb, wgate_ref[:, cs])
        up = _dot(xb, wup_ref[:, cs])
        act = (gate * jax.nn.sigmoid(gate) * up).astype(BF16)
        acc = acc + _dot(act, wdown_ref[cs, :])
    o_ref[...] = _layer_norm(acc, lg_ref[...], lb_ref[...])


def _ffn_call(x1, w, alpha):
    n_tok = x1.shape[0]
    tm = FFN_TM
    tok = lambda i: (i, 0)
    return pl.pallas_call(
        functools.partial(_ffn_kernel, alpha=alpha),
        grid=(n_tok // tm,),
        in_specs=[
            pl.BlockSpec((tm, D_MODEL), tok),
            _resident(w["wgate"].shape), _resident(w["wup"].shape), _resident(w["wdown"].shape),
            _resident(w["ln2g"].shape), _resident(w["ln2b"].shape),
        ],
        out_specs=pl.BlockSpec((tm, D_MODEL), tok),
        out_shape=jax.ShapeDtypeStruct((n_tok, D_MODEL), F32),
        compiler_params=pltpu.CompilerParams(
            dimension_semantics=("arbitrary",), vmem_limit_bytes=VMEM_LIMIT),
        name="ffn",
    )(x1, w["wgate"], w["wup"], w["wdown"], w["ln2g"], w["ln2b"])


def _rope_swap(cols):
    q = MLA_ROPE // 4
    r1, r2, c1, c2 = cols[..., :q], cols[..., q:2 * q], cols[..., 2 * q:3 * q], cols[..., 3 * q:]
    return jnp.concatenate([-r2, r1, -c2, c1], axis=-1)


def _rope_tables(seq):
    t = jnp.arange(seq)
    rows = (t // GRID_W).astype(F32)
    cols = (t % GRID_W).astype(F32)
    half = MLA_ROPE // 2
    inv_freq = ROPE_THETA ** (-jnp.arange(0, half, 2, dtype=F32) / half)
    ang_r = rows[:, None] * inv_freq
    ang_c = cols[:, None] * inv_freq
    cos = jnp.concatenate([jnp.cos(ang_r)] * 2 + [jnp.cos(ang_c)] * 2, axis=-1)
    sin = jnp.concatenate([jnp.sin(ang_r)] * 2 + [jnp.sin(ang_c)] * 2, axis=-1)
    return cos, sin


def _prep_layer(seq, w_in, b_gate, na_rpb, q_norm, w_uq, kv_norm, w_ukv, w_bna, w_bmla, w_out,
                ln1_g, ln1_b, w_ffn_in, w_ffn_out, ln2_g, ln2_b):
    o_q, o_k, o_v = 0, NA_WIDTH, 2 * NA_WIDTH
    o_ql = 3 * NA_WIDTH
    o_kvl = o_ql + MLA_Q_LORA
    o_kr = o_kvl + MLA_KV_LORA
    o_g = o_kr + MLA_ROPE
    w = {}
    na_scale = NA_HEAD_DIM ** -0.5 * LOG2_E
    w["wqT"] = (w_in[:, o_q:o_k] * na_scale).T.astype(BF16)
    w["wk"] = w_in[:, o_k:o_v].astype(BF16)
    w["wvT"] = w_in[:, o_v:o_ql].T.astype(BF16)
    k_rope_w = w_in[:, o_kr:o_g]
    pad = jnp.zeros((D_MODEL, LANES - MLA_ROPE), F32)
    w["wlat"] = jnp.concatenate(
        [w_in[:, o_ql:o_kr], k_rope_w, pad, _rope_swap(k_rope_w), pad], axis=1).astype(BF16)
    w["wg"] = w_in[:, o_g:].astype(BF16)
    w["qg"] = q_norm.reshape(1, MLA_Q_LORA)
    w["kvg"] = kv_norm.reshape(1, MLA_KV_LORA)

    uq = w_uq.reshape(MLA_Q_LORA, MLA_HEADS, MLA_QK)
    uq_pe = uq[:, :, MLA_NOPE:]
    uq_arr = jnp.concatenate([uq[:, :, :MLA_NOPE], uq_pe, _rope_swap(uq_pe)], axis=-1)
    w["wuqT"] = uq_arr.reshape(MLA_Q_LORA, MLA_HEADS * MLA_HEAD_PAD).T.astype(BF16)
    ukv = w_ukv.reshape(MLA_KV_LORA, MLA_HEADS, MLA_NOPE + MLA_V)
    uk_arr = jnp.concatenate(
        [ukv[:, :, :MLA_NOPE], jnp.zeros((MLA_KV_LORA, MLA_HEADS, MLA_HEAD_PAD - MLA_NOPE), F32)], axis=-1)
    w["wuk"] = uk_arr.reshape(MLA_KV_LORA, MLA_HEADS * MLA_HEAD_PAD).astype(BF16)
    w["wuvT"] = ukv[:, :, MLA_NOPE:].reshape(MLA_KV_LORA, MLA_WIDTH).T.astype(BF16)
    e = np.zeros((LANES, MLA_HEADS * MLA_HEAD_PAD), np.float32)
    for h in range(MLA_HEADS):
        e[np.arange(MLA_ROPE), h * MLA_HEAD_PAD + MLA_NOPE + np.arange(MLA_ROPE)] = 1.0
    w["e"] = jnp.asarray(e, BF16)

    cos, sin = _rope_tables(seq)
    zpad = jnp.zeros((seq, LANES - MLA_ROPE), F32)
    w["cn"] = jnp.concatenate([cos, zpad], axis=1)
    w["sn"] = jnp.concatenate([sin, zpad], axis=1)
    q_scale = MLA_QK ** -0.5 * LOG2_E
    ones = jnp.ones((MLA_NOPE, seq), F32)
    zer = jnp.zeros((MLA_NOPE, seq), F32)
    zer_r = jnp.zeros((MLA_ROPE, seq), F32)
    w["ctf"] = jnp.concatenate([ones, cos.T, zer_r], axis=0) * q_scale
    w["stf"] = jnp.concatenate([zer, sin.T, zer_r], axis=0) * q_scale

    w["na_bias"] = _na_bias_table(na_rpb, seq // GRID_W)

    w["wa"] = w_bna.astype(BF16)
    w["wb"] = w_bmla.astype(BF16)
    w["wo"] = w_out.astype(BF16)
    w["bg"] = b_gate.reshape(1, 2 * D_MODEL)
    w["ln1g"] = ln1_g.reshape(1, D_MODEL)
    w["ln1b"] = ln1_b.reshape(1, D_MODEL)
    w["wgate"] = w_ffn_in[:, :FFN_HIDDEN].astype(BF16)
    w["wup"] = w_ffn_in[:, FFN_HIDDEN:].astype(BF16)
    w["wdown"] = w_ffn_out.astype(BF16)
    w["ln2g"] = ln2_g.reshape(1, D_MODEL)
    w["ln2b"] = ln2_b.reshape(1, D_MODEL)
    return w


def kernel(x, w_in, b_gate, na_rpb, mla_q_norm, mla_w_uq, mla_kv_norm, mla_w_ukv, w_branch_na, w_branch_mla, w_out, ln1_g, ln1_b, w_ffn_in, w_ffn_out, ln2_g, ln2_b):
    batch, seq, d = x.shape
    depth = w_in.shape[0]
    alpha = (2.0 * depth) ** 0.25
    n_tok = batch * seq
    x2 = x.reshape(n_tok, d)
    for l in range(depth):
        w = _prep_layer(seq, w_in[l], b_gate[l], na_rpb[l], mla_q_norm[l], mla_w_uq[l], mla_kv_norm[l],
                        mla_w_ukv[l], w_branch_na[l], w_branch_mla[l], w_out[l], ln1_g[l], ln1_b[l],
                        w_ffn_in[l], w_ffn_out[l], ln2_g[l], ln2_b[l])
        qT_na, k_na, vT_na, qT_mla, k_mla, vT_mla, gates = _proj_call(x2, w, n_tok, seq)
        oT_na = _na_call(qT_na, k_na, vT_na, w["na_bias"], batch, seq)
        oT_mla = _mla_call(qT_mla, k_mla, vT_mla, batch, seq)
        x1 = _merge_call(x2, oT_na, oT_mla, gates, w, alpha)
        x2 = _ffn_call(x1, w, alpha)
    return x2.reshape(batch, seq, d)
```

```python
import functools

import numpy as np
import jax
import jax.numpy as jnp
from jax import lax
from jax.experimental import pallas as pl
from jax.experimental.pallas import tpu as pltpu

F32 = jnp.float32
BF16 = jnp.bfloat16

D_MODEL = 1024
GRID_W = 64
NA_HEADS = 8
NA_HEAD_DIM = 64
NA_WIDTH = NA_HEADS * NA_HEAD_DIM
NA_WIN_ROWS = 8
NA_WIN_COLS = 16
MLA_HEADS = 8
MLA_Q_LORA = 256
MLA_KV_LORA = 128
MLA_NOPE = 64
MLA_ROPE = 32
MLA_QK = MLA_NOPE + MLA_ROPE
MLA_V = 64
MLA_WIDTH = MLA_HEADS * MLA_V
ROPE_THETA = 10000.0
FFN_HIDDEN = 2816
LN_EPS = 1e-5
RMS_EPS = 1e-6

LANES = 128
MLA_HEAD_PAD = LANES
VMEM_LIMIT = 56 * 1024 * 1024

PROJ_TM = 512
NA_Q_ROWS = 4
NA_KEY_ROWS = 12
NA_BLOCK_ROWS = 8
MLA_TQ = 256
MLA_KC = 512
HEAD_V = 64
SUM_ROWS = 16
V_EXT = HEAD_V + SUM_ROWS
LOG2_E = 1.4426950408889634
MERGE_TM = 512
FFN_TM = 512
FFN_TF = 256

NEG_BIG = -1e30

NT = (((1,), (1,)), ((), ()))
TN = (((0,), (0,)), ((), ()))


def _dot(a, b):
    return jnp.dot(a, b, preferred_element_type=F32)


def _dot_nt(a, b):
    return lax.dot_general(a, b, NT, preferred_element_type=F32)


def _dot_tn(a, b):
    return lax.dot_general(a, b, TN, preferred_element_type=F32)


def _resident(shape):
    nd = len(shape)
    return pl.BlockSpec(shape, lambda *_: (0,) * nd, pipeline_mode=pl.Buffered(1))


def _rms(v, g):
    return v * lax.rsqrt(jnp.mean(v * v, axis=-1, keepdims=True) + RMS_EPS) * g


def _store_v_ext(ref, vT):
    ones = jnp.ones((SUM_ROWS, vT.shape[1]), BF16)
    for h in range(vT.shape[0] // HEAD_V):
        ref[h * V_EXT:h * V_EXT + HEAD_V, :] = vT[h * HEAD_V:(h + 1) * HEAD_V].astype(BF16)
        ref[h * V_EXT + HEAD_V:(h + 1) * V_EXT, :] = ones


def _proj_kernel(x_ref, wqT_ref, wk_ref, wvT_ref, wlat_ref, wg_ref, qg_ref, kvg_ref,
                 wuqT_ref, wuk_ref, e_ref, wuvT_ref, cn_ref, sn_ref, ctf_ref, stf_ref,
                 qT_na_ref, k_na_ref, vT_na_ref, qT_mla_ref, k_mla_ref, vT_mla_ref, g_ref):
    xb = x_ref[...].astype(BF16)
    qT_na_ref[...] = _dot_nt(wqT_ref[...], xb).astype(BF16)
    k_na_ref[...] = _dot(xb, wk_ref[...]).astype(BF16)
    _store_v_ext(vT_na_ref, _dot_nt(wvT_ref[...], xb))
    g_ref[...] = _dot(xb, wg_ref[...]).astype(BF16)

    lat = _dot(xb, wlat_ref[...])
    qn = _rms(lat[:, :MLA_Q_LORA], qg_ref[...]).astype(BF16)
    kvn = _rms(lat[:, MLA_Q_LORA:MLA_Q_LORA + MLA_KV_LORA], kvg_ref[...]).astype(BF16)
    kpe = lat[:, 384:512] * cn_ref[...] + lat[:, 512:640] * sn_ref[...]
    k_mla = _dot(kvn, wuk_ref[...]) + _dot(kpe.astype(BF16), e_ref[...])
    k_mla_ref[...] = k_mla.astype(BF16)
    _store_v_ext(vT_mla_ref, _dot_nt(wuvT_ref[...], kvn))

    qT = _dot_nt(wuqT_ref[...], qn)
    ctf = ctf_ref[...]
    stf = stf_ref[...]
    for h in range(MLA_HEADS):
        qh = qT[h * MLA_HEAD_PAD:(h + 1) * MLA_HEAD_PAD]
        sw = jnp.concatenate([qh[MLA_ROPE:], qh[:MLA_ROPE]], axis=0)
        qT_mla_ref[h * MLA_HEAD_PAD:(h + 1) * MLA_HEAD_PAD, :] = (qh * ctf + sw * stf).astype(BF16)


def _proj_call(x2, w, n_tok, seq):
    tm = PROJ_TM
    steps_per_seq = seq // tm
    tok = lambda i: (i, 0)
    tokT = lambda i: (0, i)
    tab = lambda i: (i % steps_per_seq, 0)
    tabT = lambda i: (0, i % steps_per_seq)
    in_specs = [
        pl.BlockSpec((tm, D_MODEL), tok),
        _resident(w["wqT"].shape), _resident(w["wk"].shape), _resident(w["wvT"].shape),
        _resident(w["wlat"].shape), _resident(w["wg"].shape),
        _resident(w["qg"].shape), _resident(w["kvg"].shape),
        _resident(w["wuqT"].shape), _resident(w["wuk"].shape), _resident(w["e"].shape),
        _resident(w["wuvT"].shape),
        pl.BlockSpec((tm, LANES), tab), pl.BlockSpec((tm, LANES), tab),
        pl.BlockSpec((MLA_HEAD_PAD, tm), tabT), pl.BlockSpec((MLA_HEAD_PAD, tm), tabT),
    ]
    out_shape = [
        jax.ShapeDtypeStruct((NA_WIDTH, n_tok), BF16),
        jax.ShapeDtypeStruct((n_tok, NA_WIDTH), BF16),
        jax.ShapeDtypeStruct((NA_HEADS * V_EXT, n_tok), BF16),
        jax.ShapeDtypeStruct((MLA_HEADS * MLA_HEAD_PAD, n_tok), BF16),
        jax.ShapeDtypeStruct((n_tok, MLA_HEADS * MLA_HEAD_PAD), BF16),
        jax.ShapeDtypeStruct((MLA_HEADS * V_EXT, n_tok), BF16),
        jax.ShapeDtypeStruct((n_tok, 2 * D_MODEL), BF16),
    ]
    out_specs = [
        pl.BlockSpec((NA_WIDTH, tm), tokT),
        pl.BlockSpec((tm, NA_WIDTH), tok),
        pl.BlockSpec((NA_HEADS * V_EXT, tm), tokT),
        pl.BlockSpec((MLA_HEADS * MLA_HEAD_PAD, tm), tokT),
        pl.BlockSpec((tm, MLA_HEADS * MLA_HEAD_PAD), tok),
        pl.BlockSpec((MLA_HEADS * V_EXT, tm), tokT),
        pl.BlockSpec((tm, 2 * D_MODEL), tok),
    ]
    return pl.pallas_call(
        _proj_kernel,
        grid=(n_tok // tm,),
        in_specs=in_specs,
        out_specs=out_specs,
        out_shape=out_shape,
        compiler_params=pltpu.CompilerParams(
            dimension_semantics=("arbitrary",), vmem_limit_bytes=VMEM_LIMIT),
        name="proj",
    )(x2, w["wqT"], w["wk"], w["wvT"], w["wlat"], w["wg"], w["qg"], w["kvg"],
      w["wuqT"], w["wuk"], w["e"], w["wuvT"], w["cn"], w["sn"], w["ctf"], w["stf"])


NA_GROUP_TYPES = 3


def _na_group_geometry(rows):
    return [(0, 0), (NA_Q_ROWS, 0), (rows - NA_Q_ROWS, rows - NA_KEY_ROWS)]


def _na_bias_table(rpb, rows):
    n_dr = 2 * NA_WIN_ROWS - 1
    n_dc = 2 * NA_WIN_COLS - 1
    w = GRID_W
    lead = w - NA_WIN_COLS
    v = jnp.pad(rpb, ((0, 0), (0, 0), (lead, 2 * w - lead - n_dc)))
    a = jnp.tile(v, (1, 1, w))[:, :, :w * (2 * w - 1)].reshape(NA_HEADS, n_dr, w, 2 * w - 1)
    toe = jnp.swapaxes(a[:, :, :, w - 1:], 2, 3)
    col = np.arange(w)
    col_start = np.clip(col - NA_WIN_COLS // 2, 0, w - NA_WIN_COLS)
    v_col = (col[:, None] >= col_start[None, :]) & (col[:, None] < col_start[None, :] + NA_WIN_COLS)
    toe = jnp.where(v_col[None, None], toe * LOG2_E, NEG_BIG)
    types = []
    for r0, ws in _na_group_geometry(rows):
        q_cols = []
        for qi in range(NA_Q_ROWS):
            q_row = r0 + qi
            r_start = min(max(q_row - NA_WIN_ROWS // 2, 0), rows - NA_WIN_ROWS)
            kr_lo = r_start - ws
            dr_lo = r_start - q_row + NA_WIN_ROWS - 1
            run = toe[:, dr_lo:dr_lo + NA_WIN_ROWS]
            q_cols.append(jnp.pad(run, ((0, 0), (kr_lo, NA_KEY_ROWS - NA_WIN_ROWS - kr_lo), (0, 0), (0, 0)),
                                  constant_values=NEG_BIG))
        types.append(jnp.stack(q_cols, axis=3))
    bias = jnp.stack(types, axis=1)
    return bias.reshape(NA_HEADS * NA_GROUP_TYPES, NA_KEY_ROWS * w, NA_Q_ROWS * w)


def _na_kernel(qT_ref, k_ref, vT_ref, bias_ref, oT_ref, *, rows):
    blk = pl.program_id(1)
    nk = NA_KEY_ROWS * GRID_W
    nq = NA_Q_ROWS * GRID_W
    row_id = lax.broadcasted_iota(jnp.int32, (2 * NA_HEAD_DIM, nq), 0)
    items = [(g, h) for g in range(NA_BLOCK_ROWS // NA_Q_ROWS) for h in range(NA_HEADS)]

    def window(g):
        r0 = blk * NA_BLOCK_ROWS + g * NA_Q_ROWS
        ws = jnp.clip(r0 - NA_WIN_ROWS // 2, 0, rows - NA_KEY_ROWS)
        typ = jnp.where(r0 == 0, 0, jnp.where(r0 == rows - NA_Q_ROWS, 2, 1))
        return pl.multiple_of(ws * GRID_W, 2 * LANES), typ

    def scores(g, h):
        tok0, typ = window(g)
        hp, sub = divmod(h, 2)
        lanes = slice(hp * LANES, (hp + 1) * LANES)
        k_pair = k_ref[pl.ds(tok0, nk), lanes]
        q_pair = qT_ref[lanes, g * nq:(g + 1) * nq]
        keep = (row_id >= sub * NA_HEAD_DIM) & (row_id < (sub + 1) * NA_HEAD_DIM)
        q_h = jnp.where(keep, q_pair, jnp.zeros_like(q_pair))
        s = (_dot(k_pair, q_h) + bias_ref[h * NA_GROUP_TYPES + typ]).astype(BF16)
        return s, jnp.max(s, axis=0, keepdims=True)

    def finish(g, h, s, m):
        tok0, _ = window(g)
        hv = slice(h * NA_HEAD_DIM, (h + 1) * NA_HEAD_DIM)
        p = jnp.exp2(s - m)
        acc = _dot(vT_ref[h * V_EXT:(h + 1) * V_EXT, pl.ds(tok0, nk)], p)
        o = acc[:NA_HEAD_DIM] * (1.0 / acc[NA_HEAD_DIM:NA_HEAD_DIM + 1])
        oT_ref[hv, g * nq:(g + 1) * nq] = o.astype(BF16)

    cur = scores(*items[0])
    for i, item in enumerate(items):
        nxt = scores(*items[i + 1]) if i + 1 < len(items) else None
        finish(*item, *cur)
        cur = nxt


def _na_call(qT, k, vT, bias, batch, seq):
    rows = seq // GRID_W
    blocks = rows // NA_BLOCK_ROWS
    tq = NA_BLOCK_ROWS * GRID_W
    n_tok = batch * seq
    return pl.pallas_call(
        functools.partial(_na_kernel, rows=rows),
        grid=(batch, blocks),
        in_specs=[
            pl.BlockSpec((NA_WIDTH, tq), lambda b, i: (0, b * blocks + i)),
            pl.BlockSpec((seq, NA_WIDTH), lambda b, i: (b, 0)),
            pl.BlockSpec((NA_HEADS * V_EXT, seq), lambda b, i: (0, b)),
            _resident(bias.shape),
        ],
        out_specs=pl.BlockSpec((NA_WIDTH, tq), lambda b, i: (0, b * blocks + i)),
        out_shape=jax.ShapeDtypeStruct((NA_WIDTH, n_tok), BF16),
        compiler_params=pltpu.CompilerParams(
            dimension_semantics=("arbitrary", "arbitrary"), vmem_limit_bytes=VMEM_LIMIT),
        name="na",
    )(qT, k, vT, bias)


def _mla_kernel(qT_ref, k_ref, vT_ref, oT_ref, s_ref, *, seq):
    n_chunks = seq // MLA_KC
    tq = oT_ref.shape[1]

    def score_chunk(h, c, slot, m):
        hk = slice(h * MLA_HEAD_PAD, (h + 1) * MLA_HEAD_PAD)
        s = _dot(k_ref[c * MLA_KC:(c + 1) * MLA_KC, hk], qT_ref[hk, :]).astype(BF16)
        s_ref[slot, c * MLA_KC:(c + 1) * MLA_KC, :] = s
        return jnp.maximum(m, jnp.max(s, axis=0, keepdims=True))

    neg = jnp.full((1, tq), -jnp.inf, BF16)
    m_next = neg
    for c in range(n_chunks):
        m_next = score_chunk(0, c, 0, m_next)
    for h in range(MLA_HEADS):
        slot = h % 2
        m_cur, m_next = m_next, neg
        hv = slice(h * MLA_V, (h + 1) * MLA_V)
        acc = jnp.zeros((MLA_V + SUM_ROWS, tq), F32)
        for c in range(n_chunks):
            if h + 1 < MLA_HEADS:
                m_next = score_chunk(h + 1, c, 1 - slot, m_next)
            cs = slice(c * MLA_KC, (c + 1) * MLA_KC)
            p = jnp.exp2(s_ref[slot, cs, :] - m_cur)
            acc = acc + _dot(vT_ref[h * V_EXT:(h + 1) * V_EXT, cs], p)
        oT_ref[hv, :] = (acc[:MLA_V] * (1.0 / acc[MLA_V:MLA_V + 1])).astype(BF16)


def _mla_call(qT, k, vT, batch, seq):
    nq = seq // MLA_TQ
    n_tok = batch * seq
    return pl.pallas_call(
        functools.partial(_mla_kernel, seq=seq),
        grid=(batch, nq),
        in_specs=[
            pl.BlockSpec((MLA_HEADS * MLA_HEAD_PAD, MLA_TQ), lambda b, i: (0, b * nq + i)),
            pl.BlockSpec((seq, MLA_HEADS * MLA_HEAD_PAD), lambda b, i: (b, 0)),
            pl.BlockSpec((MLA_HEADS * V_EXT, seq), lambda b, i: (0, b)),
        ],
        out_specs=pl.BlockSpec((MLA_WIDTH, MLA_TQ), lambda b, i: (0, b * nq + i)),
        out_shape=jax.ShapeDtypeStruct((MLA_WIDTH, n_tok), BF16),
        scratch_shapes=[pltpu.VMEM((2, seq, MLA_TQ), BF16)],
        compiler_params=pltpu.CompilerParams(
            dimension_semantics=("arbitrary", "arbitrary"), vmem_limit_bytes=VMEM_LIMIT),
        name="mla",
    )(qT, k, vT)


def _layer_norm(v, g, b):
    mu = jnp.mean(v, axis=-1, keepdims=True)
    c = v - mu
    var = jnp.mean(c * c, axis=-1, keepdims=True)
    return c * lax.rsqrt(var + LN_EPS) * g + b


def _merge_kernel(x_ref, oa_ref, ob_ref, g_ref, wa_ref, wb_ref, wo_ref, bg_ref, lg_ref, lb_ref,
                  o_ref, *, alpha):
    y_a = _dot_tn(oa_ref[...], wa_ref[...])
    y_b = _dot_tn(ob_ref[...], wb_ref[...])
    g = g_ref[...].astype(F32) + bg_ref[...]
    merged = jax.nn.sigmoid(g[:, :D_MODEL]) * y_a + jax.nn.sigmoid(g[:, D_MODEL:]) * y_b
    mix = _dot(merged.astype(BF16), wo_ref[...])
    o_ref[...] = _layer_norm(alpha * x_ref[...] + mix, lg_ref[...], lb_ref[...])


def _merge_call(x2, oTa, oTb, gates, w, alpha):
    n_tok = x2.shape[0]
    tm = MERGE_TM
    tok = lambda i: (i, 0)
    tokT = lambda i: (0, i)
    return pl.pallas_call(
        functools.partial(_merge_kernel, alpha=alpha),
        grid=(n_tok // tm,),
        in_specs=[
            pl.BlockSpec((tm, D_MODEL), tok),
            pl.BlockSpec((NA_WIDTH, tm), tokT),
            pl.BlockSpec((MLA_WIDTH, tm), tokT),
            pl.BlockSpec((tm, 2 * D_MODEL), tok),
            _resident(w["wa"].shape), _resident(w["wb"].shape), _resident(w["wo"].shape),
            _resident(w["bg"].shape), _resident(w["ln1g"].shape), _resident(w["ln1b"].shape),
        ],
        out_specs=pl.BlockSpec((tm, D_MODEL), tok),
        out_shape=jax.ShapeDtypeStruct((n_tok, D_MODEL), F32),
        compiler_params=pltpu.CompilerParams(
            dimension_semantics=("arbitrary",), vmem_limit_bytes=VMEM_LIMIT),
        name="merge",
    )(x2, oTa, oTb, gates, w["wa"], w["wb"], w["wo"], w["bg"], w["ln1g"], w["ln1b"])


def _ffn_kernel(x_ref, wgate_ref, wup_ref, wdown_ref, lg_ref, lb_ref, o_ref, *, alpha):
    x = x_ref[...]
    xb = x.astype(BF16)
    acc = alpha * x
    for c in range(FFN_HIDDEN // FFN_TF):
        cs = slice(c * FFN_TF, (c + 1) * FFN_TF)
        gate = _dot(xb, wgate_ref[:, cs])
        up = _dot(xb, wup_ref[:, cs])
        act = (gate * jax.nn.sigmoid(gate) * up).astype(BF16)
        acc = acc + _dot(act, wdown_ref[cs, :])
    o_ref[...] = _layer_norm(acc, lg_ref[...], lb_ref[...])


def _ffn_call(x1, w, alpha):
    n_tok = x1.shape[0]
    tm = FFN_TM
    tok = lambda i: (i, 0)
    return pl.pallas_call(
        functools.partial(_ffn_kernel, alpha=alpha),
        grid=(n_tok // tm,),
        in_specs=[
            pl.BlockSpec((tm, D_MODEL), tok),
            _resident(w["wgate"].shape), _resident(w["wup"].shape), _resident(w["wdown"].shape),
            _resident(w["ln2g"].shape), _resident(w["ln2b"].shape),
        ],
        out_specs=pl.BlockSpec((tm, D_MODEL), tok),
        out_shape=jax.ShapeDtypeStruct((n_tok, D_MODEL), F32),
        compiler_params=pltpu.CompilerParams(
            dimension_semantics=("arbitrary",), vmem_limit_bytes=VMEM_LIMIT),
        name="ffn",
    )(x1, w["wgate"], w["wup"], w["wdown"], w["ln2g"], w["ln2b"])


def _rope_swap(cols):
    q = MLA_ROPE // 4
    r1, r2, c1, c2 = cols[..., :q], cols[..., q:2 * q], cols[..., 2 * q:3 * q], cols[..., 3 * q:]
    return jnp.concatenate([-r2, r1, -c2, c1], axis=-1)


def _rope_tables(seq):
    t = jnp.arange(seq)
    rows = (t // GRID_W).astype(F32)
    cols = (t % GRID_W).astype(F32)
    half = MLA_ROPE // 2
    inv_freq = ROPE_THETA ** (-jnp.arange(0, half, 2, dtype=F32) / half)
    ang_r = rows[:, None] * inv_freq
    ang_c = cols[:, None] * inv_freq
    cos = jnp.concatenate([jnp.cos(ang_r)] * 2 + [jnp.cos(ang_c)] * 2, axis=-1)
    sin = jnp.concatenate([jnp.sin(ang_r)] * 2 + [jnp.sin(ang_c)] * 2, axis=-1)
    return cos, sin


def _prep_layer(seq, w_in, b_gate, na_rpb, q_norm, w_uq, kv_norm, w_ukv, w_bna, w_bmla, w_out,
                ln1_g, ln1_b, w_ffn_in, w_ffn_out, ln2_g, ln2_b):
    o_q, o_k, o_v = 0, NA_WIDTH, 2 * NA_WIDTH
    o_ql = 3 * NA_WIDTH
    o_kvl = o_ql + MLA_Q_LORA
    o_kr = o_kvl + MLA_KV_LORA
    o_g = o_kr + MLA_ROPE
    w = {}
    na_scale = NA_HEAD_DIM ** -0.5 * LOG2_E
    w["wqT"] = (w_in[:, o_q:o_k] * na_scale).T.astype(BF16)
    w["wk"] = w_in[:, o_k:o_v].astype(BF16)
    w["wvT"] = w_in[:, o_v:o_ql].T.astype(BF16)
    k_rope_w = w_in[:, o_kr:o_g]
    pad = jnp.zeros((D_MODEL, LANES - MLA_ROPE), F32)
    w["wlat"] = jnp.concatenate(
        [w_in[:, o_ql:o_kr], k_rope_w, pad, _rope_swap(k_rope_w), pad], axis=1).astype(BF16)
    w["wg"] = w_in[:, o_g:].astype(BF16)
    w["qg"] = q_norm.reshape(1, MLA_Q_LORA)
    w["kvg"] = kv_norm.reshape(1, MLA_KV_LORA)

    uq = w_uq.reshape(MLA_Q_LORA, MLA_HEADS, MLA_QK)
    uq_pe = uq[:, :, MLA_NOPE:]
    uq_arr = jnp.concatenate([uq[:, :, :MLA_NOPE], uq_pe, _rope_swap(uq_pe)], axis=-1)
    w["wuqT"] = uq_arr.reshape(MLA_Q_LORA, MLA_HEADS * MLA_HEAD_PAD).T.astype(BF16)
    ukv = w_ukv.reshape(MLA_KV_LORA, MLA_HEADS, MLA_NOPE + MLA_V)
    uk_arr = jnp.concatenate(
        [ukv[:, :, :MLA_NOPE], jnp.zeros((MLA_KV_LORA, MLA_HEADS, MLA_HEAD_PAD - MLA_NOPE), F32)], axis=-1)
    w["wuk"] = uk_arr.reshape(MLA_KV_LORA, MLA_HEADS * MLA_HEAD_PAD).astype(BF16)
    w["wuvT"] = ukv[:, :, MLA_NOPE:].reshape(MLA_KV_LORA, MLA_WIDTH).T.astype(BF16)
    e = np.zeros((LANES, MLA_HEADS * MLA_HEAD_PAD), np.float32)
    for h in range(MLA_HEADS):
        e[np.arange(MLA_ROPE), h * MLA_HEAD_PAD + MLA_NOPE + np.arange(MLA_ROPE)] = 1.0
    w["e"] = jnp.asarray(e, BF16)

    cos, sin = _rope_tables(seq)
    zpad = jnp.zeros((seq, LANES - MLA_ROPE), F32)
    w["cn"] = jnp.concatenate([cos, zpad], axis=1)
    w["sn"] = jnp.concatenate([sin, zpad], axis=1)
    q_scale = MLA_QK ** -0.5 * LOG2_E
    ones = jnp.ones((MLA_NOPE, seq), F32)
    zer = jnp.zeros((MLA_NOPE, seq), F32)
    zer_r = jnp.zeros((MLA_ROPE, seq), F32)
    w["ctf"] = jnp.concatenate([ones, cos.T, zer_r], axis=0) * q_scale
    w["stf"] = jnp.concatenate([zer, sin.T, zer_r], axis=0) * q_scale

    w["na_bias"] = _na_bias_table(na_rpb, seq // GRID_W)

    w["wa"] = w_bna.astype(BF16)
    w["wb"] = w_bmla.astype(BF16)
    w["wo"] = w_out.astype(BF16)
    w["bg"] = b_gate.reshape(1, 2 * D_MODEL)
    w["ln1g"] = ln1_g.reshape(1, D_MODEL)
    w["ln1b"] = ln1_b.reshape(1, D_MODEL)
    w["wgate"] = w_ffn_in[:, :FFN_HIDDEN].astype(BF16)
    w["wup"] = w_ffn_in[:, FFN_HIDDEN:].astype(BF16)
    w["wdown"] = w_ffn_out.astype(BF16)
    w["ln2g"] = ln2_g.reshape(1, D_MODEL)
    w["ln2b"] = ln2_b.reshape(1, D_MODEL)
    return w


def kernel(x, w_in, b_gate, na_rpb, mla_q_norm, mla_w_uq, mla_kv_norm, mla_w_ukv, w_branch_na, w_branch_mla, w_out, ln1_g, ln1_b, w_ffn_in, w_ffn_out, ln2_g, ln2_b):
    batch, seq, d = x.shape
    depth = w_in.shape[0]
    alpha = (2.0 * depth) ** 0.25
    n_tok = batch * seq
    x2 = x.reshape(n_tok, d)
    for l in range(depth):
        w = _prep_layer(seq, w_in[l], b_gate[l], na_rpb[l], mla_q_norm[l], mla_w_uq[l], mla_kv_norm[l],
                        mla_w_ukv[l], w_branch_na[l], w_branch_mla[l], w_out[l], ln1_g[l], ln1_b[l],
                        w_ffn_in[l], w_ffn_out[l], ln2_g[l], ln2_b[l])
        qT_na, k_na, vT_na, qT_mla, k_mla, vT_mla, gates = _proj_call(x2, w, n_tok, seq)
        oT_na = _na_call(qT_na, k_na, vT_na, w["na_bias"], batch, seq)
        oT_mla = _mla_call(qT_mla, k_mla, vT_mla, batch, seq)
        x1 = _merge_call(x2, oT_na, oT_mla, gates, w, alpha)
        x2 = _ffn_call(x1, w, alpha)
    return x2.reshape(batch, seq, d)
```

```python
import functools

import numpy as np
import jax
import jax.numpy as jnp
from jax import lax
from jax.experimental import pallas as pl
from jax.experimental.pallas import tpu as pltpu

F32 = jnp.float32
BF16 = jnp.bfloat16

D_MODEL = 1024
GRID_W = 64
NA_HEADS = 8
NA_HEAD_DIM = 64
NA_WIDTH = NA_HEADS * NA_HEAD_DIM
NA_WIN_ROWS = 8
NA_WIN_COLS = 16
MLA_HEADS = 8
MLA_Q_LORA = 256
MLA_KV_LORA = 128
MLA_NOPE = 64
MLA_ROPE = 32
MLA_QK = MLA_NOPE + MLA_ROPE
MLA_V = 64
MLA_WIDTH = MLA_HEADS * MLA_V
ROPE_THETA = 10000.0
FFN_HIDDEN = 2816
LN_EPS = 1e-5
RMS_EPS = 1e-6

LANES = 128
MLA_HEAD_PAD = LANES
VMEM_LIMIT = 56 * 1024 * 1024

PROJ_TM = 512
NA_Q_ROWS = 4
NA_KEY_ROWS = 12
NA_BLOCK_ROWS = 8
MLA_TQ = 256
MLA_KC = 512
HEAD_V = 64
SUM_ROWS = 16
V_EXT = HEAD_V + SUM_ROWS
LOG2_E = 1.4426950408889634
MERGE_TM = 512
FFN_TM = 512
FFN_TF = 256

NEG_BIG = -1e30

NT = (((1,), (1,)), ((), ()))
TN = (((0,), (0,)), ((), ()))


def _dot(a, b):
    return jnp.dot(a, b, preferred_element_type=F32)


def _dot_nt(a, b):
    return lax.dot_general(a, b, NT, preferred_element_type=F32)


def _dot_tn(a, b):
    return lax.dot_general(a, b, TN, preferred_element_type=F32)


def _resident(shape):
    nd = len(shape)
    return pl.BlockSpec(shape, lambda *_: (0,) * nd, pipeline_mode=pl.Buffered(1))


def _rms(v, g):
    return v * lax.rsqrt(jnp.mean(v * v, axis=-1, keepdims=True) + RMS_EPS) * g


def _store_v_ext(ref, vT):
    ones = jnp.ones((SUM_ROWS, vT.shape[1]), BF16)
    for h in range(vT.shape[0] // HEAD_V):
        ref[h * V_EXT:h * V_EXT + HEAD_V, :] = vT[h * HEAD_V:(h + 1) * HEAD_V].astype(BF16)
        ref[h * V_EXT + HEAD_V:(h + 1) * V_EXT, :] = ones


def _proj_kernel(x_ref, wqT_ref, wk_ref, wvT_ref, wlat_ref, wg_ref, qg_ref, kvg_ref,
                 wuqT_ref, wuk_ref, e_ref, wuvT_ref, cn_ref, sn_ref, ctf_ref, stf_ref,
                 qT_na_ref, k_na_ref, vT_na_ref, qT_mla_ref, k_mla_ref, vT_mla_ref, g_ref):
    xb = x_ref[...].astype(BF16)
    qT_na_ref[...] = _dot_nt(wqT_ref[...], xb).astype(BF16)
    k_na_ref[...] = _dot(xb, wk_ref[...]).astype(BF16)
    _store_v_ext(vT_na_ref, _dot_nt(wvT_ref[...], xb))
    g_ref[...] = _dot(xb, wg_ref[...]).astype(BF16)

    lat = _dot(xb, wlat_ref[...])
    qn = _rms(lat[:, :MLA_Q_LORA], qg_ref[...]).astype(BF16)
    kvn = _rms(lat[:, MLA_Q_LORA:MLA_Q_LORA + MLA_KV_LORA], kvg_ref[...]).astype(BF16)
    kpe = lat[:, 384:512] * cn_ref[...] + lat[:, 512:640] * sn_ref[...]
    k_mla = _dot(kvn, wuk_ref[...]) + _dot(kpe.astype(BF16), e_ref[...])
    for h in range(MLA_HEADS):
        k_mla_ref[h] = k_mla[:, h * MLA_HEAD_PAD:(h + 1) * MLA_HEAD_PAD].astype(BF16)
    _store_v_ext(vT_mla_ref, _dot_nt(wuvT_ref[...], kvn))

    qT = _dot_nt(wuqT_ref[...], qn)
    ctf = ctf_ref[...]
    stf = stf_ref[...]
    for h in range(MLA_HEADS):
        qh = qT[h * MLA_HEAD_PAD:(h + 1) * MLA_HEAD_PAD]
        sw = jnp.concatenate([qh[MLA_ROPE:], qh[:MLA_ROPE]], axis=0)
        qT_mla_ref[h * MLA_HEAD_PAD:(h + 1) * MLA_HEAD_PAD, :] = (qh * ctf + sw * stf).astype(BF16)


def _proj_call(x2, w, n_tok, seq):
    tm = PROJ_TM
    steps_per_seq = seq // tm
    tok = lambda i: (i, 0)
    tokT = lambda i: (0, i)
    tab = lambda i: (i % steps_per_seq, 0)
    tabT = lambda i: (0, i % steps_per_seq)
    in_specs = [
        pl.BlockSpec((tm, D_MODEL), tok),
        _resident(w["wqT"].shape), _resident(w["wk"].shape), _resident(w["wvT"].shape),
        _resident(w["wlat"].shape), _resident(w["wg"].shape),
        _resident(w["qg"].shape), _resident(w["kvg"].shape),
        _resident(w["wuqT"].shape), _resident(w["wuk"].shape), _resident(w["e"].shape),
        _resident(w["wuvT"].shape),
        pl.BlockSpec((tm, LANES), tab), pl.BlockSpec((tm, LANES), tab),
        pl.BlockSpec((MLA_HEAD_PAD, tm), tabT), pl.BlockSpec((MLA_HEAD_PAD, tm), tabT),
    ]
    out_shape = [
        jax.ShapeDtypeStruct((NA_WIDTH, n_tok), BF16),
        jax.ShapeDtypeStruct((n_tok, NA_WIDTH), BF16),
        jax.ShapeDtypeStruct((NA_HEADS * V_EXT, n_tok), BF16),
        jax.ShapeDtypeStruct((MLA_HEADS * MLA_HEAD_PAD, n_tok), BF16),
        jax.ShapeDtypeStruct((MLA_HEADS, n_tok, MLA_HEAD_PAD), BF16),
        jax.ShapeDtypeStruct((MLA_HEADS * V_EXT, n_tok), BF16),
        jax.ShapeDtypeStruct((n_tok, 2 * D_MODEL), BF16),
    ]
    out_specs = [
        pl.BlockSpec((NA_WIDTH, tm), tokT),
        pl.BlockSpec((tm, NA_WIDTH), tok),
        pl.BlockSpec((NA_HEADS * V_EXT, tm), tokT),
        pl.BlockSpec((MLA_HEADS * MLA_HEAD_PAD, tm), tokT),
        pl.BlockSpec((MLA_HEADS, tm, MLA_HEAD_PAD), lambda i: (0, i, 0)),
        pl.BlockSpec((MLA_HEADS * V_EXT, tm), tokT),
        pl.BlockSpec((tm, 2 * D_MODEL), tok),
    ]
    return pl.pallas_call(
        _proj_kernel,
        grid=(n_tok // tm,),
        in_specs=in_specs,
        out_specs=out_specs,
        out_shape=out_shape,
        compiler_params=pltpu.CompilerParams(
            dimension_semantics=("arbitrary",), vmem_limit_bytes=VMEM_LIMIT),
        name="proj",
    )(x2, w["wqT"], w["wk"], w["wvT"], w["wlat"], w["wg"], w["qg"], w["kvg"],
      w["wuqT"], w["wuk"], w["e"], w["wuvT"], w["cn"], w["sn"], w["ctf"], w["stf"])


NA_GROUP_TYPES = 3


def _na_group_geometry(rows):
    return [(0, 0), (NA_Q_ROWS, 0), (rows - NA_Q_ROWS, rows - NA_KEY_ROWS)]


def _na_bias_table(rpb, rows):
    n_dr = 2 * NA_WIN_ROWS - 1
    n_dc = 2 * NA_WIN_COLS - 1
    w = GRID_W
    lead = w - NA_WIN_COLS
    v = jnp.pad(rpb, ((0, 0), (0, 0), (lead, 2 * w - lead - n_dc)))
    a = jnp.tile(v, (1, 1, w))[:, :, :w * (2 * w - 1)].reshape(NA_HEADS, n_dr, w, 2 * w - 1)
    toe = jnp.swapaxes(a[:, :, :, w - 1:], 2, 3)
    col = np.arange(w)
    col_start = np.clip(col - NA_WIN_COLS // 2, 0, w - NA_WIN_COLS)
    v_col = (col[:, None] >= col_start[None, :]) & (col[:, None] < col_start[None, :] + NA_WIN_COLS)
    toe = jnp.where(v_col[None, None], toe * LOG2_E, NEG_BIG)
    types = []
    for r0, ws in _na_group_geometry(rows):
        q_cols = []
        for qi in range(NA_Q_ROWS):
            q_row = r0 + qi
            r_start = min(max(q_row - NA_WIN_ROWS // 2, 0), rows - NA_WIN_ROWS)
            kr_lo = r_start - ws
            dr_lo = r_start - q_row + NA_WIN_ROWS - 1
            run = toe[:, dr_lo:dr_lo + NA_WIN_ROWS]
            q_cols.append(jnp.pad(run, ((0, 0), (kr_lo, NA_KEY_ROWS - NA_WIN_ROWS - kr_lo), (0, 0), (0, 0)),
                                  constant_values=NEG_BIG))
        types.append(jnp.stack(q_cols, axis=3))
    bias = jnp.stack(types, axis=1)
    return bias.reshape(NA_HEADS * NA_GROUP_TYPES, NA_KEY_ROWS * w, NA_Q_ROWS * w)


def _na_kernel(qT_ref, k_ref, vT_ref, bias_ref, oT_ref, *, rows):
    blk = pl.program_id(1)
    nk = NA_KEY_ROWS * GRID_W
    nq = NA_Q_ROWS * GRID_W
    row_id = lax.broadcasted_iota(jnp.int32, (2 * NA_HEAD_DIM, nq), 0)
    items = [(g, h) for g in range(NA_BLOCK_ROWS // NA_Q_ROWS) for h in range(NA_HEADS)]

    def window(g):
        r0 = blk * NA_BLOCK_ROWS + g * NA_Q_ROWS
        ws = jnp.clip(r0 - NA_WIN_ROWS // 2, 0, rows - NA_KEY_ROWS)
        typ = jnp.where(r0 == 0, 0, jnp.where(r0 == rows - NA_Q_ROWS, 2, 1))
        return pl.multiple_of(ws * GRID_W, 2 * LANES), typ

    def scores(g, h):
        tok0, typ = window(g)
        hp, sub = divmod(h, 2)
        lanes = slice(hp * LANES, (hp + 1) * LANES)
        k_pair = k_ref[pl.ds(tok0, nk), lanes]
        q_pair = qT_ref[lanes, g * nq:(g + 1) * nq]
        keep = (row_id >= sub * NA_HEAD_DIM) & (row_id < (sub + 1) * NA_HEAD_DIM)
        q_h = jnp.where(keep, q_pair, jnp.zeros_like(q_pair))
        s = (_dot(k_pair, q_h) + bias_ref[h * NA_GROUP_TYPES + typ]).astype(BF16)
        return s, jnp.max(s, axis=0, keepdims=True)

    def finish(g, h, s, m):
        tok0, _ = window(g)
        hv = slice(h * NA_HEAD_DIM, (h + 1) * NA_HEAD_DIM)
        p = jnp.exp2(s - m)
        acc = _dot(vT_ref[h * V_EXT:(h + 1) * V_EXT, pl.ds(tok0, nk)], p)
        o = acc[:NA_HEAD_DIM] * (1.0 / acc[NA_HEAD_DIM:NA_HEAD_DIM + 1])
        oT_ref[hv, g * nq:(g + 1) * nq] = o.astype(BF16)

    cur = scores(*items[0])
    for i, item in enumerate(items):
        nxt = scores(*items[i + 1]) if i + 1 < len(items) else None
        finish(*item, *cur)
        cur = nxt


def _na_call(qT, k, vT, bias, batch, seq):
    rows = seq // GRID_W
    blocks = rows // NA_BLOCK_ROWS
    tq = NA_BLOCK_ROWS * GRID_W
    n_tok = batch * seq
    return pl.pallas_call(
        functools.partial(_na_kernel, rows=rows),
        grid=(batch, blocks),
        in_specs=[
            pl.BlockSpec((NA_WIDTH, tq), lambda b, i: (0, b * blocks + i)),
            pl.BlockSpec((seq, NA_WIDTH), lambda b, i: (b, 0)),
            pl.BlockSpec((NA_HEADS * V_EXT, seq), lambda b, i: (0, b)),
            _resident(bias.shape),
        ],
        out_specs=pl.BlockSpec((NA_WIDTH, tq), lambda b, i: (0, b * blocks + i)),
        out_shape=jax.ShapeDtypeStruct((NA_WIDTH, n_tok), BF16),
        compiler_params=pltpu.CompilerParams(
            dimension_semantics=("arbitrary", "arbitrary"), vmem_limit_bytes=VMEM_LIMIT),
        name="na",
    )(qT, k, vT, bias)


def _mla_kernel(qT_ref, k_ref, vT_ref, oT_ref, s_ref, *, seq):
    n_chunks = seq // MLA_KC
    tq = oT_ref.shape[1]

    def score_chunk(h, c, slot, m):
        hk = slice(h * MLA_HEAD_PAD, (h + 1) * MLA_HEAD_PAD)
        s = _dot(k_ref[h, c * MLA_KC:(c + 1) * MLA_KC, :], qT_ref[hk, :]).astype(BF16)
        s_ref[slot, c * MLA_KC:(c + 1) * MLA_KC, :] = s
        return jnp.maximum(m, jnp.max(s, axis=0, keepdims=True))

    neg = jnp.full((1, tq), -jnp.inf, BF16)
    m_next = neg
    for c in range(n_chunks):
        m_next = score_chunk(0, c, 0, m_next)
    for h in range(MLA_HEADS):
        slot = h % 2
        m_cur, m_next = m_next, neg
        hv = slice(h * MLA_V, (h + 1) * MLA_V)
        acc = jnp.zeros((MLA_V + SUM_ROWS, tq), F32)
        for c in range(n_chunks):
            if h + 1 < MLA_HEADS:
                m_next = score_chunk(h + 1, c, 1 - slot, m_next)
            cs = slice(c * MLA_KC, (c + 1) * MLA_KC)
            p = jnp.exp2(s_ref[slot, cs, :] - m_cur)
            acc = acc + _dot(vT_ref[h * V_EXT:(h + 1) * V_EXT, cs], p)
        oT_ref[hv, :] = (acc[:MLA_V] * (1.0 / acc[MLA_V:MLA_V + 1])).astype(BF16)


def _mla_call(qT, k, vT, batch, seq):
    nq = seq // MLA_TQ
    n_tok = batch * seq
    return pl.pallas_call(
        functools.partial(_mla_kernel, seq=seq),
        grid=(batch, nq),
        in_specs=[
            pl.BlockSpec((MLA_HEADS * MLA_HEAD_PAD, MLA_TQ), lambda b, i: (0, b * nq + i)),
            pl.BlockSpec((MLA_HEADS, seq, MLA_HEAD_PAD), lambda b, i: (0, b, 0)),
            pl.BlockSpec((MLA_HEADS * V_EXT, seq), lambda b, i: (0, b)),
        ],
        out_specs=pl.BlockSpec((MLA_WIDTH, MLA_TQ), lambda b, i: (0, b * nq + i)),
        out_shape=jax.ShapeDtypeStruct((MLA_WIDTH, n_tok), BF16),
        scratch_shapes=[pltpu.VMEM((2, seq, MLA_TQ), BF16)],
        compiler_params=pltpu.CompilerParams(
            dimension_semantics=("arbitrary", "arbitrary"), vmem_limit_bytes=VMEM_LIMIT),
        name="mla",
    )(qT, k, vT)


def _layer_norm(v, g, b):
    mu = jnp.mean(v, axis=-1, keepdims=True)
    c = v - mu
    var = jnp.mean(c * c, axis=-1, keepdims=True)
    return c * lax.rsqrt(var + LN_EPS) * g + b


def _merge_kernel(x_ref, oa_ref, ob_ref, g_ref, wa_ref, wb_ref, wo_ref, bg_ref, lg_ref, lb_ref,
                  o_ref, *, alpha):
    y_a = _dot_tn(oa_ref[...], wa_ref[...])
    y_b = _dot_tn(ob_ref[...], wb_ref[...])
    g = g_ref[...].astype(F32) + bg_ref[...]
    merged = jax.nn.sigmoid(g[:, :D_MODEL]) * y_a + jax.nn.sigmoid(g[:, D_MODEL:]) * y_b
    mix = _dot(merged.astype(BF16), wo_ref[...])
    o_ref[...] = _layer_norm(alpha * x_ref[...] + mix, lg_ref[...], lb_ref[...])


def _merge_call(x2, oTa, oTb, gates, w, alpha):
    n_tok = x2.shape[0]
    tm = MERGE_TM
    tok = lambda i: (i, 0)
    tokT = lambda i: (0, i)
    return pl.pallas_call(
        functools.partial(_merge_kernel, alpha=alpha),
        grid=(n_tok // tm,),
        in_specs=[
            pl.BlockSpec((tm, D_MODEL), tok),
            pl.BlockSpec((NA_WIDTH, tm), tokT),
            pl.BlockSpec((MLA_WIDTH, tm), tokT),
            pl.BlockSpec((tm, 2 * D_MODEL), tok),
            _resident(w["wa"].shape), _resident(w["wb"].shape), _resident(w["wo"].shape),
            _resident(w["bg"].shape), _resident(w["ln1g"].shape), _resident(w["ln1b"].shape),
        ],
        out_specs=pl.BlockSpec((tm, D_MODEL), tok),
        out_shape=jax.ShapeDtypeStruct((n_tok, D_MODEL), F32),
        compiler_params=pltpu.CompilerParams(
            dimension_semantics=("arbitrary",), vmem_limit_bytes=VMEM_LIMIT),
        name="merge",
    )(x2, oTa, oTb, gates, w["wa"], w["wb"], w["wo"], w["bg"], w["ln1g"], w["ln1b"])


def _ffn_kernel(x_ref, wgate_ref, wup_ref, wdown_ref, lg_ref, lb_ref, o_ref, *, alpha):
    x = x_ref[...]
    xb = x.astype(BF16)
    acc = alpha * x
    for c in range(FFN_HIDDEN // FFN_TF):
        cs = slice(c * FFN_TF, (c + 1) * FFN_TF)
        gate = _dot(xb, wgate_ref[:, cs])
        up = _dot(xb, wup_ref[:, cs])
        act = (gate * jax.nn.sigmoid(gate) * up).astype(BF16)
        acc = acc + _dot(act, wdown_ref[cs, :])
    o_ref[...] = _layer_norm(acc, lg_ref[...], lb_ref[...])


def _ffn_call(x1, w, alpha):
    n_tok = x1.shape[0]
    tm = FFN_TM
    tok = lambda i: (i, 0)
    return pl.pallas_call(
        functools.partial(_ffn_kernel, alpha=alpha),
        grid=(n_tok // tm,),
        in_specs=[
            pl.BlockSpec((tm, D_MODEL), tok),
            _resident(w["wgate"].shape), _resident(w["wup"].shape), _resident(w["wdown"].shape),
            _resident(w["ln2g"].shape), _resident(w["ln2b"].shape),
        ],
        out_specs=pl.BlockSpec((tm, D_MODEL), tok),
        out_shape=jax.ShapeDtypeStruct((n_tok, D_MODEL), F32),
        compiler_params=pltpu.CompilerParams(
            dimension_semantics=("arbitrary",), vmem_limit_bytes=VMEM_LIMIT),
        name="ffn",
    )(x1, w["wgate"], w["wup"], w["wdown"], w["ln2g"], w["ln2b"])


def _rope_swap(cols):
    q = MLA_ROPE // 4
    r1, r2, c1, c2 = cols[..., :q], cols[..., q:2 * q], cols[..., 2 * q:3 * q], cols[..., 3 * q:]
    return jnp.concatenate([-r2, r1, -c2, c1], axis=-1)


def _rope_tables(seq):
    t = jnp.arange(seq)
    rows = (t // GRID_W).astype(F32)
    cols = (t % GRID_W).astype(F32)
    half = MLA_ROPE // 2
    inv_freq = ROPE_THETA ** (-jnp.arange(0, half, 2, dtype=F32) / half)
    ang_r = rows[:, None] * inv_freq
    ang_c = cols[:, None] * inv_freq
    cos = jnp.concatenate([jnp.cos(ang_r)] * 2 + [jnp.cos(ang_c)] * 2, axis=-1)
    sin = jnp.concatenate([jnp.sin(ang_r)] * 2 + [jnp.sin(ang_c)] * 2, axis=-1)
    return cos, sin


def _prep_layer(seq, w_in, b_gate, na_rpb, q_norm, w_uq, kv_norm, w_ukv, w_bna, w_bmla, w_out,
                ln1_g, ln1_b, w_ffn_in, w_ffn_out, ln2_g, ln2_b):
    o_q, o_k, o_v = 0, NA_WIDTH, 2 * NA_WIDTH
    o_ql = 3 * NA_WIDTH
    o_kvl = o_ql + MLA_Q_LORA
    o_kr = o_kvl + MLA_KV_LORA
    o_g = o_kr + MLA_ROPE
    w = {}
    na_scale = NA_HEAD_DIM ** -0.5 * LOG2_E
    w["wqT"] = (w_in[:, o_q:o_k] * na_scale).T.astype(BF16)
    w["wk"] = w_in[:, o_k:o_v].astype(BF16)
    w["wvT"] = w_in[:, o_v:o_ql].T.astype(BF16)
    k_rope_w = w_in[:, o_kr:o_g]
    pad = jnp.zeros((D_MODEL, LANES - MLA_ROPE), F32)
    w["wlat"] = jnp.concatenate(
        [w_in[:, o_ql:o_kr], k_rope_w, pad, _rope_swap(k_rope_w), pad], axis=1).astype(BF16)
    w["wg"] = w_in[:, o_g:].astype(BF16)
    w["qg"] = q_norm.reshape(1, MLA_Q_LORA)
    w["kvg"] = kv_norm.reshape(1, MLA_KV_LORA)

    uq = w_uq.reshape(MLA_Q_LORA, MLA_HEADS, MLA_QK)
    uq_pe = uq[:, :, MLA_NOPE:]
    uq_arr = jnp.concatenate([uq[:, :, :MLA_NOPE], uq_pe, _rope_swap(uq_pe)], axis=-1)
    w["wuqT"] = uq_arr.reshape(MLA_Q_LORA, MLA_HEADS * MLA_HEAD_PAD).T.astype(BF16)
    ukv = w_ukv.reshape(MLA_KV_LORA, MLA_HEADS, MLA_NOPE + MLA_V)
    uk_arr = jnp.concatenate(
        [ukv[:, :, :MLA_NOPE], jnp.zeros((MLA_KV_LORA, MLA_HEADS, MLA_HEAD_PAD - MLA_NOPE), F32)], axis=-1)
    w["wuk"] = uk_arr.reshape(MLA_KV_LORA, MLA_HEADS * MLA_HEAD_PAD).astype(BF16)
    w["wuvT"] = ukv[:, :, MLA_NOPE:].reshape(MLA_KV_LORA, MLA_WIDTH).T.astype(BF16)
    e = np.zeros((LANES, MLA_HEADS * MLA_HEAD_PAD), np.float32)
    for h in range(MLA_HEADS):
        e[np.arange(MLA_ROPE), h * MLA_HEAD_PAD + MLA_NOPE + np.arange(MLA_ROPE)] = 1.0
    w["e"] = jnp.asarray(e, BF16)

    cos, sin = _rope_tables(seq)
    zpad = jnp.zeros((seq, LANES - MLA_ROPE), F32)
    w["cn"] = jnp.concatenate([cos, zpad], axis=1)
    w["sn"] = jnp.concatenate([sin, zpad], axis=1)
    q_scale = MLA_QK ** -0.5 * LOG2_E
    ones = jnp.ones((MLA_NOPE, seq), F32)
    zer = jnp.zeros((MLA_NOPE, seq), F32)
    zer_r = jnp.zeros((MLA_ROPE, seq), F32)
    w["ctf"] = jnp.concatenate([ones, cos.T, zer_r], axis=0) * q_scale
    w["stf"] = jnp.concatenate([zer, sin.T, zer_r], axis=0) * q_scale

    w["na_bias"] = _na_bias_table(na_rpb, seq // GRID_W)

    w["wa"] = w_bna.astype(BF16)
    w["wb"] = w_bmla.astype(BF16)
    w["wo"] = w_out.astype(BF16)
    w["bg"] = b_gate.reshape(1, 2 * D_MODEL)
    w["ln1g"] = ln1_g.reshape(1, D_MODEL)
    w["ln1b"] = ln1_b.reshape(1, D_MODEL)
    w["wgate"] = w_ffn_in[:, :FFN_HIDDEN].astype(BF16)
    w["wup"] = w_ffn_in[:, FFN_HIDDEN:].astype(BF16)
    w["wdown"] = w_ffn_out.astype(BF16)
    w["ln2g"] = ln2_g.reshape(1, D_MODEL)
    w["ln2b"] = ln2_b.reshape(1, D_MODEL)
    return w


def kernel(x, w_in, b_gate, na_rpb, mla_q_norm, mla_w_uq, mla_kv_norm, mla_w_ukv, w_branch_na, w_branch_mla, w_out, ln1_g, ln1_b, w_ffn_in, w_ffn_out, ln2_g, ln2_b):
    batch, seq, d = x.shape
    depth = w_in.shape[0]
    alpha = (2.0 * depth) ** 0.25
    n_tok = batch * seq
    x2 = x.reshape(n_tok, d)
    for l in range(depth):
        w = _prep_layer(seq, w_in[l], b_gate[l], na_rpb[l], mla_q_norm[l], mla_w_uq[l], mla_kv_norm[l],
                        mla_w_ukv[l], w_branch_na[l], w_branch_mla[l], w_out[l], ln1_g[l], ln1_b[l],
                        w_ffn_in[l], w_ffn_out[l], ln2_g[l], ln2_b[l])
        qT_na, k_na, vT_na, qT_mla, k_mla, vT_mla, gates = _proj_call(x2, w, n_tok, seq)
        oT_na = _na_call(qT_na, k_na, vT_na, w["na_bias"], batch, seq)
        oT_mla = _mla_call(qT_mla, k_mla, vT_mla, batch, seq)
        x1 = _merge_call(x2, oT_na, oT_mla, gates, w, alpha)
        x2 = _ffn_call(x1, w, alpha)
    return x2.reshape(batch, seq, d)
```

```python
import functools

import numpy as np
import jax
import jax.numpy as jnp
from jax import lax
from jax.experimental import pallas as pl
from jax.experimental.pallas import tpu as pltpu

F32 = jnp.float32
BF16 = jnp.bfloat16

D_MODEL = 1024
GRID_W = 64
NA_HEADS = 8
NA_HEAD_DIM = 64
NA_WIDTH = NA_HEADS * NA_HEAD_DIM
NA_WIN_ROWS = 8
NA_WIN_COLS = 16
MLA_HEADS = 8
MLA_Q_LORA = 256
MLA_KV_LORA = 128
MLA_NOPE = 64
MLA_ROPE = 32
MLA_QK = MLA_NOPE + MLA_ROPE
MLA_V = 64
MLA_WIDTH = MLA_HEADS * MLA_V
ROPE_THETA = 10000.0
FFN_HIDDEN = 2816
LN_EPS = 1e-5
RMS_EPS = 1e-6

LANES = 128
MLA_HEAD_PAD = LANES
VMEM_LIMIT = 56 * 1024 * 1024

PROJ_TM = 512
NA_Q_ROWS = 4
NA_KEY_ROWS = 12
NA_BLOCK_ROWS = 8
MLA_TQ = 256
MLA_KC = 512
HEAD_V = 64
SUM_ROWS = 16
V_EXT = HEAD_V + SUM_ROWS
LOG2_E = 1.4426950408889634
MERGE_TM = 512
FFN_TM = 512
FFN_TF = 256

NEG_BIG = -1e30

NT = (((1,), (1,)), ((), ()))
TN = (((0,), (0,)), ((), ()))


def _dot(a, b):
    return jnp.dot(a, b, preferred_element_type=F32)


def _dot_nt(a, b):
    return lax.dot_general(a, b, NT, preferred_element_type=F32)


def _dot_tn(a, b):
    return lax.dot_general(a, b, TN, preferred_element_type=F32)


def _resident(shape):
    nd = len(shape)
    return pl.BlockSpec(shape, lambda *_: (0,) * nd, pipeline_mode=pl.Buffered(1))


def _rms(v, g):
    return v * lax.rsqrt(jnp.mean(v * v, axis=-1, keepdims=True) + RMS_EPS) * g


def _store_v_ext(ref, vT):
    ones = jnp.ones((SUM_ROWS, vT.shape[1]), BF16)
    for h in range(vT.shape[0] // HEAD_V):
        ref[h * V_EXT:h * V_EXT + HEAD_V, :] = vT[h * HEAD_V:(h + 1) * HEAD_V].astype(BF16)
        ref[h * V_EXT + HEAD_V:(h + 1) * V_EXT, :] = ones


def _proj_kernel(x_ref, wqT_ref, wk_ref, wvT_ref, wlat_ref, wg_ref, qg_ref, kvg_ref,
                 wuqT_ref, wuk_ref, e_ref, wuvT_ref, cn_ref, sn_ref, ctf_ref, stf_ref,
                 qT_na_ref, k_na_ref, vT_na_ref, qT_mla_ref, k_mla_ref, vT_mla_ref, g_ref):
    xb = x_ref[...].astype(BF16)
    qT_na_ref[...] = _dot_nt(wqT_ref[...], xb).astype(BF16)
    k_na_ref[...] = _dot(xb, wk_ref[...]).astype(BF16)
    _store_v_ext(vT_na_ref, _dot_nt(wvT_ref[...], xb))
    g_ref[...] = _dot(xb, wg_ref[...]).astype(BF16)

    lat = _dot(xb, wlat_ref[...])
    qn = _rms(lat[:, :MLA_Q_LORA], qg_ref[...]).astype(BF16)
    kvn = _rms(lat[:, MLA_Q_LORA:MLA_Q_LORA + MLA_KV_LORA], kvg_ref[...]).astype(BF16)
    kpe = lat[:, 384:512] * cn_ref[...] + lat[:, 512:640] * sn_ref[...]
    k_mla = _dot(kvn, wuk_ref[...]) + _dot(kpe.astype(BF16), e_ref[...])
    for h in range(MLA_HEADS):
        k_mla_ref[h] = k_mla[:, h * MLA_HEAD_PAD:(h + 1) * MLA_HEAD_PAD].astype(BF16)
    _store_v_ext(vT_mla_ref, _dot_nt(wuvT_ref[...], kvn))

    qT = _dot_nt(wuqT_ref[...], qn)
    ctf = ctf_ref[...]
    stf = stf_ref[...]
    for h in range(MLA_HEADS):
        qh = qT[h * MLA_HEAD_PAD:(h + 1) * MLA_HEAD_PAD]
        sw = jnp.concatenate([qh[MLA_ROPE:], qh[:MLA_ROPE]], axis=0)
        qT_mla_ref[h * MLA_HEAD_PAD:(h + 1) * MLA_HEAD_PAD, :] = (qh * ctf + sw * stf).astype(BF16)


def _proj_call(x2, w, n_tok, seq):
    tm = PROJ_TM
    steps_per_seq = seq // tm
    tok = lambda i: (i, 0)
    tokT = lambda i: (0, i)
    tab = lambda i: (i % steps_per_seq, 0)
    tabT = lambda i: (0, i % steps_per_seq)
    in_specs = [
        pl.BlockSpec((tm, D_MODEL), tok),
        _resident(w["wqT"].shape), _resident(w["wk"].shape), _resident(w["wvT"].shape),
        _resident(w["wlat"].shape), _resident(w["wg"].shape),
        _resident(w["qg"].shape), _resident(w["kvg"].shape),
        _resident(w["wuqT"].shape), _resident(w["wuk"].shape), _resident(w["e"].shape),
        _resident(w["wuvT"].shape),
        pl.BlockSpec((tm, LANES), tab), pl.BlockSpec((tm, LANES), tab),
        pl.BlockSpec((MLA_HEAD_PAD, tm), tabT), pl.BlockSpec((MLA_HEAD_PAD, tm), tabT),
    ]
    out_shape = [
        jax.ShapeDtypeStruct((NA_WIDTH, n_tok), BF16),
        jax.ShapeDtypeStruct((n_tok, NA_WIDTH), BF16),
        jax.ShapeDtypeStruct((NA_HEADS * V_EXT, n_tok), BF16),
        jax.ShapeDtypeStruct((MLA_HEADS * MLA_HEAD_PAD, n_tok), BF16),
        jax.ShapeDtypeStruct((MLA_HEADS, n_tok, MLA_HEAD_PAD), BF16),
        jax.ShapeDtypeStruct((MLA_HEADS * V_EXT, n_tok), BF16),
        jax.ShapeDtypeStruct((n_tok, 2 * D_MODEL), BF16),
    ]
    out_specs = [
        pl.BlockSpec((NA_WIDTH, tm), tokT),
        pl.BlockSpec((tm, NA_WIDTH), tok),
        pl.BlockSpec((NA_HEADS * V_EXT, tm), tokT),
        pl.BlockSpec((MLA_HEADS * MLA_HEAD_PAD, tm), tokT),
        pl.BlockSpec((MLA_HEADS, tm, MLA_HEAD_PAD), lambda i: (0, i, 0)),
        pl.BlockSpec((MLA_HEADS * V_EXT, tm), tokT),
        pl.BlockSpec((tm, 2 * D_MODEL), tok),
    ]
    return pl.pallas_call(
        _proj_kernel,
        grid=(n_tok // tm,),
        in_specs=in_specs,
        out_specs=out_specs,
        out_shape=out_shape,
        compiler_params=pltpu.CompilerParams(
            dimension_semantics=("arbitrary",), vmem_limit_bytes=VMEM_LIMIT),
        name="proj",
    )(x2, w["wqT"], w["wk"], w["wvT"], w["wlat"], w["wg"], w["qg"], w["kvg"],
      w["wuqT"], w["wuk"], w["e"], w["wuvT"], w["cn"], w["sn"], w["ctf"], w["stf"])


NA_GROUP_TYPES = 3


def _na_group_geometry(rows):
    return [(0, 0), (NA_Q_ROWS, 0), (rows - NA_Q_ROWS, rows - NA_KEY_ROWS)]


def _na_bias_table(rpb, rows):
    n_dr = 2 * NA_WIN_ROWS - 1
    n_dc = 2 * NA_WIN_COLS - 1
    w = GRID_W
    lead = w - NA_WIN_COLS
    v = jnp.pad(rpb, ((0, 0), (0, 0), (lead, 2 * w - lead - n_dc)))
    a = jnp.tile(v, (1, 1, w))[:, :, :w * (2 * w - 1)].reshape(NA_HEADS, n_dr, w, 2 * w - 1)
    toe = jnp.swapaxes(a[:, :, :, w - 1:], 2, 3)
    col = np.arange(w)
    col_start = np.clip(col - NA_WIN_COLS // 2, 0, w - NA_WIN_COLS)
    v_col = (col[:, None] >= col_start[None, :]) & (col[:, None] < col_start[None, :] + NA_WIN_COLS)
    toe = jnp.where(v_col[None, None], toe * LOG2_E, NEG_BIG)
    types = []
    for r0, ws in _na_group_geometry(rows):
        q_cols = []
        for qi in range(NA_Q_ROWS):
            q_row = r0 + qi
            r_start = min(max(q_row - NA_WIN_ROWS // 2, 0), rows - NA_WIN_ROWS)
            kr_lo = r_start - ws
            dr_lo = r_start - q_row + NA_WIN_ROWS - 1
            run = toe[:, dr_lo:dr_lo + NA_WIN_ROWS]
            q_cols.append(jnp.pad(run, ((0, 0), (kr_lo, NA_KEY_ROWS - NA_WIN_ROWS - kr_lo), (0, 0), (0, 0)),
                                  constant_values=NEG_BIG))
        types.append(jnp.stack(q_cols, axis=3))
    bias = jnp.stack(types, axis=1)
    return bias.reshape(NA_HEADS * NA_GROUP_TYPES, NA_KEY_ROWS * w, NA_Q_ROWS * w)


def _na_kernel(qT_ref, k_ref, vT_ref, bias_ref, oT_ref, *, rows):
    blk = pl.program_id(1)
    nk = NA_KEY_ROWS * GRID_W
    nq = NA_Q_ROWS * GRID_W
    row_id = lax.broadcasted_iota(jnp.int32, (2 * NA_HEAD_DIM, nq), 0)
    items = [(g, h) for g in range(NA_BLOCK_ROWS // NA_Q_ROWS) for h in range(NA_HEADS)]

    def window(g):
        r0 = blk * NA_BLOCK_ROWS + g * NA_Q_ROWS
        ws = jnp.clip(r0 - NA_WIN_ROWS // 2, 0, rows - NA_KEY_ROWS)
        typ = jnp.where(r0 == 0, 0, jnp.where(r0 == rows - NA_Q_ROWS, 2, 1))
        return pl.multiple_of(ws * GRID_W, 2 * LANES), typ

    def scores(g, h):
        tok0, typ = window(g)
        hp, sub = divmod(h, 2)
        lanes = slice(hp * LANES, (hp + 1) * LANES)
        k_pair = k_ref[pl.ds(tok0, nk), lanes]
        q_pair = qT_ref[lanes, g * nq:(g + 1) * nq]
        keep = (row_id >= sub * NA_HEAD_DIM) & (row_id < (sub + 1) * NA_HEAD_DIM)
        q_h = jnp.where(keep, q_pair, jnp.zeros_like(q_pair))
        s = (_dot(k_pair, q_h) + bias_ref[h * NA_GROUP_TYPES + typ]).astype(BF16)
        return s, jnp.max(s, axis=0, keepdims=True)

    def finish(g, h, s, m):
        tok0, _ = window(g)
        hv = slice(h * NA_HEAD_DIM, (h + 1) * NA_HEAD_DIM)
        p = jnp.exp2((s - m).astype(F32)).astype(BF16)
        acc = _dot(vT_ref[h * V_EXT:(h + 1) * V_EXT, pl.ds(tok0, nk)], p)
        o = acc[:NA_HEAD_DIM] * (1.0 / acc[NA_HEAD_DIM:NA_HEAD_DIM + 1])
        oT_ref[hv, g * nq:(g + 1) * nq] = o.astype(BF16)

    cur = scores(*items[0])
    for i, item in enumerate(items):
        nxt = scores(*items[i + 1]) if i + 1 < len(items) else None
        finish(*item, *cur)
        cur = nxt


def _na_call(qT, k, vT, bias, batch, seq):
    rows = seq // GRID_W
    blocks = rows // NA_BLOCK_ROWS
    tq = NA_BLOCK_ROWS * GRID_W
    n_tok = batch * seq
    return pl.pallas_call(
        functools.partial(_na_kernel, rows=rows),
        grid=(batch, blocks),
        in_specs=[
            pl.BlockSpec((NA_WIDTH, tq), lambda b, i: (0, b * blocks + i)),
            pl.BlockSpec((seq, NA_WIDTH), lambda b, i: (b, 0)),
            pl.BlockSpec((NA_HEADS * V_EXT, seq), lambda b, i: (0, b)),
            _resident(bias.shape),
        ],
        out_specs=pl.BlockSpec((NA_WIDTH, tq), lambda b, i: (0, b * blocks + i)),
        out_shape=jax.ShapeDtypeStruct((NA_WIDTH, n_tok), BF16),
        compiler_params=pltpu.CompilerParams(
            dimension_semantics=("arbitrary", "arbitrary"), vmem_limit_bytes=VMEM_LIMIT),
        name="na",
    )(qT, k, vT, bias)


def _mla_kernel(qT_ref, k_ref, vT_ref, oT_ref, s_ref, *, seq):
    n_chunks = seq // MLA_KC
    tq = oT_ref.shape[1]

    def score_chunk(h, c, slot, m):
        hk = slice(h * MLA_HEAD_PAD, (h + 1) * MLA_HEAD_PAD)
        s = _dot(k_ref[h, c * MLA_KC:(c + 1) * MLA_KC, :], qT_ref[hk, :]).astype(BF16)
        s_ref[slot, c * MLA_KC:(c + 1) * MLA_KC, :] = s
        return jnp.maximum(m, jnp.max(s, axis=0, keepdims=True))

    neg = jnp.full((1, tq), -jnp.inf, BF16)
    m_next = neg
    for c in range(n_chunks):
        m_next = score_chunk(0, c, 0, m_next)
    for h in range(MLA_HEADS):
        slot = h % 2
        m_cur, m_next = m_next, neg
        hv = slice(h * MLA_V, (h + 1) * MLA_V)
        acc = jnp.zeros((MLA_V + SUM_ROWS, tq), F32)
        for c in range(n_chunks):
            if h + 1 < MLA_HEADS:
                m_next = score_chunk(h + 1, c, 1 - slot, m_next)
            cs = slice(c * MLA_KC, (c + 1) * MLA_KC)
            p = jnp.exp2((s_ref[slot, cs, :] - m_cur).astype(F32)).astype(BF16)
            acc = acc + _dot(vT_ref[h * V_EXT:(h + 1) * V_EXT, cs], p)
        oT_ref[hv, :] = (acc[:MLA_V] * (1.0 / acc[MLA_V:MLA_V + 1])).astype(BF16)


def _mla_call(qT, k, vT, batch, seq):
    nq = seq // MLA_TQ
    n_tok = batch * seq
    return pl.pallas_call(
        functools.partial(_mla_kernel, seq=seq),
        grid=(batch, nq),
        in_specs=[
            pl.BlockSpec((MLA_HEADS * MLA_HEAD_PAD, MLA_TQ), lambda b, i: (0, b * nq + i)),
            pl.BlockSpec((MLA_HEADS, seq, MLA_HEAD_PAD), lambda b, i: (0, b, 0)),
            pl.BlockSpec((MLA_HEADS * V_EXT, seq), lambda b, i: (0, b)),
        ],
        out_specs=pl.BlockSpec((MLA_WIDTH, MLA_TQ), lambda b, i: (0, b * nq + i)),
        out_shape=jax.ShapeDtypeStruct((MLA_WIDTH, n_tok), BF16),
        scratch_shapes=[pltpu.VMEM((2, seq, MLA_TQ), BF16)],
        compiler_params=pltpu.CompilerParams(
            dimension_semantics=("arbitrary", "arbitrary"), vmem_limit_bytes=VMEM_LIMIT),
        name="mla",
    )(qT, k, vT)


def _layer_norm(v, g, b):
    mu = jnp.mean(v, axis=-1, keepdims=True)
    c = v - mu
    var = jnp.mean(c * c, axis=-1, keepdims=True)
    return c * lax.rsqrt(var + LN_EPS) * g + b


def _merge_kernel(x_ref, oa_ref, ob_ref, g_ref, wa_ref, wb_ref, wo_ref, bg_ref, lg_ref, lb_ref,
                  o_ref, *, alpha):
    y_a = _dot_tn(oa_ref[...], wa_ref[...])
    y_b = _dot_tn(ob_ref[...], wb_ref[...])
    g = g_ref[...].astype(F32) + bg_ref[...]
    merged = jax.nn.sigmoid(g[:, :D_MODEL]) * y_a + jax.nn.sigmoid(g[:, D_MODEL:]) * y_b
    mix = _dot(merged.astype(BF16), wo_ref[...])
    o_ref[...] = _layer_norm(alpha * x_ref[...] + mix, lg_ref[...], lb_ref[...])


def _merge_call(x2, oTa, oTb, gates, w, alpha):
    n_tok = x2.shape[0]
    tm = MERGE_TM
    tok = lambda i: (i, 0)
    tokT = lambda i: (0, i)
    return pl.pallas_call(
        functools.partial(_merge_kernel, alpha=alpha),
        grid=(n_tok // tm,),
        in_specs=[
            pl.BlockSpec((tm, D_MODEL), tok),
            pl.BlockSpec((NA_WIDTH, tm), tokT),
            pl.BlockSpec((MLA_WIDTH, tm), tokT),
            pl.BlockSpec((tm, 2 * D_MODEL), tok),
            _resident(w["wa"].shape), _resident(w["wb"].shape), _resident(w["wo"].shape),
            _resident(w["bg"].shape), _resident(w["ln1g"].shape), _resident(w["ln1b"].shape),
        ],
        out_specs=pl.BlockSpec((tm, D_MODEL), tok),
        out_shape=jax.ShapeDtypeStruct((n_tok, D_MODEL), F32),
        compiler_params=pltpu.CompilerParams(
            dimension_semantics=("arbitrary",), vmem_limit_bytes=VMEM_LIMIT),
        name="merge",
    )(x2, oTa, oTb, gates, w["wa"], w["wb"], w["wo"], w["bg"], w["ln1g"], w["ln1b"])


def _ffn_kernel(x_ref, wgate_ref, wup_ref, wdown_ref, lg_ref, lb_ref, o_ref, *, alpha):
    x = x_ref[...]
    xb = x.astype(BF16)
    acc = alpha * x
    for c in range(FFN_HIDDEN // FFN_TF):
        cs = slice(c * FFN_TF, (c + 1) * FFN_TF)
        gate = _dot(xb, wgate_ref[:, cs])
        up = _dot(xb, wup_ref[:, cs])
        act = (gate * jax.nn.sigmoid(gate) * up).astype(BF16)
        acc = acc + _dot(act, wdown_ref[cs, :])
    o_ref[...] = _layer_norm(acc, lg_ref[...], lb_ref[...])


def _ffn_call(x1, w, alpha):
    n_tok = x1.shape[0]
    tm = FFN_TM
    tok = lambda i: (i, 0)
    return pl.pallas_call(
        functools.partial(_ffn_kernel, alpha=alpha),
        grid=(n_tok // tm,),
        in_specs=[
            pl.BlockSpec((tm, D_MODEL), tok),
            _resident(w["wgate"].shape), _resident(w["wup"].shape), _resident(w["wdown"].shape),
            _resident(w["ln2g"].shape), _resident(w["ln2b"].shape),
        ],
        out_specs=pl.BlockSpec((tm, D_MODEL), tok),
        out_shape=jax.ShapeDtypeStruct((n_tok, D_MODEL), F32),
        compiler_params=pltpu.CompilerParams(
            dimension_semantics=("arbitrary",), vmem_limit_bytes=VMEM_LIMIT),
        name="ffn",
    )(x1, w["wgate"], w["wup"], w["wdown"], w["ln2g"], w["ln2b"])


def _rope_swap(cols):
    q = MLA_ROPE // 4
    r1, r2, c1, c2 = cols[..., :q], cols[..., q:2 * q], cols[..., 2 * q:3 * q], cols[..., 3 * q:]
    return jnp.concatenate([-r2, r1, -c2, c1], axis=-1)


def _rope_tables(seq):
    t = jnp.arange(seq)
    rows = (t // GRID_W).astype(F32)
    cols = (t % GRID_W).astype(F32)
    half = MLA_ROPE // 2
    inv_freq = ROPE_THETA ** (-jnp.arange(0, half, 2, dtype=F32) / half)
    ang_r = rows[:, None] * inv_freq
    ang_c = cols[:, None] * inv_freq
    cos = jnp.concatenate([jnp.cos(ang_r)] * 2 + [jnp.cos(ang_c)] * 2, axis=-1)
    sin = jnp.concatenate([jnp.sin(ang_r)] * 2 + [jnp.sin(ang_c)] * 2, axis=-1)
    return cos, sin


def _prep_layer(seq, w_in, b_gate, na_rpb, q_norm, w_uq, kv_norm, w_ukv, w_bna, w_bmla, w_out,
                ln1_g, ln1_b, w_ffn_in, w_ffn_out, ln2_g, ln2_b):
    o_q, o_k, o_v = 0, NA_WIDTH, 2 * NA_WIDTH
    o_ql = 3 * NA_WIDTH
    o_kvl = o_ql + MLA_Q_LORA
    o_kr = o_kvl + MLA_KV_LORA
    o_g = o_kr + MLA_ROPE
    w = {}
    na_scale = NA_HEAD_DIM ** -0.5 * LOG2_E
    w["wqT"] = (w_in[:, o_q:o_k] * na_scale).T.astype(BF16)
    w["wk"] = w_in[:, o_k:o_v].astype(BF16)
    w["wvT"] = w_in[:, o_v:o_ql].T.astype(BF16)
    k_rope_w = w_in[:, o_kr:o_g]
    pad = jnp.zeros((D_MODEL, LANES - MLA_ROPE), F32)
    w["wlat"] = jnp.concatenate(
        [w_in[:, o_ql:o_kr], k_rope_w, pad, _rope_swap(k_rope_w), pad], axis=1).astype(BF16)
    w["wg"] = w_in[:, o_g:].astype(BF16)
    w["qg"] = q_norm.reshape(1, MLA_Q_LORA)
    w["kvg"] = kv_norm.reshape(1, MLA_KV_LORA)

    uq = w_uq.reshape(MLA_Q_LORA, MLA_HEADS, MLA_QK)
    uq_pe = uq[:, :, MLA_NOPE:]
    uq_arr = jnp.concatenate([uq[:, :, :MLA_NOPE], uq_pe, _rope_swap(uq_pe)], axis=-1)
    w["wuqT"] = uq_arr.reshape(MLA_Q_LORA, MLA_HEADS * MLA_HEAD_PAD).T.astype(BF16)
    ukv = w_ukv.reshape(MLA_KV_LORA, MLA_HEADS, MLA_NOPE + MLA_V)
    uk_arr = jnp.concatenate(
        [ukv[:, :, :MLA_NOPE], jnp.zeros((MLA_KV_LORA, MLA_HEADS, MLA_HEAD_PAD - MLA_NOPE), F32)], axis=-1)
    w["wuk"] = uk_arr.reshape(MLA_KV_LORA, MLA_HEADS * MLA_HEAD_PAD).astype(BF16)
    w["wuvT"] = ukv[:, :, MLA_NOPE:].reshape(MLA_KV_LORA, MLA_WIDTH).T.astype(BF16)
    e = np.zeros((LANES, MLA_HEADS * MLA_HEAD_PAD), np.float32)
    for h in range(MLA_HEADS):
        e[np.arange(MLA_ROPE), h * MLA_HEAD_PAD + MLA_NOPE + np.arange(MLA_ROPE)] = 1.0
    w["e"] = jnp.asarray(e, BF16)

    cos, sin = _rope_tables(seq)
    zpad = jnp.zeros((seq, LANES - MLA_ROPE), F32)
    w["cn"] = jnp.concatenate([cos, zpad], axis=1)
    w["sn"] = jnp.concatenate([sin, zpad], axis=1)
    q_scale = MLA_QK ** -0.5 * LOG2_E
    ones = jnp.ones((MLA_NOPE, seq), F32)
    zer = jnp.zeros((MLA_NOPE, seq), F32)
    zer_r = jnp.zeros((MLA_ROPE, seq), F32)
    w["ctf"] = jnp.concatenate([ones, cos.T, zer_r], axis=0) * q_scale
    w["stf"] = jnp.concatenate([zer, sin.T, zer_r], axis=0) * q_scale

    w["na_bias"] = _na_bias_table(na_rpb, seq // GRID_W)

    w["wa"] = w_bna.astype(BF16)
    w["wb"] = w_bmla.astype(BF16)
    w["wo"] = w_out.astype(BF16)
    w["bg"] = b_gate.reshape(1, 2 * D_MODEL)
    w["ln1g"] = ln1_g.reshape(1, D_MODEL)
    w["ln1b"] = ln1_b.reshape(1, D_MODEL)
    w["wgate"] = w_ffn_in[:, :FFN_HIDDEN].astype(BF16)
    w["wup"] = w_ffn_in[:, FFN_HIDDEN:].astype(BF16)
    w["wdown"] = w_ffn_out.astype(BF16)
    w["ln2g"] = ln2_g.reshape(1, D_MODEL)
    w["ln2b"] = ln2_b.reshape(1, D_MODEL)
    return w


def kernel(x, w_in, b_gate, na_rpb, mla_q_norm, mla_w_uq, mla_kv_norm, mla_w_ukv, w_branch_na, w_branch_mla, w_out, ln1_g, ln1_b, w_ffn_in, w_ffn_out, ln2_g, ln2_b):
    batch, seq, d = x.shape
    depth = w_in.shape[0]
    alpha = (2.0 * depth) ** 0.25
    n_tok = batch * seq
    x2 = x.reshape(n_tok, d)
    for l in range(depth):
        w = _prep_layer(seq, w_in[l], b_gate[l], na_rpb[l], mla_q_norm[l], mla_w_uq[l], mla_kv_norm[l],
                        mla_w_ukv[l], w_branch_na[l], w_branch_mla[l], w_out[l], ln1_g[l], ln1_b[l],
                        w_ffn_in[l], w_ffn_out[l], ln2_g[l], ln2_b[l])
        qT_na, k_na, vT_na, qT_mla, k_mla, vT_mla, gates = _proj_call(x2, w, n_tok, seq)
        oT_na = _na_call(qT_na, k_na, vT_na, w["na_bias"], batch, seq)
        oT_mla = _mla_call(qT_mla, k_mla, vT_mla, batch, seq)
        x1 = _merge_call(x2, oT_na, oT_mla, gates, w, alpha)
        x2 = _ffn_call(x1, w, alpha)
    return x2.reshape(batch, seq, d)
```

```python
import functools

import numpy as np
import jax
import jax.numpy as jnp
from jax import lax
from jax.experimental import pallas as pl
from jax.experimental.pallas import tpu as pltpu

F32 = jnp.float32
BF16 = jnp.bfloat16

D_MODEL = 1024
GRID_W = 64
NA_HEADS = 8
NA_HEAD_DIM = 64
NA_WIDTH = NA_HEADS * NA_HEAD_DIM
NA_WIN_ROWS = 8
NA_WIN_COLS = 16
MLA_HEADS = 8
MLA_Q_LORA = 256
MLA_KV_LORA = 128
MLA_NOPE = 64
MLA_ROPE = 32
MLA_QK = MLA_NOPE + MLA_ROPE
MLA_V = 64
MLA_WIDTH = MLA_HEADS * MLA_V
ROPE_THETA = 10000.0
FFN_HIDDEN = 2816
LN_EPS = 1e-5
RMS_EPS = 1e-6

LANES = 128
MLA_HEAD_PAD = LANES
VMEM_LIMIT = 56 * 1024 * 1024

PROJ_TM = 512
NA_Q_ROWS = 4
NA_KEY_ROWS = 12
NA_BLOCK_ROWS = 8
MLA_TQ = 256
MLA_KC = 512
HEAD_V = 64
SUM_ROWS = 16
V_EXT = HEAD_V + SUM_ROWS
LOG2_E = 1.4426950408889634
MERGE_TM = 512
FFN_TM = 512
FFN_TF = 256

NEG_BIG = -1e30

NT = (((1,), (1,)), ((), ()))
TN = (((0,), (0,)), ((), ()))


def _dot(a, b):
    return jnp.dot(a, b, preferred_element_type=F32)


def _dot_nt(a, b):
    return lax.dot_general(a, b, NT, preferred_element_type=F32)


def _dot_tn(a, b):
    return lax.dot_general(a, b, TN, preferred_element_type=F32)


def _resident(shape):
    nd = len(shape)
    return pl.BlockSpec(shape, lambda *_: (0,) * nd, pipeline_mode=pl.Buffered(1))


def _rms(v, g):
    return v * lax.rsqrt(jnp.mean(v * v, axis=-1, keepdims=True) + RMS_EPS) * g


def _store_v_ext(ref, vT):
    ones = jnp.ones((SUM_ROWS, vT.shape[1]), BF16)
    for h in range(vT.shape[0] // HEAD_V):
        ref[h * V_EXT:h * V_EXT + HEAD_V, :] = vT[h * HEAD_V:(h + 1) * HEAD_V].astype(BF16)
        ref[h * V_EXT + HEAD_V:(h + 1) * V_EXT, :] = ones


def _proj_kernel(x_ref, wqT_ref, wk_ref, wvT_ref, wlat_ref, wg_ref, qg_ref, kvg_ref,
                 wuqT_ref, wuk_ref, e_ref, wuvT_ref, cn_ref, sn_ref, ctf_ref, stf_ref,
                 qT_na_ref, k_na_ref, vT_na_ref, qT_mla_ref, k_mla_ref, vT_mla_ref, g_ref):
    xb = x_ref[...].astype(BF16)
    qT_na_ref[...] = _dot_nt(wqT_ref[...], xb).astype(BF16)
    k_na_ref[...] = _dot(xb, wk_ref[...]).astype(BF16)
    _store_v_ext(vT_na_ref, _dot_nt(wvT_ref[...], xb))
    g_ref[...] = _dot(xb, wg_ref[...]).astype(BF16)

    lat = _dot(xb, wlat_ref[...])
    qn = _rms(lat[:, :MLA_Q_LORA], qg_ref[...]).astype(BF16)
    kvn = _rms(lat[:, MLA_Q_LORA:MLA_Q_LORA + MLA_KV_LORA], kvg_ref[...]).astype(BF16)
    kpe = lat[:, 384:512] * cn_ref[...] + lat[:, 512:640] * sn_ref[...]
    k_mla = _dot(kvn, wuk_ref[...]) + _dot(kpe.astype(BF16), e_ref[...])
    for h in range(MLA_HEADS):
        k_mla_ref[h] = k_mla[:, h * MLA_HEAD_PAD:(h + 1) * MLA_HEAD_PAD].astype(BF16)
    _store_v_ext(vT_mla_ref, _dot_nt(wuvT_ref[...], kvn))

    qT = _dot_nt(wuqT_ref[...], qn)
    ctf = ctf_ref[...]
    stf = stf_ref[...]
    for h in range(MLA_HEADS):
        qh = qT[h * MLA_HEAD_PAD:(h + 1) * MLA_HEAD_PAD]
        sw = jnp.concatenate([qh[MLA_ROPE:], qh[:MLA_ROPE]], axis=0)
        qT_mla_ref[h * MLA_HEAD_PAD:(h + 1) * MLA_HEAD_PAD, :] = (qh * ctf + sw * stf).astype(BF16)


def _proj_call(x2, w, n_tok, seq):
    tm = PROJ_TM
    steps_per_seq = seq // tm
    tok = lambda i: (i, 0)
    tokT = lambda i: (0, i)
    tab = lambda i: (i % steps_per_seq, 0)
    tabT = lambda i: (0, i % steps_per_seq)
    in_specs = [
        pl.BlockSpec((tm, D_MODEL), tok),
        _resident(w["wqT"].shape), _resident(w["wk"].shape), _resident(w["wvT"].shape),
        _resident(w["wlat"].shape), _resident(w["wg"].shape),
        _resident(w["qg"].shape), _resident(w["kvg"].shape),
        _resident(w["wuqT"].shape), _resident(w["wuk"].shape), _resident(w["e"].shape),
        _resident(w["wuvT"].shape),
        pl.BlockSpec((tm, LANES), tab), pl.BlockSpec((tm, LANES), tab),
        pl.BlockSpec((MLA_HEAD_PAD, tm), tabT), pl.BlockSpec((MLA_HEAD_PAD, tm), tabT),
    ]
    out_shape = [
        jax.ShapeDtypeStruct((NA_WIDTH, n_tok), BF16),
        jax.ShapeDtypeStruct((n_tok, NA_WIDTH), BF16),
        jax.ShapeDtypeStruct((NA_HEADS * V_EXT, n_tok), BF16),
        jax.ShapeDtypeStruct((MLA_HEADS * MLA_HEAD_PAD, n_tok), BF16),
        jax.ShapeDtypeStruct((MLA_HEADS, n_tok, MLA_HEAD_PAD), BF16),
        jax.ShapeDtypeStruct((MLA_HEADS * V_EXT, n_tok), BF16),
        jax.ShapeDtypeStruct((n_tok, 2 * D_MODEL), BF16),
    ]
    out_specs = [
        pl.BlockSpec((NA_WIDTH, tm), tokT),
        pl.BlockSpec((tm, NA_WIDTH), tok),
        pl.BlockSpec((NA_HEADS * V_EXT, tm), tokT),
        pl.BlockSpec((MLA_HEADS * MLA_HEAD_PAD, tm), tokT),
        pl.BlockSpec((MLA_HEADS, tm, MLA_HEAD_PAD), lambda i: (0, i, 0)),
        pl.BlockSpec((MLA_HEADS * V_EXT, tm), tokT),
        pl.BlockSpec((tm, 2 * D_MODEL), tok),
    ]
    return pl.pallas_call(
        _proj_kernel,
        grid=(n_tok // tm,),
        in_specs=in_specs,
        out_specs=out_specs,
        out_shape=out_shape,
        compiler_params=pltpu.CompilerParams(
            dimension_semantics=("arbitrary",), vmem_limit_bytes=VMEM_LIMIT),
        name="proj",
    )(x2, w["wqT"], w["wk"], w["wvT"], w["wlat"], w["wg"], w["qg"], w["kvg"],
      w["wuqT"], w["wuk"], w["e"], w["wuvT"], w["cn"], w["sn"], w["ctf"], w["stf"])


NA_GROUP_TYPES = 3


def _na_group_geometry(rows):
    return [(0, 0), (NA_Q_ROWS, 0), (rows - NA_Q_ROWS, rows - NA_KEY_ROWS)]


NA_MASKED_BLOCK = 2 * NA_WIN_ROWS - 1


def _na_block_index(rows, kr, qi):
    out = []
    for r0, ws in _na_group_geometry(rows):
        q_row, k_row = r0 + qi, ws + kr
        r_start = min(max(q_row - NA_WIN_ROWS // 2, 0), rows - NA_WIN_ROWS)
        valid = r_start <= k_row < r_start + NA_WIN_ROWS
        out.append(k_row - q_row + NA_WIN_ROWS - 1 if valid else NA_MASKED_BLOCK)
    return out


def _na_bias_blocks(rpb):
    n_dr = 2 * NA_WIN_ROWS - 1
    n_dc = 2 * NA_WIN_COLS - 1
    w = GRID_W
    lead = w - NA_WIN_COLS
    v = jnp.pad(rpb, ((0, 0), (0, 0), (lead, 2 * w - lead - n_dc)))
    a = jnp.tile(v, (1, 1, w))[:, :, :w * (2 * w - 1)].reshape(NA_HEADS, n_dr, w, 2 * w - 1)
    toe = jnp.swapaxes(a[:, :, :, w - 1:], 2, 3)
    col = np.arange(w)
    col_start = np.clip(col - NA_WIN_COLS // 2, 0, w - NA_WIN_COLS)
    v_col = (col[:, None] >= col_start[None, :]) & (col[:, None] < col_start[None, :] + NA_WIN_COLS)
    toe = jnp.where(v_col[None, None], toe * LOG2_E, NEG_BIG)
    toe = jnp.concatenate([toe, jnp.full((NA_HEADS, 1, w, w), NEG_BIG, F32)], axis=1)
    return jnp.concatenate([toe, toe], axis=-1)


def _na_kernel(qT_ref, k_ref, vT_ref, bias_ref, oT_ref, *, rows):
    blk = pl.program_id(1)
    nk = NA_KEY_ROWS * GRID_W
    nq = NA_Q_ROWS * GRID_W
    row_id = lax.broadcasted_iota(jnp.int32, (2 * NA_HEAD_DIM, nq), 0)
    left_half = lax.broadcasted_iota(jnp.int32, (GRID_W, LANES), 1) < GRID_W
    items = [(g, h) for g in range(NA_BLOCK_ROWS // NA_Q_ROWS) for h in range(NA_HEADS)]

    def window(g):
        r0 = blk * NA_BLOCK_ROWS + g * NA_Q_ROWS
        ws = jnp.clip(r0 - NA_WIN_ROWS // 2, 0, rows - NA_KEY_ROWS)
        typ = jnp.where(r0 == 0, 0, jnp.where(r0 == rows - NA_Q_ROWS, 2, 1))
        return pl.multiple_of(ws * GRID_W, 2 * LANES), typ

    def bias_tile(h, typ):
        def block(kr, qi):
            first, interior, last = _na_block_index(rows, kr, qi)
            idx = jnp.where(typ == 0, first, jnp.where(typ == 1, interior, last))
            return bias_ref[h, idx]
        key_rows = []
        for kr in range(NA_KEY_ROWS):
            pairs = [jnp.where(left_half, block(kr, qi), block(kr, qi + 1)) for qi in range(0, NA_Q_ROWS, 2)]
            key_rows.append(jnp.concatenate(pairs, axis=1))
        return jnp.concatenate(key_rows, axis=0)

    def scores(g, h):
        tok0, typ = window(g)
        hp, sub = divmod(h, 2)
        lanes = slice(hp * LANES, (hp + 1) * LANES)
        k_pair = k_ref[pl.ds(tok0, nk), lanes]
        q_pair = qT_ref[lanes, g * nq:(g + 1) * nq]
        keep = (row_id >= sub * NA_HEAD_DIM) & (row_id < (sub + 1) * NA_HEAD_DIM)
        q_h = jnp.where(keep, q_pair, jnp.zeros_like(q_pair))
        s = (_dot(k_pair, q_h) + bias_tile(h, typ)).astype(BF16)
        return s, jnp.max(s, axis=0, keepdims=True)

    def finish(g, h, s, m):
        tok0, _ = window(g)
        hv = slice(h * NA_HEAD_DIM, (h + 1) * NA_HEAD_DIM)
        p = jnp.exp2(s - m)
        acc = _dot(vT_ref[h * V_EXT:(h + 1) * V_EXT, pl.ds(tok0, nk)], p)
        o = acc[:NA_HEAD_DIM] * (1.0 / acc[NA_HEAD_DIM:NA_HEAD_DIM + 1])
        oT_ref[hv, g * nq:(g + 1) * nq] = o.astype(BF16)

    cur = scores(*items[0])
    for i, item in enumerate(items):
        nxt = scores(*items[i + 1]) if i + 1 < len(items) else None
        finish(*item, *cur)
        cur = nxt


def _na_call(qT, k, vT, bias, batch, seq):
    rows = seq // GRID_W
    blocks = rows // NA_BLOCK_ROWS
    tq = NA_BLOCK_ROWS * GRID_W
    n_tok = batch * seq
    return pl.pallas_call(
        functools.partial(_na_kernel, rows=rows),
        grid=(batch, blocks),
        in_specs=[
            pl.BlockSpec((NA_WIDTH, tq), lambda b, i: (0, b * blocks + i)),
            pl.BlockSpec((seq, NA_WIDTH), lambda b, i: (b, 0)),
            pl.BlockSpec((NA_HEADS * V_EXT, seq), lambda b, i: (0, b)),
            _resident(bias.shape),
        ],
        out_specs=pl.BlockSpec((NA_WIDTH, tq), lambda b, i: (0, b * blocks + i)),
        out_shape=jax.ShapeDtypeStruct((NA_WIDTH, n_tok), BF16),
        compiler_params=pltpu.CompilerParams(
            dimension_semantics=("arbitrary", "arbitrary"), vmem_limit_bytes=VMEM_LIMIT),
        name="na",
    )(qT, k, vT, bias)


def _mla_kernel(qT_ref, k_ref, vT_ref, oT_ref, s_ref, *, seq):
    n_chunks = seq // MLA_KC
    tq = oT_ref.shape[1]

    def score_chunk(h, c, slot, m):
        hk = slice(h * MLA_HEAD_PAD, (h + 1) * MLA_HEAD_PAD)
        s = _dot(k_ref[h, c * MLA_KC:(c + 1) * MLA_KC, :], qT_ref[hk, :]).astype(BF16)
        s_ref[slot, c * MLA_KC:(c + 1) * MLA_KC, :] = s
        return jnp.maximum(m, jnp.max(s, axis=0, keepdims=True))

    neg = jnp.full((1, tq), -jnp.inf, BF16)
    m_next = neg
    for c in range(n_chunks):
        m_next = score_chunk(0, c, 0, m_next)
    for h in range(MLA_HEADS):
        slot = h % 2
        m_cur, m_next = m_next, neg
        hv = slice(h * MLA_V, (h + 1) * MLA_V)
        acc = jnp.zeros((MLA_V + SUM_ROWS, tq), F32)
        for c in range(n_chunks):
            if h + 1 < MLA_HEADS:
                m_next = score_chunk(h + 1, c, 1 - slot, m_next)
            cs = slice(c * MLA_KC, (c + 1) * MLA_KC)
            p = jnp.exp2(s_ref[slot, cs, :] - m_cur)
            acc = acc + _dot(vT_ref[h * V_EXT:(h + 1) * V_EXT, cs], p)
        oT_ref[hv, :] = (acc[:MLA_V] * (1.0 / acc[MLA_V:MLA_V + 1])).astype(BF16)


def _mla_call(qT, k, vT, batch, seq):
    nq = seq // MLA_TQ
    n_tok = batch * seq
    return pl.pallas_call(
        functools.partial(_mla_kernel, seq=seq),
        grid=(batch, nq),
        in_specs=[
            pl.BlockSpec((MLA_HEADS * MLA_HEAD_PAD, MLA_TQ), lambda b, i: (0, b * nq + i)),
            pl.BlockSpec((MLA_HEADS, seq, MLA_HEAD_PAD), lambda b, i: (0, b, 0)),
            pl.BlockSpec((MLA_HEADS * V_EXT, seq), lambda b, i: (0, b)),
        ],
        out_specs=pl.BlockSpec((MLA_WIDTH, MLA_TQ), lambda b, i: (0, b * nq + i)),
        out_shape=jax.ShapeDtypeStruct((MLA_WIDTH, n_tok), BF16),
        scratch_shapes=[pltpu.VMEM((2, seq, MLA_TQ), BF16)],
        compiler_params=pltpu.CompilerParams(
            dimension_semantics=("arbitrary", "arbitrary"), vmem_limit_bytes=VMEM_LIMIT),
        name="mla",
    )(qT, k, vT)


def _layer_norm(v, g, b):
    mu = jnp.mean(v, axis=-1, keepdims=True)
    c = v - mu
    var = jnp.mean(c * c, axis=-1, keepdims=True)
    return c * lax.rsqrt(var + LN_EPS) * g + b


def _merge_kernel(x_ref, oa_ref, ob_ref, g_ref, wa_ref, wb_ref, wo_ref, bg_ref, lg_ref, lb_ref,
                  o_ref, *, alpha):
    y_a = _dot_tn(oa_ref[...], wa_ref[...])
    y_b = _dot_tn(ob_ref[...], wb_ref[...])
    g = g_ref[...].astype(F32) + bg_ref[...]
    merged = jax.nn.sigmoid(g[:, :D_MODEL]) * y_a + jax.nn.sigmoid(g[:, D_MODEL:]) * y_b
    mix = _dot(merged.astype(BF16), wo_ref[...])
    o_ref[...] = _layer_norm(alpha * x_ref[...] + mix, lg_ref[...], lb_ref[...])


def _merge_call(x2, oTa, oTb, gates, w, alpha):
    n_tok = x2.shape[0]
    tm = MERGE_TM
    tok = lambda i: (i, 0)
    tokT = lambda i: (0, i)
    return pl.pallas_call(
        functools.partial(_merge_kernel, alpha=alpha),
        grid=(n_tok // tm,),
        in_specs=[
            pl.BlockSpec((tm, D_MODEL), tok),
            pl.BlockSpec((NA_WIDTH, tm), tokT),
            pl.BlockSpec((MLA_WIDTH, tm), tokT),
            pl.BlockSpec((tm, 2 * D_MODEL), tok),
            _resident(w["wa"].shape), _resident(w["wb"].shape), _resident(w["wo"].shape),
            _resident(w["bg"].shape), _resident(w["ln1g"].shape), _resident(w["ln1b"].shape),
        ],
        out_specs=pl.BlockSpec((tm, D_MODEL), tok),
        out_shape=jax.ShapeDtypeStruct((n_tok, D_MODEL), F32),
        compiler_params=pltpu.CompilerParams(
            dimension_semantics=("arbitrary",), vmem_limit_bytes=VMEM_LIMIT),
        name="merge",
    )(x2, oTa, oTb, gates, w["wa"], w["wb"], w["wo"], w["bg"], w["ln1g"], w["ln1b"])


def _ffn_kernel(x_ref, win_ref, wdown_ref, lg_ref, lb_ref, o_ref, *, alpha):
    x = x_ref[...]
    xb = x.astype(BF16)
    acc = alpha * x
    for c in range(FFN_HIDDEN // FFN_TF):
        cs = slice(c * FFN_TF, (c + 1) * FFN_TF)
        gate = _dot(xb, win_ref[:, cs])
        up = _dot(xb, win_ref[:, FFN_HIDDEN + c * FFN_TF:FFN_HIDDEN + (c + 1) * FFN_TF])
        act = (gate * jax.nn.sigmoid(gate) * up).astype(BF16)
        acc = acc + _dot(act, wdown_ref[cs, :])
    o_ref[...] = _layer_norm(acc, lg_ref[...], lb_ref[...])


def _ffn_call(x1, w, alpha):
    n_tok = x1.shape[0]
    tm = FFN_TM
    tok = lambda i: (i, 0)
    return pl.pallas_call(
        functools.partial(_ffn_kernel, alpha=alpha),
        grid=(n_tok // tm,),
        in_specs=[
            pl.BlockSpec((tm, D_MODEL), tok),
            _resident(w["wffn_in"].shape), _resident(w["wdown"].shape),
            _resident(w["ln2g"].shape), _resident(w["ln2b"].shape),
        ],
        out_specs=pl.BlockSpec((tm, D_MODEL), tok),
        out_shape=jax.ShapeDtypeStruct((n_tok, D_MODEL), F32),
        compiler_params=pltpu.CompilerParams(
            dimension_semantics=("arbitrary",), vmem_limit_bytes=VMEM_LIMIT),
        name="ffn",
    )(x1, w["wffn_in"], w["wdown"], w["ln2g"], w["ln2b"])


def _rope_swap(cols):
    q = MLA_ROPE // 4
    r1, r2, c1, c2 = cols[..., :q], cols[..., q:2 * q], cols[..., 2 * q:3 * q], cols[..., 3 * q:]
    return jnp.concatenate([-r2, r1, -c2, c1], axis=-1)


def _rope_tables(seq, q_scale):
    half = MLA_ROPE // 2
    quarter = MLA_ROPE // 4
    inv_freq = ROPE_THETA ** (-jnp.arange(0, half, 2, dtype=F32) / half)
    freq = jnp.tile(inv_freq, LANES // quarter)
    t = jnp.arange(seq)
    rows = (t // GRID_W).astype(F32)
    cols = (t % GRID_W).astype(F32)
    d = np.arange(LANES)
    ang = jnp.where((d < half)[None, :], rows[:, None], cols[:, None]) * freq[None, :]
    live = (d < MLA_ROPE)[None, :]
    cn = jnp.where(live, jnp.cos(ang), 0.0)
    sn = jnp.where(live, jnp.sin(ang), 0.0)
    r = d - MLA_NOPE
    ang_t = jnp.where(((r >= 0) & (r < half))[:, None], rows[None, :], cols[None, :]) * freq[:, None]
    rot = ((r >= 0) & (r < MLA_ROPE))[:, None]
    ctf = jnp.where((d < MLA_NOPE)[:, None], q_scale, jnp.where(rot, jnp.cos(ang_t) * q_scale, 0.0))
    stf = jnp.where(rot, jnp.sin(ang_t) * q_scale, 0.0)
    return cn, sn, ctf.astype(F32), stf.astype(F32)


def _prep_layer(seq, w_in, b_gate, na_rpb, q_norm, w_uq, kv_norm, w_ukv, w_bna, w_bmla, w_out,
                ln1_g, ln1_b, w_ffn_in, w_ffn_out, ln2_g, ln2_b):
    o_q, o_k, o_v = 0, NA_WIDTH, 2 * NA_WIDTH
    o_ql = 3 * NA_WIDTH
    o_kvl = o_ql + MLA_Q_LORA
    o_kr = o_kvl + MLA_KV_LORA
    o_g = o_kr + MLA_ROPE
    w = {}
    na_scale = NA_HEAD_DIM ** -0.5 * LOG2_E
    w["wqT"] = (w_in[:, o_q:o_k] * na_scale).T.astype(BF16)
    w["wk"] = w_in[:, o_k:o_v].astype(BF16)
    w["wvT"] = w_in[:, o_v:o_ql].T.astype(BF16)
    k_rope_w = w_in[:, o_kr:o_g]
    pad = jnp.zeros((D_MODEL, LANES - MLA_ROPE), F32)
    w["wlat"] = jnp.concatenate(
        [w_in[:, o_ql:o_kr], k_rope_w, pad, _rope_swap(k_rope_w), pad], axis=1).astype(BF16)
    w["wg"] = w_in[:, o_g:].astype(BF16)
    w["qg"] = q_norm.reshape(1, MLA_Q_LORA)
    w["kvg"] = kv_norm.reshape(1, MLA_KV_LORA)

    uq = w_uq.reshape(MLA_Q_LORA, MLA_HEADS, MLA_QK)
    uq_pe = uq[:, :, MLA_NOPE:]
    uq_arr = jnp.concatenate([uq[:, :, :MLA_NOPE], uq_pe, _rope_swap(uq_pe)], axis=-1)
    w["wuqT"] = uq_arr.reshape(MLA_Q_LORA, MLA_HEADS * MLA_HEAD_PAD).T.astype(BF16)
    ukv = w_ukv.reshape(MLA_KV_LORA, MLA_HEADS, MLA_NOPE + MLA_V)
    uk_arr = jnp.concatenate(
        [ukv[:, :, :MLA_NOPE], jnp.zeros((MLA_KV_LORA, MLA_HEADS, MLA_HEAD_PAD - MLA_NOPE), F32)], axis=-1)
    w["wuk"] = uk_arr.reshape(MLA_KV_LORA, MLA_HEADS * MLA_HEAD_PAD).astype(BF16)
    w["wuvT"] = ukv[:, :, MLA_NOPE:].reshape(MLA_KV_LORA, MLA_WIDTH).T.astype(BF16)
    e = np.zeros((LANES, MLA_HEADS * MLA_HEAD_PAD), np.float32)
    for h in range(MLA_HEADS):
        e[np.arange(MLA_ROPE), h * MLA_HEAD_PAD + MLA_NOPE + np.arange(MLA_ROPE)] = 1.0
    w["e"] = jnp.asarray(e, BF16)

    q_scale = MLA_QK ** -0.5 * LOG2_E
    w["cn"], w["sn"], w["ctf"], w["stf"] = _rope_tables(seq, q_scale)

    w["na_bias"] = _na_bias_blocks(na_rpb)

    w["wa"] = w_bna.astype(BF16)
    w["wb"] = w_bmla.astype(BF16)
    w["wo"] = w_out.astype(BF16)
    w["bg"] = b_gate.reshape(1, 2 * D_MODEL)
    w["ln1g"] = ln1_g.reshape(1, D_MODEL)
    w["ln1b"] = ln1_b.reshape(1, D_MODEL)
    w["wffn_in"] = w_ffn_in.astype(BF16)
    w["wdown"] = w_ffn_out.astype(BF16)
    w["ln2g"] = ln2_g.reshape(1, D_MODEL)
    w["ln2b"] = ln2_b.reshape(1, D_MODEL)
    return w


def kernel(x, w_in, b_gate, na_rpb, mla_q_norm, mla_w_uq, mla_kv_norm, mla_w_ukv, w_branch_na, w_branch_mla, w_out, ln1_g, ln1_b, w_ffn_in, w_ffn_out, ln2_g, ln2_b):
    batch, seq, d = x.shape
    depth = w_in.shape[0]
    alpha = (2.0 * depth) ** 0.25
    n_tok = batch * seq
    x2 = x.reshape(n_tok, d)
    for l in range(depth):
        w = _prep_layer(seq, w_in[l], b_gate[l], na_rpb[l], mla_q_norm[l], mla_w_uq[l], mla_kv_norm[l],
                        mla_w_ukv[l], w_branch_na[l], w_branch_mla[l], w_out[l], ln1_g[l], ln1_b[l],
                        w_ffn_in[l], w_ffn_out[l], ln2_g[l], ln2_b[l])
        qT_na, k_na, vT_na, qT_mla, k_mla, vT_mla, gates = _proj_call(x2, w, n_tok, seq)
        oT_na = _na_call(qT_na, k_na, vT_na, w["na_bias"], batch, seq)
        oT_mla = _mla_call(qT_mla, k_mla, vT_mla, batch, seq)
        x1 = _merge_call(x2, oT_na, oT_mla, gates, w, alpha)
        x2 = _ffn_call(x1, w, alpha)
    return x2.reshape(batch, seq, d)
```

```python
import functools

import numpy as np
import jax
import jax.numpy as jnp
from jax import lax
from jax.experimental import pallas as pl
from jax.experimental.pallas import tpu as pltpu

F32 = jnp.float32
BF16 = jnp.bfloat16

D_MODEL = 1024
GRID_W = 64
NA_HEADS = 8
NA_HEAD_DIM = 64
NA_WIDTH = NA_HEADS * NA_HEAD_DIM
NA_WIN_ROWS = 8
NA_WIN_COLS = 16
MLA_HEADS = 8
MLA_Q_LORA = 256
MLA_KV_LORA = 128
MLA_NOPE = 64
MLA_ROPE = 32
MLA_QK = MLA_NOPE + MLA_ROPE
MLA_V = 64
MLA_WIDTH = MLA_HEADS * MLA_V
ROPE_THETA = 10000.0
FFN_HIDDEN = 2816
LN_EPS = 1e-5
RMS_EPS = 1e-6

LANES = 128
MLA_HEAD_PAD = LANES
VMEM_LIMIT = 56 * 1024 * 1024

PROJ_TM = 512
NA_Q_ROWS = 4
NA_KEY_ROWS = 12
NA_BLOCK_ROWS = 8
MLA_TQ = 512
MLA_KC = 512
HEAD_V = 64
SUM_ROWS = 16
V_EXT = HEAD_V + SUM_ROWS
LOG2_E = 1.4426950408889634
MERGE_TM = 512
FFN_TM = 512
FFN_TF = 256

NEG_BIG = -1e30

NT = (((1,), (1,)), ((), ()))
TN = (((0,), (0,)), ((), ()))


def _dot(a, b):
    return jnp.dot(a, b, preferred_element_type=F32)


def _dot_nt(a, b):
    return lax.dot_general(a, b, NT, preferred_element_type=F32)


def _dot_tn(a, b):
    return lax.dot_general(a, b, TN, preferred_element_type=F32)


def _resident(shape):
    nd = len(shape)
    return pl.BlockSpec(shape, lambda *_: (0,) * nd, pipeline_mode=pl.Buffered(1))


def _rms(v, g):
    return v * lax.rsqrt(jnp.mean(v * v, axis=-1, keepdims=True) + RMS_EPS) * g


def _store_v_ext(ref, vT):
    ones = jnp.ones((SUM_ROWS, vT.shape[1]), BF16)
    for h in range(vT.shape[0] // HEAD_V):
        ref[h * V_EXT:h * V_EXT + HEAD_V, :] = vT[h * HEAD_V:(h + 1) * HEAD_V].astype(BF16)
        ref[h * V_EXT + HEAD_V:(h + 1) * V_EXT, :] = ones


def _proj_kernel(x_ref, wqT_ref, wk_ref, wvT_ref, wlat_ref, wg_ref, qg_ref, kvg_ref,
                 wuqT_ref, wuk_ref, e_ref, wuvT_ref, cn_ref, sn_ref, ctf_ref, stf_ref,
                 qT_na_ref, k_na_ref, vT_na_ref, qT_mla_ref, k_mla_ref, vT_mla_ref, g_ref):
    xb = x_ref[...].astype(BF16)
    qT_na_ref[...] = _dot_nt(wqT_ref[...], xb).astype(BF16)
    k_na_ref[...] = _dot(xb, wk_ref[...]).astype(BF16)
    _store_v_ext(vT_na_ref, _dot_nt(wvT_ref[...], xb))
    g_ref[...] = _dot(xb, wg_ref[...]).astype(BF16)

    lat = _dot(xb, wlat_ref[...])
    qn = _rms(lat[:, :MLA_Q_LORA], qg_ref[...]).astype(BF16)
    kvn = _rms(lat[:, MLA_Q_LORA:MLA_Q_LORA + MLA_KV_LORA], kvg_ref[...]).astype(BF16)
    kpe = lat[:, 384:512] * cn_ref[...] + lat[:, 512:640] * sn_ref[...]
    k_mla = _dot(kvn, wuk_ref[...]) + _dot(kpe.astype(BF16), e_ref[...])
    for h in range(MLA_HEADS):
        k_mla_ref[h] = k_mla[:, h * MLA_HEAD_PAD:(h + 1) * MLA_HEAD_PAD].astype(BF16)
    _store_v_ext(vT_mla_ref, _dot_nt(wuvT_ref[...], kvn))

    qT = _dot_nt(wuqT_ref[...], qn)
    ctf = ctf_ref[...]
    stf = stf_ref[...]
    for h in range(MLA_HEADS):
        qh = qT[h * MLA_HEAD_PAD:(h + 1) * MLA_HEAD_PAD]
        sw = jnp.concatenate([qh[MLA_ROPE:], qh[:MLA_ROPE]], axis=0)
        qT_mla_ref[h * MLA_HEAD_PAD:(h + 1) * MLA_HEAD_PAD, :] = (qh * ctf + sw * stf).astype(BF16)


def _proj_call(x2, w, n_tok, seq):
    tm = PROJ_TM
    steps_per_seq = seq // tm
    tok = lambda i: (i, 0)
    tokT = lambda i: (0, i)
    tab = lambda i: (i % steps_per_seq, 0)
    tabT = lambda i: (0, i % steps_per_seq)
    in_specs = [
        pl.BlockSpec((tm, D_MODEL), tok),
        _resident(w["wqT"].shape), _resident(w["wk"].shape), _resident(w["wvT"].shape),
        _resident(w["wlat"].shape), _resident(w["wg"].shape),
        _resident(w["qg"].shape), _resident(w["kvg"].shape),
        _resident(w["wuqT"].shape), _resident(w["wuk"].shape), _resident(w["e"].shape),
        _resident(w["wuvT"].shape),
        pl.BlockSpec((tm, LANES), tab), pl.BlockSpec((tm, LANES), tab),
        pl.BlockSpec((MLA_HEAD_PAD, tm), tabT), pl.BlockSpec((MLA_HEAD_PAD, tm), tabT),
    ]
    out_shape = [
        jax.ShapeDtypeStruct((NA_WIDTH, n_tok), BF16),
        jax.ShapeDtypeStruct((n_tok, NA_WIDTH), BF16),
        jax.ShapeDtypeStruct((NA_HEADS * V_EXT, n_tok), BF16),
        jax.ShapeDtypeStruct((MLA_HEADS * MLA_HEAD_PAD, n_tok), BF16),
        jax.ShapeDtypeStruct((MLA_HEADS, n_tok, MLA_HEAD_PAD), BF16),
        jax.ShapeDtypeStruct((MLA_HEADS * V_EXT, n_tok), BF16),
        jax.ShapeDtypeStruct((n_tok, 2 * D_MODEL), BF16),
    ]
    out_specs = [
        pl.BlockSpec((NA_WIDTH, tm), tokT),
        pl.BlockSpec((tm, NA_WIDTH), tok),
        pl.BlockSpec((NA_HEADS * V_EXT, tm), tokT),
        pl.BlockSpec((MLA_HEADS * MLA_HEAD_PAD, tm), tokT),
        pl.BlockSpec((MLA_HEADS, tm, MLA_HEAD_PAD), lambda i: (0, i, 0)),
        pl.BlockSpec((MLA_HEADS * V_EXT, tm), tokT),
        pl.BlockSpec((tm, 2 * D_MODEL), tok),
    ]
    return pl.pallas_call(
        _proj_kernel,
        grid=(n_tok // tm,),
        in_specs=in_specs,
        out_specs=out_specs,
        out_shape=out_shape,
        compiler_params=pltpu.CompilerParams(
            dimension_semantics=("arbitrary",), vmem_limit_bytes=VMEM_LIMIT),
        name="proj",
    )(x2, w["wqT"], w["wk"], w["wvT"], w["wlat"], w["wg"], w["qg"], w["kvg"],
      w["wuqT"], w["wuk"], w["e"], w["wuvT"], w["cn"], w["sn"], w["ctf"], w["stf"])


NA_GROUP_TYPES = 3


def _na_group_geometry(rows):
    return [(0, 0), (NA_Q_ROWS, 0), (rows - NA_Q_ROWS, rows - NA_KEY_ROWS)]


NA_MASKED_BLOCK = 2 * NA_WIN_ROWS - 1


def _na_block_index(rows, kr, qi):
    out = []
    for r0, ws in _na_group_geometry(rows):
        q_row, k_row = r0 + qi, ws + kr
        r_start = min(max(q_row - NA_WIN_ROWS // 2, 0), rows - NA_WIN_ROWS)
        valid = r_start <= k_row < r_start + NA_WIN_ROWS
        out.append(k_row - q_row + NA_WIN_ROWS - 1 if valid else NA_MASKED_BLOCK)
    return out


def _na_bias_blocks(rpb):
    n_dr = 2 * NA_WIN_ROWS - 1
    n_dc = 2 * NA_WIN_COLS - 1
    w = GRID_W
    lead = w - NA_WIN_COLS
    v = jnp.pad(rpb, ((0, 0), (0, 0), (lead, 2 * w - lead - n_dc)))
    a = jnp.tile(v, (1, 1, w))[:, :, :w * (2 * w - 1)].reshape(NA_HEADS, n_dr, w, 2 * w - 1)
    toe = jnp.swapaxes(a[:, :, :, w - 1:], 2, 3)
    col = np.arange(w)
    col_start = np.clip(col - NA_WIN_COLS // 2, 0, w - NA_WIN_COLS)
    v_col = (col[:, None] >= col_start[None, :]) & (col[:, None] < col_start[None, :] + NA_WIN_COLS)
    toe = jnp.where(v_col[None, None], toe * LOG2_E, NEG_BIG)
    toe = jnp.concatenate([toe, jnp.full((NA_HEADS, 1, w, w), NEG_BIG, F32)], axis=1)
    return jnp.concatenate([toe, toe], axis=-1)


def _na_kernel(qT_ref, k_ref, vT_ref, bias_ref, oT_ref, *, rows):
    blk = pl.program_id(1)
    nk = NA_KEY_ROWS * GRID_W
    nq = NA_Q_ROWS * GRID_W
    row_id = lax.broadcasted_iota(jnp.int32, (2 * NA_HEAD_DIM, nq), 0)
    left_half = lax.broadcasted_iota(jnp.int32, (GRID_W, LANES), 1) < GRID_W
    items = [(g, h) for g in range(NA_BLOCK_ROWS // NA_Q_ROWS) for h in range(NA_HEADS)]

    def window(g):
        r0 = blk * NA_BLOCK_ROWS + g * NA_Q_ROWS
        ws = jnp.clip(r0 - NA_WIN_ROWS // 2, 0, rows - NA_KEY_ROWS)
        typ = jnp.where(r0 == 0, 0, jnp.where(r0 == rows - NA_Q_ROWS, 2, 1))
        return pl.multiple_of(ws * GRID_W, 2 * LANES), typ

    def bias_tile(h, typ):
        def block(kr, qi):
            first, interior, last = _na_block_index(rows, kr, qi)
            idx = jnp.where(typ == 0, first, jnp.where(typ == 1, interior, last))
            return bias_ref[h, idx]
        key_rows = []
        for kr in range(NA_KEY_ROWS):
            pairs = [jnp.where(left_half, block(kr, qi), block(kr, qi + 1)) for qi in range(0, NA_Q_ROWS, 2)]
            key_rows.append(jnp.concatenate(pairs, axis=1))
        return jnp.concatenate(key_rows, axis=0)

    def scores(g, h):
        tok0, typ = window(g)
        hp, sub = divmod(h, 2)
        lanes = slice(hp * LANES, (hp + 1) * LANES)
        k_pair = k_ref[pl.ds(tok0, nk), lanes]
        q_pair = qT_ref[lanes, g * nq:(g + 1) * nq]
        keep = (row_id >= sub * NA_HEAD_DIM) & (row_id < (sub + 1) * NA_HEAD_DIM)
        q_h = jnp.where(keep, q_pair, jnp.zeros_like(q_pair))
        s = (_dot(k_pair, q_h) + bias_tile(h, typ)).astype(BF16)
        return s, jnp.max(s, axis=0, keepdims=True)

    def finish(g, h, s, m):
        tok0, _ = window(g)
        hv = slice(h * NA_HEAD_DIM, (h + 1) * NA_HEAD_DIM)
        p = jnp.exp2(s - m)
        acc = _dot(vT_ref[h * V_EXT:(h + 1) * V_EXT, pl.ds(tok0, nk)], p)
        o = acc[:NA_HEAD_DIM] * (1.0 / acc[NA_HEAD_DIM:NA_HEAD_DIM + 1])
        oT_ref[hv, g * nq:(g + 1) * nq] = o.astype(BF16)

    cur = scores(*items[0])
    for i, item in enumerate(items):
        nxt = scores(*items[i + 1]) if i + 1 < len(items) else None
        finish(*item, *cur)
        cur = nxt


def _na_call(qT, k, vT, bias, batch, seq):
    rows = seq // GRID_W
    blocks = rows // NA_BLOCK_ROWS
    tq = NA_BLOCK_ROWS * GRID_W
    n_tok = batch * seq
    return pl.pallas_call(
        functools.partial(_na_kernel, rows=rows),
        grid=(batch, blocks),
        in_specs=[
            pl.BlockSpec((NA_WIDTH, tq), lambda b, i: (0, b * blocks + i)),
            pl.BlockSpec((seq, NA_WIDTH), lambda b, i: (b, 0)),
            pl.BlockSpec((NA_HEADS * V_EXT, seq), lambda b, i: (0, b)),
            _resident(bias.shape),
        ],
        out_specs=pl.BlockSpec((NA_WIDTH, tq), lambda b, i: (0, b * blocks + i)),
        out_shape=jax.ShapeDtypeStruct((NA_WIDTH, n_tok), BF16),
        compiler_params=pltpu.CompilerParams(
            dimension_semantics=("arbitrary", "arbitrary"), vmem_limit_bytes=VMEM_LIMIT),
        name="na",
    )(qT, k, vT, bias)


def _mla_kernel(qT_ref, k_ref, vT_ref, oT_ref, s_ref, *, seq):
    n_chunks = seq // MLA_KC
    tq = oT_ref.shape[1]

    def score_chunk(h, c, slot, m):
        hk = slice(h * MLA_HEAD_PAD, (h + 1) * MLA_HEAD_PAD)
        s = _dot(k_ref[h, c * MLA_KC:(c + 1) * MLA_KC, :], qT_ref[hk, :]).astype(BF16)
        s_ref[slot, c * MLA_KC:(c + 1) * MLA_KC, :] = s
        return jnp.maximum(m, jnp.max(s, axis=0, keepdims=True))

    neg = jnp.full((1, tq), -jnp.inf, BF16)
    m_next = neg
    for c in range(n_chunks):
        m_next = score_chunk(0, c, 0, m_next)
    for h in range(MLA_HEADS):
        slot = h % 2
        m_cur, m_next = m_next, neg
        hv = slice(h * MLA_V, (h + 1) * MLA_V)
        acc = jnp.zeros((MLA_V + SUM_ROWS, tq), F32)
        for c in range(n_chunks):
            if h + 1 < MLA_HEADS:
                m_next = score_chunk(h + 1, c, 1 - slot, m_next)
            cs = slice(c * MLA_KC, (c + 1) * MLA_KC)
            p = jnp.exp2(s_ref[slot, cs, :] - m_cur)
            acc = acc + _dot(vT_ref[h * V_EXT:(h + 1) * V_EXT, cs], p)
        oT_ref[hv, :] = (acc[:MLA_V] * (1.0 / acc[MLA_V:MLA_V + 1])).astype(BF16)


def _mla_call(qT, k, vT, batch, seq):
    nq = seq // MLA_TQ
    n_tok = batch * seq
    return pl.pallas_call(
        functools.partial(_mla_kernel, seq=seq),
        grid=(batch, nq),
        in_specs=[
            pl.BlockSpec((MLA_HEADS * MLA_HEAD_PAD, MLA_TQ), lambda b, i: (0, b * nq + i)),
            pl.BlockSpec((MLA_HEADS, seq, MLA_HEAD_PAD), lambda b, i: (0, b, 0)),
            pl.BlockSpec((MLA_HEADS * V_EXT, seq), lambda b, i: (0, b)),
        ],
        out_specs=pl.BlockSpec((MLA_WIDTH, MLA_TQ), lambda b, i: (0, b * nq + i)),
        out_shape=jax.ShapeDtypeStruct((MLA_WIDTH, n_tok), BF16),
        scratch_shapes=[pltpu.VMEM((2, seq, MLA_TQ), BF16)],
        compiler_params=pltpu.CompilerParams(
            dimension_semantics=("arbitrary", "arbitrary"), vmem_limit_bytes=VMEM_LIMIT),
        name="mla",
    )(qT, k, vT)


def _layer_norm(v, g, b):
    mu = jnp.mean(v, axis=-1, keepdims=True)
    c = v - mu
    var = jnp.mean(c * c, axis=-1, keepdims=True)
    return c * lax.rsqrt(var + LN_EPS) * g + b


def _merge_kernel(x_ref, oa_ref, ob_ref, g_ref, wa_ref, wb_ref, wo_ref, bg_ref, lg_ref, lb_ref,
                  o_ref, *, alpha):
    y_a = _dot_tn(oa_ref[...], wa_ref[...])
    y_b = _dot_tn(ob_ref[...], wb_ref[...])
    g = g_ref[...].astype(F32) + bg_ref[...]
    merged = jax.nn.sigmoid(g[:, :D_MODEL]) * y_a + jax.nn.sigmoid(g[:, D_MODEL:]) * y_b
    mix = _dot(merged.astype(BF16), wo_ref[...])
    o_ref[...] = _layer_norm(alpha * x_ref[...] + mix, lg_ref[...], lb_ref[...])


def _merge_call(x2, oTa, oTb, gates, w, alpha):
    n_tok = x2.shape[0]
    tm = MERGE_TM
    tok = lambda i: (i, 0)
    tokT = lambda i: (0, i)
    return pl.pallas_call(
        functools.partial(_merge_kernel, alpha=alpha),
        grid=(n_tok // tm,),
        in_specs=[
            pl.BlockSpec((tm, D_MODEL), tok),
            pl.BlockSpec((NA_WIDTH, tm), tokT),
            pl.BlockSpec((MLA_WIDTH, tm), tokT),
            pl.BlockSpec((tm, 2 * D_MODEL), tok),
            _resident(w["wa"].shape), _resident(w["wb"].shape), _resident(w["wo"].shape),
            _resident(w["bg"].shape), _resident(w["ln1g"].shape), _resident(w["ln1b"].shape),
        ],
        out_specs=pl.BlockSpec((tm, D_MODEL), tok),
        out_shape=jax.ShapeDtypeStruct((n_tok, D_MODEL), F32),
        compiler_params=pltpu.CompilerParams(
            dimension_semantics=("arbitrary",), vmem_limit_bytes=VMEM_LIMIT),
        name="merge",
    )(x2, oTa, oTb, gates, w["wa"], w["wb"], w["wo"], w["bg"], w["ln1g"], w["ln1b"])


def _ffn_kernel(x_ref, win_ref, wdown_ref, lg_ref, lb_ref, o_ref, *, alpha):
    x = x_ref[...]
    xb = x.astype(BF16)
    acc = alpha * x
    for c in range(FFN_HIDDEN // FFN_TF):
        cs = slice(c * FFN_TF, (c + 1) * FFN_TF)
        gate = _dot(xb, win_ref[:, cs])
        up = _dot(xb, win_ref[:, FFN_HIDDEN + c * FFN_TF:FFN_HIDDEN + (c + 1) * FFN_TF])
        act = (gate * jax.nn.sigmoid(gate) * up).astype(BF16)
        acc = acc + _dot(act, wdown_ref[cs, :])
    o_ref[...] = _layer_norm(acc, lg_ref[...], lb_ref[...])


def _ffn_call(x1, w, alpha):
    n_tok = x1.shape[0]
    tm = FFN_TM
    tok = lambda i: (i, 0)
    return pl.pallas_call(
        functools.partial(_ffn_kernel, alpha=alpha),
        grid=(n_tok // tm,),
        in_specs=[
            pl.BlockSpec((tm, D_MODEL), tok),
            _resident(w["wffn_in"].shape), _resident(w["wdown"].shape),
            _resident(w["ln2g"].shape), _resident(w["ln2b"].shape),
        ],
        out_specs=pl.BlockSpec((tm, D_MODEL), tok),
        out_shape=jax.ShapeDtypeStruct((n_tok, D_MODEL), F32),
        compiler_params=pltpu.CompilerParams(
            dimension_semantics=("arbitrary",), vmem_limit_bytes=VMEM_LIMIT),
        name="ffn",
    )(x1, w["wffn_in"], w["wdown"], w["ln2g"], w["ln2b"])


def _rope_swap(cols):
    q = MLA_ROPE // 4
    r1, r2, c1, c2 = cols[..., :q], cols[..., q:2 * q], cols[..., 2 * q:3 * q], cols[..., 3 * q:]
    return jnp.concatenate([-r2, r1, -c2, c1], axis=-1)


def _rope_tables(seq, q_scale):
    half = MLA_ROPE // 2
    quarter = MLA_ROPE // 4
    n_rows = seq // GRID_W
    inv_freq = ROPE_THETA ** (-jnp.arange(0, half, 2, dtype=F32) / half)
    freq = jnp.tile(inv_freq, LANES // quarter)
    coord = jnp.arange(max(n_rows, GRID_W), dtype=F32)
    ang = coord[:, None] * freq[None, :]
    cos_c, sin_c = jnp.cos(ang), jnp.sin(ang)
    d = np.arange(LANES)
    by_row = (d < half)[None, None, :]
    live = (d < MLA_ROPE)[None, None, :]

    def natural(tab):
        per_tok = jnp.where(by_row, tab[:n_rows, None, :], tab[None, :GRID_W, :])
        return jnp.where(live, per_tok, 0.0).reshape(seq, LANES)

    r = d - MLA_NOPE
    by_row_t = ((r >= 0) & (r < half))[:, None, None]
    rot = ((r >= 0) & (r < MLA_ROPE))[:, None, None]

    def transposed(tab, nope_value):
        tab_t = tab.T * q_scale
        per_tok = jnp.where(by_row_t, tab_t[:, :n_rows, None], tab_t[:, None, :GRID_W])
        full = jnp.where((d < MLA_NOPE)[:, None, None], nope_value, jnp.where(rot, per_tok, 0.0))
        return full.reshape(LANES, seq).astype(F32)

    return natural(cos_c), natural(sin_c), transposed(cos_c, q_scale), transposed(sin_c, 0.0)


def _prep_layer(seq, w_in, b_gate, na_rpb, q_norm, w_uq, kv_norm, w_ukv, w_bna, w_bmla, w_out,
                ln1_g, ln1_b, w_ffn_in, w_ffn_out, ln2_g, ln2_b):
    o_q, o_k, o_v = 0, NA_WIDTH, 2 * NA_WIDTH
    o_ql = 3 * NA_WIDTH
    o_kvl = o_ql + MLA_Q_LORA
    o_kr = o_kvl + MLA_KV_LORA
    o_g = o_kr + MLA_ROPE
    w = {}
    na_scale = NA_HEAD_DIM ** -0.5 * LOG2_E
    w["wqT"] = (w_in[:, o_q:o_k] * na_scale).T.astype(BF16)
    w["wk"] = w_in[:, o_k:o_v].astype(BF16)
    w["wvT"] = w_in[:, o_v:o_ql].T.astype(BF16)
    k_rope_w = w_in[:, o_kr:o_g]
    pad = jnp.zeros((D_MODEL, LANES - MLA_ROPE), F32)
    w["wlat"] = jnp.concatenate(
        [w_in[:, o_ql:o_kr], k_rope_w, pad, _rope_swap(k_rope_w), pad], axis=1).astype(BF16)
    w["wg"] = w_in[:, o_g:].astype(BF16)
    w["qg"] = q_norm.reshape(1, MLA_Q_LORA)
    w["kvg"] = kv_norm.reshape(1, MLA_KV_LORA)

    uq = w_uq.reshape(MLA_Q_LORA, MLA_HEADS, MLA_QK)
    uq_pe = uq[:, :, MLA_NOPE:]
    uq_arr = jnp.concatenate([uq[:, :, :MLA_NOPE], uq_pe, _rope_swap(uq_pe)], axis=-1)
    w["wuqT"] = uq_arr.reshape(MLA_Q_LORA, MLA_HEADS * MLA_HEAD_PAD).T.astype(BF16)
    ukv = w_ukv.reshape(MLA_KV_LORA, MLA_HEADS, MLA_NOPE + MLA_V)
    uk_arr = jnp.concatenate(
        [ukv[:, :, :MLA_NOPE], jnp.zeros((MLA_KV_LORA, MLA_HEADS, MLA_HEAD_PAD - MLA_NOPE), F32)], axis=-1)
    w["wuk"] = uk_arr.reshape(MLA_KV_LORA, MLA_HEADS * MLA_HEAD_PAD).astype(BF16)
    w["wuvT"] = ukv[:, :, MLA_NOPE:].reshape(MLA_KV_LORA, MLA_WIDTH).T.astype(BF16)
    e = np.zeros((LANES, MLA_HEADS * MLA_HEAD_PAD), np.float32)
    for h in range(MLA_HEADS):
        e[np.arange(MLA_ROPE), h * MLA_HEAD_PAD + MLA_NOPE + np.arange(MLA_ROPE)] = 1.0
    w["e"] = jnp.asarray(e, BF16)

    q_scale = MLA_QK ** -0.5 * LOG2_E
    w["cn"], w["sn"], w["ctf"], w["stf"] = _rope_tables(seq, q_scale)

    w["na_bias"] = _na_bias_blocks(na_rpb)

    w["wa"] = w_bna.astype(BF16)
    w["wb"] = w_bmla.astype(BF16)
    w["wo"] = w_out.astype(BF16)
    w["bg"] = b_gate.reshape(1, 2 * D_MODEL)
    w["ln1g"] = ln1_g.reshape(1, D_MODEL)
    w["ln1b"] = ln1_b.reshape(1, D_MODEL)
    w["wffn_in"] = w_ffn_in.astype(BF16)
    w["wdown"] = w_ffn_out.astype(BF16)
    w["ln2g"] = ln2_g.reshape(1, D_MODEL)
    w["ln2b"] = ln2_b.reshape(1, D_MODEL)
    return w


def kernel(x, w_in, b_gate, na_rpb, mla_q_norm, mla_w_uq, mla_kv_norm, mla_w_ukv, w_branch_na, w_branch_mla, w_out, ln1_g, ln1_b, w_ffn_in, w_ffn_out, ln2_g, ln2_b):
    batch, seq, d = x.shape
    depth = w_in.shape[0]
    alpha = (2.0 * depth) ** 0.25
    n_tok = batch * seq
    x2 = x.reshape(n_tok, d)
    for l in range(depth):
        w = _prep_layer(seq, w_in[l], b_gate[l], na_rpb[l], mla_q_norm[l], mla_w_uq[l], mla_kv_norm[l],
                        mla_w_ukv[l], w_branch_na[l], w_branch_mla[l], w_out[l], ln1_g[l], ln1_b[l],
                        w_ffn_in[l], w_ffn_out[l], ln2_g[l], ln2_b[l])
        qT_na, k_na, vT_na, qT_mla, k_mla, vT_mla, gates = _proj_call(x2, w, n_tok, seq)
        oT_na = _na_call(qT_na, k_na, vT_na, w["na_bias"], batch, seq)
        oT_mla = _mla_call(qT_mla, k_mla, vT_mla, batch, seq)
        x1 = _merge_call(x2, oT_na, oT_mla, gates, w, alpha)
        x2 = _ffn_call(x1, w, alpha)
    return x2.reshape(batch, seq, d)
```

```python
import functools

import numpy as np
import jax
import jax.numpy as jnp
from jax import lax
from jax.experimental import pallas as pl
from jax.experimental.pallas import tpu as pltpu

F32 = jnp.float32
BF16 = jnp.bfloat16

D_MODEL = 1024
GRID_W = 64
NA_HEADS = 8
NA_HEAD_DIM = 64
NA_WIDTH = NA_HEADS * NA_HEAD_DIM
NA_WIN_ROWS = 8
NA_WIN_COLS = 16
MLA_HEADS = 8
MLA_Q_LORA = 256
MLA_KV_LORA = 128
MLA_NOPE = 64
MLA_ROPE = 32
MLA_QK = MLA_NOPE + MLA_ROPE
MLA_V = 64
MLA_WIDTH = MLA_HEADS * MLA_V
ROPE_THETA = 10000.0
FFN_HIDDEN = 2816
LN_EPS = 1e-5
RMS_EPS = 1e-6

LANES = 128
MLA_HEAD_PAD = LANES
VMEM_LIMIT = 56 * 1024 * 1024

PROJ_TM = 512
NA_Q_ROWS = 4
NA_KEY_ROWS = 12
NA_BLOCK_ROWS = 8
MLA_TQ = 512
MLA_KC = 512
HEAD_V = 64
SUM_ROWS = 16
V_EXT = HEAD_V + SUM_ROWS
LOG2_E = 1.4426950408889634
MERGE_TM = 1024
FFN_TM = 1024
FFN_TF = 256

NEG_BIG = -1e30

NT = (((1,), (1,)), ((), ()))
TN = (((0,), (0,)), ((), ()))


def _dot(a, b):
    return jnp.dot(a, b, preferred_element_type=F32)


def _dot_nt(a, b):
    return lax.dot_general(a, b, NT, preferred_element_type=F32)


def _dot_tn(a, b):
    return lax.dot_general(a, b, TN, preferred_element_type=F32)


def _resident(shape):
    nd = len(shape)
    return pl.BlockSpec(shape, lambda *_: (0,) * nd, pipeline_mode=pl.Buffered(1))


def _rms(v, g):
    return v * lax.rsqrt(jnp.mean(v * v, axis=-1, keepdims=True) + RMS_EPS) * g


def _store_v_ext(ref, vT):
    ones = jnp.ones((SUM_ROWS, vT.shape[1]), BF16)
    for h in range(vT.shape[0] // HEAD_V):
        ref[h * V_EXT:h * V_EXT + HEAD_V, :] = vT[h * HEAD_V:(h + 1) * HEAD_V].astype(BF16)
        ref[h * V_EXT + HEAD_V:(h + 1) * V_EXT, :] = ones


def _proj_kernel(x_ref, wqT_ref, wk_ref, wvT_ref, wlat_ref, wg_ref, qg_ref, kvg_ref,
                 wuqT_ref, wuk_ref, e_ref, wuvT_ref, cn_ref, sn_ref, ctf_ref, stf_ref,
                 qT_na_ref, k_na_ref, vT_na_ref, qT_mla_ref, k_mla_ref, vT_mla_ref, g_ref):
    xb = x_ref[...].astype(BF16)
    qT_na_ref[...] = _dot_nt(wqT_ref[...], xb).astype(BF16)
    k_na_ref[...] = _dot(xb, wk_ref[...]).astype(BF16)
    _store_v_ext(vT_na_ref, _dot_nt(wvT_ref[...], xb))
    g_ref[...] = _dot(xb, wg_ref[...]).astype(BF16)

    lat = _dot(xb, wlat_ref[...])
    qn = _rms(lat[:, :MLA_Q_LORA], qg_ref[...]).astype(BF16)
    kvn = _rms(lat[:, MLA_Q_LORA:MLA_Q_LORA + MLA_KV_LORA], kvg_ref[...]).astype(BF16)
    kpe = lat[:, 384:512] * cn_ref[...] + lat[:, 512:640] * sn_ref[...]
    k_mla = _dot(kvn, wuk_ref[...]) + _dot(kpe.astype(BF16), e_ref[...])
    for h in range(MLA_HEADS):
        k_mla_ref[h] = k_mla[:, h * MLA_HEAD_PAD:(h + 1) * MLA_HEAD_PAD].astype(BF16)
    _store_v_ext(vT_mla_ref, _dot_nt(wuvT_ref[...], kvn))

    qT = _dot_nt(wuqT_ref[...], qn)
    ctf = ctf_ref[...]
    stf = stf_ref[...]
    for h in range(MLA_HEADS):
        qh = qT[h * MLA_HEAD_PAD:(h + 1) * MLA_HEAD_PAD]
        sw = jnp.concatenate([qh[MLA_ROPE:], qh[:MLA_ROPE]], axis=0)
        qT_mla_ref[h * MLA_HEAD_PAD:(h + 1) * MLA_HEAD_PAD, :] = (qh * ctf + sw * stf).astype(BF16)


def _proj_call(x2, w, n_tok, seq):
    tm = PROJ_TM
    steps_per_seq = seq // tm
    tok = lambda i: (i, 0)
    tokT = lambda i: (0, i)
    tab = lambda i: (i % steps_per_seq, 0)
    tabT = lambda i: (0, i % steps_per_seq)
    in_specs = [
        pl.BlockSpec((tm, D_MODEL), tok),
        _resident(w["wqT"].shape), _resident(w["wk"].shape), _resident(w["wvT"].shape),
        _resident(w["wlat"].shape), _resident(w["wg"].shape),
        _resident(w["qg"].shape), _resident(w["kvg"].shape),
        _resident(w["wuqT"].shape), _resident(w["wuk"].shape), _resident(w["e"].shape),
        _resident(w["wuvT"].shape),
        pl.BlockSpec((tm, LANES), tab), pl.BlockSpec((tm, LANES), tab),
        pl.BlockSpec((MLA_HEAD_PAD, tm), tabT), pl.BlockSpec((MLA_HEAD_PAD, tm), tabT),
    ]
    out_shape = [
        jax.ShapeDtypeStruct((NA_WIDTH, n_tok), BF16),
        jax.ShapeDtypeStruct((n_tok, NA_WIDTH), BF16),
        jax.ShapeDtypeStruct((NA_HEADS * V_EXT, n_tok), BF16),
        jax.ShapeDtypeStruct((MLA_HEADS * MLA_HEAD_PAD, n_tok), BF16),
        jax.ShapeDtypeStruct((MLA_HEADS, n_tok, MLA_HEAD_PAD), BF16),
        jax.ShapeDtypeStruct((MLA_HEADS * V_EXT, n_tok), BF16),
        jax.ShapeDtypeStruct((n_tok, 2 * D_MODEL), BF16),
    ]
    out_specs = [
        pl.BlockSpec((NA_WIDTH, tm), tokT),
        pl.BlockSpec((tm, NA_WIDTH), tok),
        pl.BlockSpec((NA_HEADS * V_EXT, tm), tokT),
        pl.BlockSpec((MLA_HEADS * MLA_HEAD_PAD, tm), tokT),
        pl.BlockSpec((MLA_HEADS, tm, MLA_HEAD_PAD), lambda i: (0, i, 0)),
        pl.BlockSpec((MLA_HEADS * V_EXT, tm), tokT),
        pl.BlockSpec((tm, 2 * D_MODEL), tok),
    ]
    return pl.pallas_call(
        _proj_kernel,
        grid=(n_tok // tm,),
        in_specs=in_specs,
        out_specs=out_specs,
        out_shape=out_shape,
        compiler_params=pltpu.CompilerParams(
            dimension_semantics=("arbitrary",), vmem_limit_bytes=VMEM_LIMIT),
        name="proj",
    )(x2, w["wqT"], w["wk"], w["wvT"], w["wlat"], w["wg"], w["qg"], w["kvg"],
      w["wuqT"], w["wuk"], w["e"], w["wuvT"], w["cn"], w["sn"], w["ctf"], w["stf"])


NA_GROUP_TYPES = 3


def _na_group_geometry(rows):
    return [(0, 0), (NA_Q_ROWS, 0), (rows - NA_Q_ROWS, rows - NA_KEY_ROWS)]


NA_MASKED_BLOCK = 2 * NA_WIN_ROWS - 1


def _na_block_index(rows, kr, qi):
    out = []
    for r0, ws in _na_group_geometry(rows):
        q_row, k_row = r0 + qi, ws + kr
        r_start = min(max(q_row - NA_WIN_ROWS // 2, 0), rows - NA_WIN_ROWS)
        valid = r_start <= k_row < r_start + NA_WIN_ROWS
        out.append(k_row - q_row + NA_WIN_ROWS - 1 if valid else NA_MASKED_BLOCK)
    return out


def _na_bias_blocks(rpb):
    n_dr = 2 * NA_WIN_ROWS - 1
    n_dc = 2 * NA_WIN_COLS - 1
    w = GRID_W
    lead = w - NA_WIN_COLS
    v = jnp.pad(rpb, ((0, 0), (0, 0), (lead, 2 * w - lead - n_dc)))
    a = jnp.tile(v, (1, 1, w))[:, :, :w * (2 * w - 1)].reshape(NA_HEADS, n_dr, w, 2 * w - 1)
    toe = jnp.swapaxes(a[:, :, :, w - 1:], 2, 3)
    col = np.arange(w)
    col_start = np.clip(col - NA_WIN_COLS // 2, 0, w - NA_WIN_COLS)
    v_col = (col[:, None] >= col_start[None, :]) & (col[:, None] < col_start[None, :] + NA_WIN_COLS)
    toe = jnp.where(v_col[None, None], toe * LOG2_E, NEG_BIG)
    toe = jnp.concatenate([toe, jnp.full((NA_HEADS, 1, w, w), NEG_BIG, F32)], axis=1)
    return jnp.concatenate([toe, toe], axis=-1)


def _na_kernel(qT_ref, k_ref, vT_ref, bias_ref, oT_ref, *, rows):
    blk = pl.program_id(1)
    nk = NA_KEY_ROWS * GRID_W
    nq = NA_Q_ROWS * GRID_W
    row_id = lax.broadcasted_iota(jnp.int32, (2 * NA_HEAD_DIM, nq), 0)
    left_half = lax.broadcasted_iota(jnp.int32, (GRID_W, LANES), 1) < GRID_W
    items = [(g, h) for g in range(NA_BLOCK_ROWS // NA_Q_ROWS) for h in range(NA_HEADS)]

    def window(g):
        r0 = blk * NA_BLOCK_ROWS + g * NA_Q_ROWS
        ws = jnp.clip(r0 - NA_WIN_ROWS // 2, 0, rows - NA_KEY_ROWS)
        typ = jnp.where(r0 == 0, 0, jnp.where(r0 == rows - NA_Q_ROWS, 2, 1))
        return pl.multiple_of(ws * GRID_W, 2 * LANES), typ

    def bias_tile(h, typ):
        def block(kr, qi):
            first, interior, last = _na_block_index(rows, kr, qi)
            idx = jnp.where(typ == 0, first, jnp.where(typ == 1, interior, last))
            return bias_ref[h, idx]
        key_rows = []
        for kr in range(NA_KEY_ROWS):
            pairs = [jnp.where(left_half, block(kr, qi), block(kr, qi + 1)) for qi in range(0, NA_Q_ROWS, 2)]
            key_rows.append(jnp.concatenate(pairs, axis=1))
        return jnp.concatenate(key_rows, axis=0)

    def scores(g, h):
        tok0, typ = window(g)
        hp, sub = divmod(h, 2)
        lanes = slice(hp * LANES, (hp + 1) * LANES)
        k_pair = k_ref[pl.ds(tok0, nk), lanes]
        q_pair = qT_ref[lanes, g * nq:(g + 1) * nq]
        keep = (row_id >= sub * NA_HEAD_DIM) & (row_id < (sub + 1) * NA_HEAD_DIM)
        q_h = jnp.where(keep, q_pair, jnp.zeros_like(q_pair))
        s = (_dot(k_pair, q_h) + bias_tile(h, typ)).astype(BF16)
        return s, jnp.max(s, axis=0, keepdims=True)

    def finish(g, h, s, m):
        tok0, _ = window(g)
        hv = slice(h * NA_HEAD_DIM, (h + 1) * NA_HEAD_DIM)
        p = jnp.exp2(s - m)
        acc = _dot(vT_ref[h * V_EXT:(h + 1) * V_EXT, pl.ds(tok0, nk)], p)
        o = acc[:NA_HEAD_DIM] * (1.0 / acc[NA_HEAD_DIM:NA_HEAD_DIM + 1])
        oT_ref[hv, g * nq:(g + 1) * nq] = o.astype(BF16)

    cur = scores(*items[0])
    for i, item in enumerate(items):
        nxt = scores(*items[i + 1]) if i + 1 < len(items) else None
        finish(*item, *cur)
        cur = nxt


def _na_call(qT, k, vT, bias, batch, seq):
    rows = seq // GRID_W
    blocks = rows // NA_BLOCK_ROWS
    tq = NA_BLOCK_ROWS * GRID_W
    n_tok = batch * seq
    return pl.pallas_call(
        functools.partial(_na_kernel, rows=rows),
        grid=(batch, blocks),
        in_specs=[
            pl.BlockSpec((NA_WIDTH, tq), lambda b, i: (0, b * blocks + i)),
            pl.BlockSpec((seq, NA_WIDTH), lambda b, i: (b, 0)),
            pl.BlockSpec((NA_HEADS * V_EXT, seq), lambda b, i: (0, b)),
            _resident(bias.shape),
        ],
        out_specs=pl.BlockSpec((NA_WIDTH, tq), lambda b, i: (0, b * blocks + i)),
        out_shape=jax.ShapeDtypeStruct((NA_WIDTH, n_tok), BF16),
        compiler_params=pltpu.CompilerParams(
            dimension_semantics=("arbitrary", "arbitrary"), vmem_limit_bytes=VMEM_LIMIT),
        name="na",
    )(qT, k, vT, bias)


def _mla_kernel(qT_ref, k_ref, vT_ref, oT_ref, s_even_ref, s_odd_ref, *, seq):
    n_chunks = seq // MLA_KC
    tq = oT_ref.shape[1]
    s_refs = (s_even_ref, s_odd_ref)

    def score_chunk(h, c, slot, m):
        hk = slice(h * MLA_HEAD_PAD, (h + 1) * MLA_HEAD_PAD)
        s = _dot(k_ref[h, c * MLA_KC:(c + 1) * MLA_KC, :], qT_ref[hk, :]).astype(BF16)
        s_refs[slot][c * MLA_KC:(c + 1) * MLA_KC, :] = s
        return jnp.maximum(m, jnp.max(s, axis=0, keepdims=True))

    neg = jnp.full((1, tq), -jnp.inf, BF16)
    m_next = neg
    for c in range(n_chunks):
        m_next = score_chunk(0, c, 0, m_next)
    for h in range(MLA_HEADS):
        slot = h % 2
        m_cur, m_next = m_next, neg
        hv = slice(h * MLA_V, (h + 1) * MLA_V)
        acc = jnp.zeros((MLA_V + SUM_ROWS, tq), F32)
        for c in range(n_chunks):
            if h + 1 < MLA_HEADS:
                m_next = score_chunk(h + 1, c, 1 - slot, m_next)
            cs = slice(c * MLA_KC, (c + 1) * MLA_KC)
            p = jnp.exp2(s_refs[slot][cs, :] - m_cur)
            acc = acc + _dot(vT_ref[h * V_EXT:(h + 1) * V_EXT, cs], p)
        oT_ref[hv, :] = (acc[:MLA_V] * (1.0 / acc[MLA_V:MLA_V + 1])).astype(BF16)


def _mla_call(qT, k, vT, batch, seq):
    nq = seq // MLA_TQ
    n_tok = batch * seq
    return pl.pallas_call(
        functools.partial(_mla_kernel, seq=seq),
        grid=(batch, nq),
        in_specs=[
            pl.BlockSpec((MLA_HEADS * MLA_HEAD_PAD, MLA_TQ), lambda b, i: (0, b * nq + i)),
            pl.BlockSpec((MLA_HEADS, seq, MLA_HEAD_PAD), lambda b, i: (0, b, 0)),
            pl.BlockSpec((MLA_HEADS * V_EXT, seq), lambda b, i: (0, b)),
        ],
        out_specs=pl.BlockSpec((MLA_WIDTH, MLA_TQ), lambda b, i: (0, b * nq + i)),
        out_shape=jax.ShapeDtypeStruct((MLA_WIDTH, n_tok), BF16),
        scratch_shapes=[pltpu.VMEM((seq, MLA_TQ), BF16), pltpu.VMEM((seq, MLA_TQ), BF16)],
        compiler_params=pltpu.CompilerParams(
            dimension_semantics=("arbitrary", "arbitrary"), vmem_limit_bytes=VMEM_LIMIT),
        name="mla",
    )(qT, k, vT)


def _layer_norm(v, g, b):
    mu = jnp.mean(v, axis=-1, keepdims=True)
    c = v - mu
    var = jnp.mean(c * c, axis=-1, keepdims=True)
    return c * lax.rsqrt(var + LN_EPS) * g + b


def _merge_kernel(x_ref, oa_ref, ob_ref, g_ref, wa_ref, wb_ref, wo_ref, bg_ref, lg_ref, lb_ref,
                  o_ref, *, alpha):
    y_a = _dot_tn(oa_ref[...], wa_ref[...])
    y_b = _dot_tn(ob_ref[...], wb_ref[...])
    g = g_ref[...].astype(F32) + bg_ref[...]
    merged = jax.nn.sigmoid(g[:, :D_MODEL]) * y_a + jax.nn.sigmoid(g[:, D_MODEL:]) * y_b
    mix = _dot(merged.astype(BF16), wo_ref[...])
    o_ref[...] = _layer_norm(alpha * x_ref[...] + mix, lg_ref[...], lb_ref[...])


def _merge_call(x2, oTa, oTb, gates, w, alpha):
    n_tok = x2.shape[0]
    tm = MERGE_TM
    tok = lambda i: (i, 0)
    tokT = lambda i: (0, i)
    return pl.pallas_call(
        functools.partial(_merge_kernel, alpha=alpha),
        grid=(n_tok // tm,),
        in_specs=[
            pl.BlockSpec((tm, D_MODEL), tok),
            pl.BlockSpec((NA_WIDTH, tm), tokT),
            pl.BlockSpec((MLA_WIDTH, tm), tokT),
            pl.BlockSpec((tm, 2 * D_MODEL), tok),
            _resident(w["wa"].shape), _resident(w["wb"].shape), _resident(w["wo"].shape),
            _resident(w["bg"].shape), _resident(w["ln1g"].shape), _resident(w["ln1b"].shape),
        ],
        out_specs=pl.BlockSpec((tm, D_MODEL), tok),
        out_shape=jax.ShapeDtypeStruct((n_tok, D_MODEL), F32),
        compiler_params=pltpu.CompilerParams(
            dimension_semantics=("arbitrary",), vmem_limit_bytes=VMEM_LIMIT),
        name="merge",
    )(x2, oTa, oTb, gates, w["wa"], w["wb"], w["wo"], w["bg"], w["ln1g"], w["ln1b"])


def _ffn_kernel(x_ref, win_ref, wdown_ref, lg_ref, lb_ref, o_ref, *, alpha):
    x = x_ref[...]
    xb = x.astype(BF16)
    acc = alpha * x
    for c in range(FFN_HIDDEN // FFN_TF):
        cs = slice(c * FFN_TF, (c + 1) * FFN_TF)
        gate = _dot(xb, win_ref[:, cs])
        up = _dot(xb, win_ref[:, FFN_HIDDEN + c * FFN_TF:FFN_HIDDEN + (c + 1) * FFN_TF])
        act = (gate * jax.nn.sigmoid(gate) * up).astype(BF16)
        acc = acc + _dot(act, wdown_ref[cs, :])
    o_ref[...] = _layer_norm(acc, lg_ref[...], lb_ref[...])


def _ffn_call(x1, w, alpha):
    n_tok = x1.shape[0]
    tm = FFN_TM
    tok = lambda i: (i, 0)
    return pl.pallas_call(
        functools.partial(_ffn_kernel, alpha=alpha),
        grid=(n_tok // tm,),
        in_specs=[
            pl.BlockSpec((tm, D_MODEL), tok),
            _resident(w["wffn_in"].shape), _resident(w["wdown"].shape),
            _resident(w["ln2g"].shape), _resident(w["ln2b"].shape),
        ],
        out_specs=pl.BlockSpec((tm, D_MODEL), tok),
        out_shape=jax.ShapeDtypeStruct((n_tok, D_MODEL), F32),
        compiler_params=pltpu.CompilerParams(
            dimension_semantics=("arbitrary",), vmem_limit_bytes=VMEM_LIMIT),
        name="ffn",
    )(x1, w["wffn_in"], w["wdown"], w["ln2g"], w["ln2b"])


def _rope_swap(cols):
    q = MLA_ROPE // 4
    r1, r2, c1, c2 = cols[..., :q], cols[..., q:2 * q], cols[..., 2 * q:3 * q], cols[..., 3 * q:]
    return jnp.concatenate([-r2, r1, -c2, c1], axis=-1)


def _rope_tables(seq, q_scale):
    half = MLA_ROPE // 2
    quarter = MLA_ROPE // 4
    n_rows = seq // GRID_W
    inv_freq = ROPE_THETA ** (-jnp.arange(0, half, 2, dtype=F32) / half)
    freq = jnp.tile(inv_freq, LANES // quarter)
    coord = jnp.arange(max(n_rows, GRID_W), dtype=F32)
    ang = coord[:, None] * freq[None, :]
    cos_c, sin_c = jnp.cos(ang), jnp.sin(ang)
    d = np.arange(LANES)
    by_row = (d < half)[None, None, :]
    live = (d < MLA_ROPE)[None, None, :]

    def natural(tab):
        per_tok = jnp.where(by_row, tab[:n_rows, None, :], tab[None, :GRID_W, :])
        return jnp.where(live, per_tok, 0.0).reshape(seq, LANES)

    r = d - MLA_NOPE
    by_row_t = ((r >= 0) & (r < half))[:, None, None]
    rot = ((r >= 0) & (r < MLA_ROPE))[:, None, None]

    def transposed(tab, nope_value):
        tab_t = tab.T * q_scale
        per_tok = jnp.where(by_row_t, tab_t[:, :n_rows, None], tab_t[:, None, :GRID_W])
        full = jnp.where((d < MLA_NOPE)[:, None, None], nope_value, jnp.where(rot, per_tok, 0.0))
        return full.reshape(LANES, seq).astype(F32)

    return natural(cos_c), natural(sin_c), transposed(cos_c, q_scale), transposed(sin_c, 0.0)


def _prep_layer(seq, w_in, b_gate, na_rpb, q_norm, w_uq, kv_norm, w_ukv, w_bna, w_bmla, w_out,
                ln1_g, ln1_b, w_ffn_in, w_ffn_out, ln2_g, ln2_b):
    o_q, o_k, o_v = 0, NA_WIDTH, 2 * NA_WIDTH
    o_ql = 3 * NA_WIDTH
    o_kvl = o_ql + MLA_Q_LORA
    o_kr = o_kvl + MLA_KV_LORA
    o_g = o_kr + MLA_ROPE
    w = {}
    na_scale = NA_HEAD_DIM ** -0.5 * LOG2_E
    w["wqT"] = (w_in[:, o_q:o_k] * na_scale).astype(BF16).T
    w["wk"] = w_in[:, o_k:o_v].astype(BF16)
    w["wvT"] = w_in[:, o_v:o_ql].astype(BF16).T
    k_rope_w = w_in[:, o_kr:o_g]
    pad = jnp.zeros((D_MODEL, LANES - MLA_ROPE), F32)
    w["wlat"] = jnp.concatenate(
        [w_in[:, o_ql:o_kr], k_rope_w, pad, _rope_swap(k_rope_w), pad], axis=1).astype(BF16)
    w["wg"] = w_in[:, o_g:].astype(BF16)
    w["qg"] = q_norm.reshape(1, MLA_Q_LORA)
    w["kvg"] = kv_norm.reshape(1, MLA_KV_LORA)

    uq = w_uq.reshape(MLA_Q_LORA, MLA_HEADS, MLA_QK)
    uq_pe = uq[:, :, MLA_NOPE:]
    uq_arr = jnp.concatenate([uq[:, :, :MLA_NOPE], uq_pe, _rope_swap(uq_pe)], axis=-1)
    w["wuqT"] = uq_arr.reshape(MLA_Q_LORA, MLA_HEADS * MLA_HEAD_PAD).T.astype(BF16)
    ukv = w_ukv.reshape(MLA_KV_LORA, MLA_HEADS, MLA_NOPE + MLA_V)
    uk_arr = jnp.concatenate(
        [ukv[:, :, :MLA_NOPE], jnp.zeros((MLA_KV_LORA, MLA_HEADS, MLA_HEAD_PAD - MLA_NOPE), F32)], axis=-1)
    w["wuk"] = uk_arr.reshape(MLA_KV_LORA, MLA_HEADS * MLA_HEAD_PAD).astype(BF16)
    w["wuvT"] = ukv[:, :, MLA_NOPE:].reshape(MLA_KV_LORA, MLA_WIDTH).T.astype(BF16)
    e = np.zeros((LANES, MLA_HEADS * MLA_HEAD_PAD), np.float32)
    for h in range(MLA_HEADS):
        e[np.arange(MLA_ROPE), h * MLA_HEAD_PAD + MLA_NOPE + np.arange(MLA_ROPE)] = 1.0
    w["e"] = jnp.asarray(e, BF16)

    q_scale = MLA_QK ** -0.5 * LOG2_E
    w["cn"], w["sn"], w["ctf"], w["stf"] = _rope_tables(seq, q_scale)

    w["na_bias"] = _na_bias_blocks(na_rpb)

    w["wa"] = w_bna.astype(BF16)
    w["wb"] = w_bmla.astype(BF16)
    w["wo"] = w_out.astype(BF16)
    w["bg"] = b_gate.reshape(1, 2 * D_MODEL)
    w["ln1g"] = ln1_g.reshape(1, D_MODEL)
    w["ln1b"] = ln1_b.reshape(1, D_MODEL)
    w["wffn_in"] = w_ffn_in.astype(BF16)
    w["wdown"] = w_ffn_out.astype(BF16)
    w["ln2g"] = ln2_g.reshape(1, D_MODEL)
    w["ln2b"] = ln2_b.reshape(1, D_MODEL)
    return w


def kernel(x, w_in, b_gate, na_rpb, mla_q_norm, mla_w_uq, mla_kv_norm, mla_w_ukv, w_branch_na, w_branch_mla, w_out, ln1_g, ln1_b, w_ffn_in, w_ffn_out, ln2_g, ln2_b):
    batch, seq, d = x.shape
    depth = w_in.shape[0]
    alpha = (2.0 * depth) ** 0.25
    n_tok = batch * seq
    x2 = x.reshape(n_tok, d)
    for l in range(depth):
        w = _prep_layer(seq, w_in[l], b_gate[l], na_rpb[l], mla_q_norm[l], mla_w_uq[l], mla_kv_norm[l],
                        mla_w_ukv[l], w_branch_na[l], w_branch_mla[l], w_out[l], ln1_g[l], ln1_b[l],
                        w_ffn_in[l], w_ffn_out[l], ln2_g[l], ln2_b[l])
        qT_na, k_na, vT_na, qT_mla, k_mla, vT_mla, gates = _proj_call(x2, w, n_tok, seq)
        oT_na = _na_call(qT_na, k_na, vT_na, w["na_bias"], batch, seq)
        oT_mla = _mla_call(qT_mla, k_mla, vT_mla, batch, seq)
        x1 = _merge_call(x2, oT_na, oT_mla, gates, w, alpha)
        x2 = _ffn_call(x1, w, alpha)
    return x2.reshape(batch, seq, d)
```

```python
import functools

import numpy as np
import jax
import jax.numpy as jnp
from jax import lax
from jax.experimental import pallas as pl
from jax.experimental.pallas import tpu as pltpu

F32 = jnp.float32
BF16 = jnp.bfloat16

D_MODEL = 1024
GRID_W = 64
NA_HEADS = 8
NA_HEAD_DIM = 64
NA_WIDTH = NA_HEADS * NA_HEAD_DIM
NA_WIN_ROWS = 8
NA_WIN_COLS = 16
MLA_HEADS = 8
MLA_Q_LORA = 256
MLA_KV_LORA = 128
MLA_NOPE = 64
MLA_ROPE = 32
MLA_QK = MLA_NOPE + MLA_ROPE
MLA_V = 64
MLA_WIDTH = MLA_HEADS * MLA_V
ROPE_THETA = 10000.0
FFN_HIDDEN = 2816
LN_EPS = 1e-5
RMS_EPS = 1e-6

LANES = 128
MLA_HEAD_PAD = LANES
VMEM_LIMIT = 56 * 1024 * 1024

PROJ_TM = 512
NA_Q_ROWS = 4
NA_KEY_ROWS = 12
NA_BLOCK_ROWS = 16
MLA_TQ = 512
MLA_KC = 512
HEAD_V = 64
SUM_ROWS = 16
V_EXT = HEAD_V + SUM_ROWS
LOG2_E = 1.4426950408889634
MERGE_TM = 1024
FFN_TM = 1024
FFN_TF = 256

NEG_BIG = -1e30

NT = (((1,), (1,)), ((), ()))
TN = (((0,), (0,)), ((), ()))


def _dot(a, b):
    return jnp.dot(a, b, preferred_element_type=F32)


def _dot_nt(a, b):
    return lax.dot_general(a, b, NT, preferred_element_type=F32)


def _dot_tn(a, b):
    return lax.dot_general(a, b, TN, preferred_element_type=F32)


def _resident(shape):
    nd = len(shape)
    return pl.BlockSpec(shape, lambda *_: (0,) * nd, pipeline_mode=pl.Buffered(1))


def _rms(v, g):
    return v * lax.rsqrt(jnp.mean(v * v, axis=-1, keepdims=True) + RMS_EPS) * g


def _store_v_ext(ref, vT):
    ones = jnp.ones((SUM_ROWS, vT.shape[1]), BF16)
    for h in range(vT.shape[0] // HEAD_V):
        ref[h * V_EXT:h * V_EXT + HEAD_V, :] = vT[h * HEAD_V:(h + 1) * HEAD_V].astype(BF16)
        ref[h * V_EXT + HEAD_V:(h + 1) * V_EXT, :] = ones


def _proj_kernel(x_ref, wqT_ref, wkT_ref, wvT_ref, wlatT_ref, wgT_ref, qg_ref, kvg_ref,
                 wuqT_ref, wuk_ref, e_ref, wuvT_ref, cn_ref, sn_ref, ctf_ref, stf_ref,
                 qT_na_ref, k_na_ref, vT_na_ref, qT_mla_ref, k_mla_ref, vT_mla_ref, g_ref):
    xb = x_ref[...].astype(BF16)
    qT_na_ref[...] = _dot_nt(wqT_ref[...], xb).astype(BF16)
    k_na_ref[...] = _dot_nt(xb, wkT_ref[...]).astype(BF16)
    _store_v_ext(vT_na_ref, _dot_nt(wvT_ref[...], xb))
    g_ref[...] = _dot_nt(xb, wgT_ref[...]).astype(BF16)

    lat = _dot_nt(xb, wlatT_ref[...])
    qn = _rms(lat[:, :MLA_Q_LORA], qg_ref[...]).astype(BF16)
    kvn = _rms(lat[:, MLA_Q_LORA:MLA_Q_LORA + MLA_KV_LORA], kvg_ref[...]).astype(BF16)
    kpe = lat[:, 384:512] * cn_ref[...] + lat[:, 512:640] * sn_ref[...]
    k_mla = _dot(kvn, wuk_ref[...]) + _dot(kpe.astype(BF16), e_ref[...])
    for h in range(MLA_HEADS):
        k_mla_ref[h] = k_mla[:, h * MLA_HEAD_PAD:(h + 1) * MLA_HEAD_PAD].astype(BF16)
    _store_v_ext(vT_mla_ref, _dot_nt(wuvT_ref[...], kvn))

    qT = _dot_nt(wuqT_ref[...], qn)
    ctf = ctf_ref[...]
    stf = stf_ref[...]
    for h in range(MLA_HEADS):
        qh = qT[h * MLA_HEAD_PAD:(h + 1) * MLA_HEAD_PAD]
        sw = jnp.concatenate([qh[MLA_ROPE:], qh[:MLA_ROPE]], axis=0)
        qT_mla_ref[h * MLA_HEAD_PAD:(h + 1) * MLA_HEAD_PAD, :] = (qh * ctf + sw * stf).astype(BF16)


def _proj_call(x2, w, n_tok, seq):
    tm = PROJ_TM
    steps_per_seq = seq // tm
    tok = lambda i: (i, 0)
    tokT = lambda i: (0, i)
    tab = lambda i: (i % steps_per_seq, 0)
    tabT = lambda i: (0, i % steps_per_seq)
    in_specs = [
        pl.BlockSpec((tm, D_MODEL), tok),
        _resident(w["wqT"].shape), _resident(w["wkT"].shape), _resident(w["wvT"].shape),
        _resident(w["wlatT"].shape), _resident(w["wgT"].shape),
        _resident(w["qg"].shape), _resident(w["kvg"].shape),
        _resident(w["wuqT"].shape), _resident(w["wuk"].shape), _resident(w["e"].shape),
        _resident(w["wuvT"].shape),
        pl.BlockSpec((tm, LANES), tab), pl.BlockSpec((tm, LANES), tab),
        pl.BlockSpec((MLA_HEAD_PAD, tm), tabT), pl.BlockSpec((MLA_HEAD_PAD, tm), tabT),
    ]
    out_shape = [
        jax.ShapeDtypeStruct((NA_WIDTH, n_tok), BF16),
        jax.ShapeDtypeStruct((n_tok, NA_WIDTH), BF16),
        jax.ShapeDtypeStruct((NA_HEADS * V_EXT, n_tok), BF16),
        jax.ShapeDtypeStruct((MLA_HEADS * MLA_HEAD_PAD, n_tok), BF16),
        jax.ShapeDtypeStruct((MLA_HEADS, n_tok, MLA_HEAD_PAD), BF16),
        jax.ShapeDtypeStruct((MLA_HEADS * V_EXT, n_tok), BF16),
        jax.ShapeDtypeStruct((n_tok, 2 * D_MODEL), BF16),
    ]
    out_specs = [
        pl.BlockSpec((NA_WIDTH, tm), tokT),
        pl.BlockSpec((tm, NA_WIDTH), tok),
        pl.BlockSpec((NA_HEADS * V_EXT, tm), tokT),
        pl.BlockSpec((MLA_HEADS * MLA_HEAD_PAD, tm), tokT),
        pl.BlockSpec((MLA_HEADS, tm, MLA_HEAD_PAD), lambda i: (0, i, 0)),
        pl.BlockSpec((MLA_HEADS * V_EXT, tm), tokT),
        pl.BlockSpec((tm, 2 * D_MODEL), tok),
    ]
    return pl.pallas_call(
        _proj_kernel,
        grid=(n_tok // tm,),
        in_specs=in_specs,
        out_specs=out_specs,
        out_shape=out_shape,
        compiler_params=pltpu.CompilerParams(
            dimension_semantics=("arbitrary",), vmem_limit_bytes=VMEM_LIMIT),
        name="proj",
    )(x2, w["wqT"], w["wkT"], w["wvT"], w["wlatT"], w["wgT"], w["qg"], w["kvg"],
      w["wuqT"], w["wuk"], w["e"], w["wuvT"], w["cn"], w["sn"], w["ctf"], w["stf"])


NA_GROUP_TYPES = 3


def _na_group_geometry(rows):
    return [(0, 0), (NA_Q_ROWS, 0), (rows - NA_Q_ROWS, rows - NA_KEY_ROWS)]


NA_MASKED_BLOCK = 2 * NA_WIN_ROWS - 1


def _na_block_index(rows, kr, qi):
    out = []
    for r0, ws in _na_group_geometry(rows):
        q_row, k_row = r0 + qi, ws + kr
        r_start = min(max(q_row - NA_WIN_ROWS // 2, 0), rows - NA_WIN_ROWS)
        valid = r_start <= k_row < r_start + NA_WIN_ROWS
        out.append(k_row - q_row + NA_WIN_ROWS - 1 if valid else NA_MASKED_BLOCK)
    return out


def _na_bias_blocks(rpb):
    n_dr = 2 * NA_WIN_ROWS - 1
    n_dc = 2 * NA_WIN_COLS - 1
    w = GRID_W
    lead = w - NA_WIN_COLS
    v = jnp.pad(rpb, ((0, 0), (0, 0), (lead, 2 * w - lead - n_dc)))
    a = jnp.tile(v, (1, 1, w))[:, :, :w * (2 * w - 1)].reshape(NA_HEADS, n_dr, w, 2 * w - 1)
    toe = jnp.swapaxes(a[:, :, :, w - 1:], 2, 3)
    col = np.arange(w)
    col_start = np.clip(col - NA_WIN_COLS // 2, 0, w - NA_WIN_COLS)
    v_col = (col[:, None] >= col_start[None, :]) & (col[:, None] < col_start[None, :] + NA_WIN_COLS)
    toe = jnp.where(v_col[None, None], toe * LOG2_E, NEG_BIG)
    toe = jnp.concatenate([toe, jnp.full((NA_HEADS, 1, w, w), NEG_BIG, F32)], axis=1)
    return jnp.concatenate([toe, toe], axis=-1)


def _na_kernel(qT_ref, k_ref, vT_ref, bias_ref, oT_ref, *, rows):
    blk = pl.program_id(1)
    nk = NA_KEY_ROWS * GRID_W
    nq = NA_Q_ROWS * GRID_W
    row_id = lax.broadcasted_iota(jnp.int32, (2 * NA_HEAD_DIM, nq), 0)
    left_half = lax.broadcasted_iota(jnp.int32, (GRID_W, LANES), 1) < GRID_W
    items = [(g, h) for g in range(NA_BLOCK_ROWS // NA_Q_ROWS) for h in range(NA_HEADS)]

    def window(g):
        r0 = blk * NA_BLOCK_ROWS + g * NA_Q_ROWS
        ws = jnp.clip(r0 - NA_WIN_ROWS // 2, 0, rows - NA_KEY_ROWS)
        typ = jnp.where(r0 == 0, 0, jnp.where(r0 == rows - NA_Q_ROWS, 2, 1))
        return pl.multiple_of(ws * GRID_W, 2 * LANES), typ

    def bias_tile(h, typ):
        def block(kr, qi):
            first, interior, last = _na_block_index(rows, kr, qi)
            idx = jnp.where(typ == 0, first, jnp.where(typ == 1, interior, last))
            return bias_ref[h, idx]
        key_rows = []
        for kr in range(NA_KEY_ROWS):
            pairs = [jnp.where(left_half, block(kr, qi), block(kr, qi + 1)) for qi in range(0, NA_Q_ROWS, 2)]
            key_rows.append(jnp.concatenate(pairs, axis=1))
        return jnp.concatenate(key_rows, axis=0)

    def scores(g, h):
        tok0, typ = window(g)
        hp, sub = divmod(h, 2)
        lanes = slice(hp * LANES, (hp + 1) * LANES)
        k_pair = k_ref[pl.ds(tok0, nk), lanes]
        q_pair = qT_ref[lanes, g * nq:(g + 1) * nq]
        keep = (row_id >= sub * NA_HEAD_DIM) & (row_id < (sub + 1) * NA_HEAD_DIM)
        q_h = jnp.where(keep, q_pair, jnp.zeros_like(q_pair))
        s = (_dot(k_pair, q_h) + bias_tile(h, typ)).astype(BF16)
        return s, jnp.max(s, axis=0, keepdims=True)

    def finish(g, h, s, m):
        tok0, _ = window(g)
        hv = slice(h * NA_HEAD_DIM, (h + 1) * NA_HEAD_DIM)
        p = jnp.exp2(s - m)
        acc = _dot(vT_ref[h * V_EXT:(h + 1) * V_EXT, pl.ds(tok0, nk)], p)
        o = acc[:NA_HEAD_DIM] * (1.0 / acc[NA_HEAD_DIM:NA_HEAD_DIM + 1])
        oT_ref[hv, g * nq:(g + 1) * nq] = o.astype(BF16)

    cur = scores(*items[0])
    for i, item in enumerate(items):
        nxt = scores(*items[i + 1]) if i + 1 < len(items) else None
        finish(*item, *cur)
        cur = nxt


def _na_call(qT, k, vT, bias, batch, seq):
    rows = seq // GRID_W
    blocks = rows // NA_BLOCK_ROWS
    tq = NA_BLOCK_ROWS * GRID_W
    n_tok = batch * seq
    return pl.pallas_call(
        functools.partial(_na_kernel, rows=rows),
        grid=(batch, blocks),
        in_specs=[
            pl.BlockSpec((NA_WIDTH, tq), lambda b, i: (0, b * blocks + i)),
            pl.BlockSpec((seq, NA_WIDTH), lambda b, i: (b, 0)),
            pl.BlockSpec((NA_HEADS * V_EXT, seq), lambda b, i: (0, b)),
            _resident(bias.shape),
        ],
        out_specs=pl.BlockSpec((NA_WIDTH, tq), lambda b, i: (0, b * blocks + i)),
        out_shape=jax.ShapeDtypeStruct((NA_WIDTH, n_tok), BF16),
        compiler_params=pltpu.CompilerParams(
            dimension_semantics=("arbitrary", "arbitrary"), vmem_limit_bytes=VMEM_LIMIT),
        name="na",
    )(qT, k, vT, bias)


def _mla_kernel(qT_ref, k_ref, vT_ref, oT_ref, s_even_ref, s_odd_ref, *, seq):
    n_chunks = seq // MLA_KC
    tq = oT_ref.shape[1]
    s_refs = (s_even_ref, s_odd_ref)

    def score_chunk(h, c, slot, m):
        hk = slice(h * MLA_HEAD_PAD, (h + 1) * MLA_HEAD_PAD)
        s = _dot(k_ref[h, c * MLA_KC:(c + 1) * MLA_KC, :], qT_ref[hk, :]).astype(BF16)
        s_refs[slot][c * MLA_KC:(c + 1) * MLA_KC, :] = s
        return jnp.maximum(m, jnp.max(s, axis=0, keepdims=True))

    neg = jnp.full((1, tq), -jnp.inf, BF16)
    m_next = neg
    for c in range(n_chunks):
        m_next = score_chunk(0, c, 0, m_next)
    for h in range(MLA_HEADS):
        slot = h % 2
        m_cur, m_next = m_next, neg
        hv = slice(h * MLA_V, (h + 1) * MLA_V)
        acc = jnp.zeros((MLA_V + SUM_ROWS, tq), F32)
        for c in range(n_chunks):
            if h + 1 < MLA_HEADS:
                m_next = score_chunk(h + 1, c, 1 - slot, m_next)
            cs = slice(c * MLA_KC, (c + 1) * MLA_KC)
            p = jnp.exp2(s_refs[slot][cs, :] - m_cur)
            acc = acc + _dot(vT_ref[h * V_EXT:(h + 1) * V_EXT, cs], p)
        oT_ref[hv, :] = (acc[:MLA_V] * (1.0 / acc[MLA_V:MLA_V + 1])).astype(BF16)


def _mla_call(qT, k, vT, batch, seq):
    nq = seq // MLA_TQ
    n_tok = batch * seq
    return pl.pallas_call(
        functools.partial(_mla_kernel, seq=seq),
        grid=(batch, nq),
        in_specs=[
            pl.BlockSpec((MLA_HEADS * MLA_HEAD_PAD, MLA_TQ), lambda b, i: (0, b * nq + i)),
            pl.BlockSpec((MLA_HEADS, seq, MLA_HEAD_PAD), lambda b, i: (0, b, 0)),
            pl.BlockSpec((MLA_HEADS * V_EXT, seq), lambda b, i: (0, b)),
        ],
        out_specs=pl.BlockSpec((MLA_WIDTH, MLA_TQ), lambda b, i: (0, b * nq + i)),
        out_shape=jax.ShapeDtypeStruct((MLA_WIDTH, n_tok), BF16),
        scratch_shapes=[pltpu.VMEM((seq, MLA_TQ), BF16), pltpu.VMEM((seq, MLA_TQ), BF16)],
        compiler_params=pltpu.CompilerParams(
            dimension_semantics=("arbitrary", "arbitrary"), vmem_limit_bytes=VMEM_LIMIT),
        name="mla",
    )(qT, k, vT)


def _layer_norm(v, g, b):
    mu = jnp.mean(v, axis=-1, keepdims=True)
    c = v - mu
    var = jnp.mean(c * c, axis=-1, keepdims=True)
    return c * lax.rsqrt(var + LN_EPS) * g + b


def _merge_kernel(x_ref, oa_ref, ob_ref, g_ref, wa_ref, wb_ref, wo_ref, bg_ref, lg_ref, lb_ref,
                  o_ref, *, alpha):
    y_a = _dot_tn(oa_ref[...], wa_ref[...])
    y_b = _dot_tn(ob_ref[...], wb_ref[...])
    g = g_ref[...].astype(F32) + bg_ref[...]
    merged = jax.nn.sigmoid(g[:, :D_MODEL]) * y_a + jax.nn.sigmoid(g[:, D_MODEL:]) * y_b
    mix = _dot(merged.astype(BF16), wo_ref[...])
    o_ref[...] = _layer_norm(alpha * x_ref[...] + mix, lg_ref[...], lb_ref[...])


def _merge_call(x2, oTa, oTb, gates, w, alpha):
    n_tok = x2.shape[0]
    tm = MERGE_TM
    tok = lambda i: (i, 0)
    tokT = lambda i: (0, i)
    return pl.pallas_call(
        functools.partial(_merge_kernel, alpha=alpha),
        grid=(n_tok // tm,),
        in_specs=[
            pl.BlockSpec((tm, D_MODEL), tok),
            pl.BlockSpec((NA_WIDTH, tm), tokT),
            pl.BlockSpec((MLA_WIDTH, tm), tokT),
            pl.BlockSpec((tm, 2 * D_MODEL), tok),
            _resident(w["wa"].shape), _resident(w["wb"].shape), _resident(w["wo"].shape),
            _resident(w["bg"].shape), _resident(w["ln1g"].shape), _resident(w["ln1b"].shape),
        ],
        out_specs=pl.BlockSpec((tm, D_MODEL), tok),
        out_shape=jax.ShapeDtypeStruct((n_tok, D_MODEL), F32),
        compiler_params=pltpu.CompilerParams(
            dimension_semantics=("arbitrary",), vmem_limit_bytes=VMEM_LIMIT),
        name="merge",
    )(x2, oTa, oTb, gates, w["wa"], w["wb"], w["wo"], w["bg"], w["ln1g"], w["ln1b"])


def _ffn_kernel(x_ref, win_ref, wdown_ref, lg_ref, lb_ref, o_ref, *, alpha):
    x = x_ref[...]
    xb = x.astype(BF16)
    acc = alpha * x
    for c in range(FFN_HIDDEN // FFN_TF):
        cs = slice(c * FFN_TF, (c + 1) * FFN_TF)
        gate = _dot(xb, win_ref[:, cs])
        up = _dot(xb, win_ref[:, FFN_HIDDEN + c * FFN_TF:FFN_HIDDEN + (c + 1) * FFN_TF])
        act = (gate * jax.nn.sigmoid(gate) * up).astype(BF16)
        acc = acc + _dot(act, wdown_ref[cs, :])
    o_ref[...] = _layer_norm(acc, lg_ref[...], lb_ref[...])


def _ffn_call(x1, w, alpha):
    n_tok = x1.shape[0]
    tm = FFN_TM
    tok = lambda i: (i, 0)
    return pl.pallas_call(
        functools.partial(_ffn_kernel, alpha=alpha),
        grid=(n_tok // tm,),
        in_specs=[
            pl.BlockSpec((tm, D_MODEL), tok),
            _resident(w["wffn_in"].shape), _resident(w["wdown"].shape),
            _resident(w["ln2g"].shape), _resident(w["ln2b"].shape),
        ],
        out_specs=pl.BlockSpec((tm, D_MODEL), tok),
        out_shape=jax.ShapeDtypeStruct((n_tok, D_MODEL), F32),
        compiler_params=pltpu.CompilerParams(
            dimension_semantics=("arbitrary",), vmem_limit_bytes=VMEM_LIMIT),
        name="ffn",
    )(x1, w["wffn_in"], w["wdown"], w["ln2g"], w["ln2b"])


def _rope_swap(cols):
    q = MLA_ROPE // 4
    r1, r2, c1, c2 = cols[..., :q], cols[..., q:2 * q], cols[..., 2 * q:3 * q], cols[..., 3 * q:]
    return jnp.concatenate([-r2, r1, -c2, c1], axis=-1)


def _rope_tables(seq, q_scale):
    half = MLA_ROPE // 2
    quarter = MLA_ROPE // 4
    n_rows = seq // GRID_W
    inv_freq = ROPE_THETA ** (-jnp.arange(0, half, 2, dtype=F32) / half)
    freq = jnp.tile(inv_freq, LANES // quarter)
    coord = jnp.arange(max(n_rows, GRID_W), dtype=F32)
    ang = coord[:, None] * freq[None, :]
    cos_c, sin_c = jnp.cos(ang), jnp.sin(ang)
    d = np.arange(LANES)
    by_row = (d < half)[None, None, :]
    live = (d < MLA_ROPE)[None, None, :]

    def natural(tab):
        per_tok = jnp.where(by_row, tab[:n_rows, None, :], tab[None, :GRID_W, :])
        return jnp.where(live, per_tok, 0.0).reshape(seq, LANES)

    r = d - MLA_NOPE
    by_row_t = ((r >= 0) & (r < half))[:, None, None]
    rot = ((r >= 0) & (r < MLA_ROPE))[:, None, None]

    def transposed(tab, nope_value):
        tab_t = tab.T * q_scale
        per_tok = jnp.where(by_row_t, tab_t[:, :n_rows, None], tab_t[:, None, :GRID_W])
        full = jnp.where((d < MLA_NOPE)[:, None, None], nope_value, jnp.where(rot, per_tok, 0.0))
        return full.reshape(LANES, seq).astype(F32)

    return natural(cos_c), natural(sin_c), transposed(cos_c, q_scale), transposed(sin_c, 0.0)


def _prep_layer(seq, w_in, b_gate, na_rpb, q_norm, w_uq, kv_norm, w_ukv, w_bna, w_bmla, w_out,
                ln1_g, ln1_b, w_ffn_in, w_ffn_out, ln2_g, ln2_b):
    o_q, o_k, o_v = 0, NA_WIDTH, 2 * NA_WIDTH
    o_ql = 3 * NA_WIDTH
    o_kvl = o_ql + MLA_Q_LORA
    o_kr = o_kvl + MLA_KV_LORA
    o_g = o_kr + MLA_ROPE
    w = {}
    na_scale = NA_HEAD_DIM ** -0.5 * LOG2_E
    w_t = w_in.T
    w["wqT"] = (w_t[o_q:o_k] * na_scale).astype(BF16)
    w["wkT"] = w_t[o_k:o_v].astype(BF16)
    w["wvT"] = w_t[o_v:o_ql].astype(BF16)
    k_rope_t = w_t[o_kr:o_g]
    zrows = jnp.zeros((LANES - MLA_ROPE, D_MODEL), F32)
    w["wlatT"] = jnp.concatenate(
        [w_t[o_ql:o_kr], k_rope_t, zrows, _rope_swap(k_rope_t.T).T, zrows], axis=0).astype(BF16)
    w["wgT"] = w_t[o_g:].astype(BF16)
    w["qg"] = q_norm.reshape(1, MLA_Q_LORA)
    w["kvg"] = kv_norm.reshape(1, MLA_KV_LORA)

    uq = w_uq.reshape(MLA_Q_LORA, MLA_HEADS, MLA_QK)
    uq_pe = uq[:, :, MLA_NOPE:]
    uq_arr = jnp.concatenate([uq[:, :, :MLA_NOPE], uq_pe, _rope_swap(uq_pe)], axis=-1)
    w["wuqT"] = uq_arr.reshape(MLA_Q_LORA, MLA_HEADS * MLA_HEAD_PAD).T.astype(BF16)
    ukv = w_ukv.reshape(MLA_KV_LORA, MLA_HEADS, MLA_NOPE + MLA_V)
    uk_arr = jnp.concatenate(
        [ukv[:, :, :MLA_NOPE], jnp.zeros((MLA_KV_LORA, MLA_HEADS, MLA_HEAD_PAD - MLA_NOPE), F32)], axis=-1)
    w["wuk"] = uk_arr.reshape(MLA_KV_LORA, MLA_HEADS * MLA_HEAD_PAD).astype(BF16)
    w["wuvT"] = ukv[:, :, MLA_NOPE:].reshape(MLA_KV_LORA, MLA_WIDTH).T.astype(BF16)
    e = np.zeros((LANES, MLA_HEADS * MLA_HEAD_PAD), np.float32)
    for h in range(MLA_HEADS):
        e[np.arange(MLA_ROPE), h * MLA_HEAD_PAD + MLA_NOPE + np.arange(MLA_ROPE)] = 1.0
    w["e"] = jnp.asarray(e, BF16)

    q_scale = MLA_QK ** -0.5 * LOG2_E
    w["cn"], w["sn"], w["ctf"], w["stf"] = _rope_tables(seq, q_scale)

    w["na_bias"] = _na_bias_blocks(na_rpb)

    w["wa"] = w_bna.astype(BF16)
    w["wb"] = w_bmla.astype(BF16)
    w["wo"] = w_out.astype(BF16)
    w["bg"] = b_gate.reshape(1, 2 * D_MODEL)
    w["ln1g"] = ln1_g.reshape(1, D_MODEL)
    w["ln1b"] = ln1_b.reshape(1, D_MODEL)
    w["wffn_in"] = w_ffn_in.astype(BF16)
    w["wdown"] = w_ffn_out.astype(BF16)
    w["ln2g"] = ln2_g.reshape(1, D_MODEL)
    w["ln2b"] = ln2_b.reshape(1, D_MODEL)
    return w


def kernel(x, w_in, b_gate, na_rpb, mla_q_norm, mla_w_uq, mla_kv_norm, mla_w_ukv, w_branch_na, w_branch_mla, w_out, ln1_g, ln1_b, w_ffn_in, w_ffn_out, ln2_g, ln2_b):
    batch, seq, d = x.shape
    depth = w_in.shape[0]
    alpha = (2.0 * depth) ** 0.25
    n_tok = batch * seq
    x2 = x.reshape(n_tok, d)
    for l in range(depth):
        w = _prep_layer(seq, w_in[l], b_gate[l], na_rpb[l], mla_q_norm[l], mla_w_uq[l], mla_kv_norm[l],
                        mla_w_ukv[l], w_branch_na[l], w_branch_mla[l], w_out[l], ln1_g[l], ln1_b[l],
                        w_ffn_in[l], w_ffn_out[l], ln2_g[l], ln2_b[l])
        qT_na, k_na, vT_na, qT_mla, k_mla, vT_mla, gates = _proj_call(x2, w, n_tok, seq)
        oT_na = _na_call(qT_na, k_na, vT_na, w["na_bias"], batch, seq)
        oT_mla = _mla_call(qT_mla, k_mla, vT_mla, batch, seq)
        x1 = _merge_call(x2, oT_na, oT_mla, gates, w, alpha)
        x2 = _ffn_call(x1, w, alpha)
    return x2.reshape(batch, seq, d)
```

```python
import functools

import numpy as np
import jax
import jax.numpy as jnp
from jax import lax
from jax.experimental import pallas as pl
from jax.experimental.pallas import tpu as pltpu

F32 = jnp.float32
BF16 = jnp.bfloat16

D_MODEL = 1024
GRID_W = 64
NA_HEADS = 8
NA_HEAD_DIM = 64
NA_WIDTH = NA_HEADS * NA_HEAD_DIM
NA_WIN_ROWS = 8
NA_WIN_COLS = 16
MLA_HEADS = 8
MLA_Q_LORA = 256
MLA_KV_LORA = 128
MLA_NOPE = 64
MLA_ROPE = 32
MLA_QK = MLA_NOPE + MLA_ROPE
MLA_V = 64
MLA_WIDTH = MLA_HEADS * MLA_V
ROPE_THETA = 10000.0
FFN_HIDDEN = 2816
LN_EPS = 1e-5
RMS_EPS = 1e-6

LANES = 128
MLA_HEAD_PAD = LANES
VMEM_LIMIT = 56 * 1024 * 1024

PROJ_TM = 512
NA_Q_ROWS = 4
NA_KEY_ROWS = 12
NA_BLOCK_ROWS = 16
MLA_TQ = 512
MLA_KC = 512
HEAD_V = 64
SUM_ROWS = 16
V_EXT = HEAD_V + SUM_ROWS
LOG2_E = 1.4426950408889634
MERGE_TM = 1024
FFN_TM = 1024
SUB_TM = 512
FFN_TF = 256

NEG_BIG = -1e30

NT = (((1,), (1,)), ((), ()))
TN = (((0,), (0,)), ((), ()))


def _dot(a, b):
    return jnp.dot(a, b, preferred_element_type=F32)


def _dot_nt(a, b):
    return lax.dot_general(a, b, NT, preferred_element_type=F32)


def _dot_tn(a, b):
    return lax.dot_general(a, b, TN, preferred_element_type=F32)


def _resident(shape):
    nd = len(shape)
    return pl.BlockSpec(shape, lambda *_: (0,) * nd, pipeline_mode=pl.Buffered(1))


def _rms(v, g):
    return v * lax.rsqrt(jnp.mean(v * v, axis=-1, keepdims=True) + RMS_EPS) * g


def _store_v_ext(ref, vT):
    ones = jnp.ones((SUM_ROWS, vT.shape[1]), BF16)
    for h in range(vT.shape[0] // HEAD_V):
        ref[h * V_EXT:h * V_EXT + HEAD_V, :] = vT[h * HEAD_V:(h + 1) * HEAD_V].astype(BF16)
        ref[h * V_EXT + HEAD_V:(h + 1) * V_EXT, :] = ones


def _proj_kernel(x_ref, wqT_ref, wkT_ref, wvT_ref, wlatT_ref, wgT_ref, qg_ref, kvg_ref,
                 wuqT_ref, wuk_ref, e_ref, wuvT_ref, cn_ref, sn_ref, ctf_ref, stf_ref,
                 qT_na_ref, k_na_ref, vT_na_ref, qT_mla_ref, k_mla_ref, vT_mla_ref, g_ref):
    xb = x_ref[...].astype(BF16)
    qT_na_ref[...] = _dot_nt(wqT_ref[...], xb).astype(BF16)
    k_na_ref[...] = _dot_nt(xb, wkT_ref[...]).astype(BF16)
    _store_v_ext(vT_na_ref, _dot_nt(wvT_ref[...], xb))
    g_ref[...] = _dot_nt(xb, wgT_ref[...]).astype(BF16)

    lat = _dot_nt(xb, wlatT_ref[...])
    qn = _rms(lat[:, :MLA_Q_LORA], qg_ref[...]).astype(BF16)
    kvn = _rms(lat[:, MLA_Q_LORA:MLA_Q_LORA + MLA_KV_LORA], kvg_ref[...]).astype(BF16)
    kpe = lat[:, 384:512] * cn_ref[...] + lat[:, 512:640] * sn_ref[...]
    k_mla = _dot(kvn, wuk_ref[...]) + _dot(kpe.astype(BF16), e_ref[...])
    for h in range(MLA_HEADS):
        k_mla_ref[h] = k_mla[:, h * MLA_HEAD_PAD:(h + 1) * MLA_HEAD_PAD].astype(BF16)
    _store_v_ext(vT_mla_ref, _dot_nt(wuvT_ref[...], kvn))

    qT = _dot_nt(wuqT_ref[...], qn)
    ctf = ctf_ref[...]
    stf = stf_ref[...]
    for h in range(MLA_HEADS):
        qh = qT[h * MLA_HEAD_PAD:(h + 1) * MLA_HEAD_PAD]
        sw = jnp.concatenate([qh[MLA_ROPE:], qh[:MLA_ROPE]], axis=0)
        qT_mla_ref[h * MLA_HEAD_PAD:(h + 1) * MLA_HEAD_PAD, :] = (qh * ctf + sw * stf).astype(BF16)


def _proj_call(x2, w, n_tok, seq):
    tm = PROJ_TM
    steps_per_seq = seq // tm
    tok = lambda i: (i, 0)
    tokT = lambda i: (0, i)
    tab = lambda i: (i % steps_per_seq, 0)
    tabT = lambda i: (0, i % steps_per_seq)
    in_specs = [
        pl.BlockSpec((tm, D_MODEL), tok),
        _resident(w["wqT"].shape), _resident(w["wkT"].shape), _resident(w["wvT"].shape),
        _resident(w["wlatT"].shape), _resident(w["wgT"].shape),
        _resident(w["qg"].shape), _resident(w["kvg"].shape),
        _resident(w["wuqT"].shape), _resident(w["wuk"].shape), _resident(w["e"].shape),
        _resident(w["wuvT"].shape),
        pl.BlockSpec((tm, LANES), tab), pl.BlockSpec((tm, LANES), tab),
        pl.BlockSpec((MLA_HEAD_PAD, tm), tabT), pl.BlockSpec((MLA_HEAD_PAD, tm), tabT),
    ]
    out_shape = [
        jax.ShapeDtypeStruct((NA_WIDTH, n_tok), BF16),
        jax.ShapeDtypeStruct((n_tok, NA_WIDTH), BF16),
        jax.ShapeDtypeStruct((NA_HEADS * V_EXT, n_tok), BF16),
        jax.ShapeDtypeStruct((MLA_HEADS * MLA_HEAD_PAD, n_tok), BF16),
        jax.ShapeDtypeStruct((MLA_HEADS, n_tok, MLA_HEAD_PAD), BF16),
        jax.ShapeDtypeStruct((MLA_HEADS * V_EXT, n_tok), BF16),
        jax.ShapeDtypeStruct((n_tok, 2 * D_MODEL), BF16),
    ]
    out_specs = [
        pl.BlockSpec((NA_WIDTH, tm), tokT),
        pl.BlockSpec((tm, NA_WIDTH), tok),
        pl.BlockSpec((NA_HEADS * V_EXT, tm), tokT),
        pl.BlockSpec((MLA_HEADS * MLA_HEAD_PAD, tm), tokT),
        pl.BlockSpec((MLA_HEADS, tm, MLA_HEAD_PAD), lambda i: (0, i, 0)),
        pl.BlockSpec((MLA_HEADS * V_EXT, tm), tokT),
        pl.BlockSpec((tm, 2 * D_MODEL), tok),
    ]
    return pl.pallas_call(
        _proj_kernel,
        grid=(n_tok // tm,),
        in_specs=in_specs,
        out_specs=out_specs,
        out_shape=out_shape,
        compiler_params=pltpu.CompilerParams(
            dimension_semantics=("arbitrary",), vmem_limit_bytes=VMEM_LIMIT),
        name="proj",
    )(x2, w["wqT"], w["wkT"], w["wvT"], w["wlatT"], w["wgT"], w["qg"], w["kvg"],
      w["wuqT"], w["wuk"], w["e"], w["wuvT"], w["cn"], w["sn"], w["ctf"], w["stf"])


NA_GROUP_TYPES = 3


def _na_group_geometry(rows):
    return [(0, 0), (NA_Q_ROWS, 0), (rows - NA_Q_ROWS, rows - NA_KEY_ROWS)]


NA_MASKED_BLOCK = 2 * NA_WIN_ROWS - 1


def _na_block_index(rows, kr, qi):
    out = []
    for r0, ws in _na_group_geometry(rows):
        q_row, k_row = r0 + qi, ws + kr
        r_start = min(max(q_row - NA_WIN_ROWS // 2, 0), rows - NA_WIN_ROWS)
        valid = r_start <= k_row < r_start + NA_WIN_ROWS
        out.append(k_row - q_row + NA_WIN_ROWS - 1 if valid else NA_MASKED_BLOCK)
    return out


def _na_bias_blocks(rpb):
    n_dr = 2 * NA_WIN_ROWS - 1
    n_dc = 2 * NA_WIN_COLS - 1
    w = GRID_W
    lead = w - NA_WIN_COLS
    v = jnp.pad(rpb, ((0, 0), (0, 0), (lead, 2 * w - lead - n_dc)))
    a = jnp.tile(v, (1, 1, w))[:, :, :w * (2 * w - 1)].reshape(NA_HEADS, n_dr, w, 2 * w - 1)
    toe = jnp.swapaxes(a[:, :, :, w - 1:], 2, 3)
    col = np.arange(w)
    col_start = np.clip(col - NA_WIN_COLS // 2, 0, w - NA_WIN_COLS)
    v_col = (col[:, None] >= col_start[None, :]) & (col[:, None] < col_start[None, :] + NA_WIN_COLS)
    toe = jnp.where(v_col[None, None], toe * LOG2_E, NEG_BIG)
    toe = jnp.concatenate([toe, jnp.full((NA_HEADS, 1, w, w), NEG_BIG, F32)], axis=1)
    return jnp.concatenate([toe, toe], axis=-1)


def _na_kernel(qT_ref, k_ref, vT_ref, bias_ref, oT_ref, *, rows):
    blk = pl.program_id(1)
    nk = NA_KEY_ROWS * GRID_W
    nq = NA_Q_ROWS * GRID_W
    row_id = lax.broadcasted_iota(jnp.int32, (2 * NA_HEAD_DIM, nq), 0)
    left_half = lax.broadcasted_iota(jnp.int32, (GRID_W, LANES), 1) < GRID_W
    items = [(g, h) for g in range(NA_BLOCK_ROWS // NA_Q_ROWS) for h in range(NA_HEADS)]

    def window(g):
        r0 = blk * NA_BLOCK_ROWS + g * NA_Q_ROWS
        ws = jnp.clip(r0 - NA_WIN_ROWS // 2, 0, rows - NA_KEY_ROWS)
        typ = jnp.where(r0 == 0, 0, jnp.where(r0 == rows - NA_Q_ROWS, 2, 1))
        return pl.multiple_of(ws * GRID_W, 2 * LANES), typ

    def bias_tile(h, typ):
        def block(kr, qi):
            first, interior, last = _na_block_index(rows, kr, qi)
            idx = jnp.where(typ == 0, first, jnp.where(typ == 1, interior, last))
            return bias_ref[h, idx]
        key_rows = []
        for kr in range(NA_KEY_ROWS):
            pairs = [jnp.where(left_half, block(kr, qi), block(kr, qi + 1)) for qi in range(0, NA_Q_ROWS, 2)]
            key_rows.append(jnp.concatenate(pairs, axis=1))
        return jnp.concatenate(key_rows, axis=0)

    def scores(g, h):
        tok0, typ = window(g)
        hp, sub = divmod(h, 2)
        lanes = slice(hp * LANES, (hp + 1) * LANES)
        k_pair = k_ref[pl.ds(tok0, nk), lanes]
        q_pair = qT_ref[lanes, g * nq:(g + 1) * nq]
        keep = (row_id >= sub * NA_HEAD_DIM) & (row_id < (sub + 1) * NA_HEAD_DIM)
        q_h = jnp.where(keep, q_pair, jnp.zeros_like(q_pair))
        s = (_dot(k_pair, q_h) + bias_tile(h, typ)).astype(BF16)
        return s, jnp.max(s, axis=0, keepdims=True)

    def finish(g, h, s, m):
        tok0, _ = window(g)
        hv = slice(h * NA_HEAD_DIM, (h + 1) * NA_HEAD_DIM)
        p = jnp.exp2(s - m)
        acc = _dot(vT_ref[h * V_EXT:(h + 1) * V_EXT, pl.ds(tok0, nk)], p)
        o = acc[:NA_HEAD_DIM] * (1.0 / acc[NA_HEAD_DIM:NA_HEAD_DIM + 1])
        oT_ref[hv, g * nq:(g + 1) * nq] = o.astype(BF16)

    cur = scores(*items[0])
    for i, item in enumerate(items):
        nxt = scores(*items[i + 1]) if i + 1 < len(items) else None
        finish(*item, *cur)
        cur = nxt


def _na_call(qT, k, vT, bias, batch, seq):
    rows = seq // GRID_W
    blocks = rows // NA_BLOCK_ROWS
    tq = NA_BLOCK_ROWS * GRID_W
    n_tok = batch * seq
    return pl.pallas_call(
        functools.partial(_na_kernel, rows=rows),
        grid=(batch, blocks),
        in_specs=[
            pl.BlockSpec((NA_WIDTH, tq), lambda b, i: (0, b * blocks + i)),
            pl.BlockSpec((seq, NA_WIDTH), lambda b, i: (b, 0)),
            pl.BlockSpec((NA_HEADS * V_EXT, seq), lambda b, i: (0, b)),
            _resident(bias.shape),
        ],
        out_specs=pl.BlockSpec((NA_WIDTH, tq), lambda b, i: (0, b * blocks + i)),
        out_shape=jax.ShapeDtypeStruct((NA_WIDTH, n_tok), BF16),
        compiler_params=pltpu.CompilerParams(
            dimension_semantics=("arbitrary", "arbitrary"), vmem_limit_bytes=VMEM_LIMIT),
        name="na",
    )(qT, k, vT, bias)


def _mla_kernel(qT_ref, k_ref, vT_ref, oT_ref, s_even_ref, s_odd_ref, *, seq):
    n_chunks = seq // MLA_KC
    tq = oT_ref.shape[1]
    s_refs = (s_even_ref, s_odd_ref)

    def score_chunk(h, c, slot, m):
        hk = slice(h * MLA_HEAD_PAD, (h + 1) * MLA_HEAD_PAD)
        s = _dot(k_ref[h, c * MLA_KC:(c + 1) * MLA_KC, :], qT_ref[hk, :]).astype(BF16)
        s_refs[slot][c * MLA_KC:(c + 1) * MLA_KC, :] = s
        return jnp.maximum(m, jnp.max(s, axis=0, keepdims=True))

    neg = jnp.full((1, tq), -jnp.inf, BF16)
    m_next = neg
    for c in range(n_chunks):
        m_next = score_chunk(0, c, 0, m_next)
    for h in range(MLA_HEADS):
        slot = h % 2
        m_cur, m_next = m_next, neg
        hv = slice(h * MLA_V, (h + 1) * MLA_V)
        acc = jnp.zeros((MLA_V + SUM_ROWS, tq), F32)
        for c in range(n_chunks):
            if h + 1 < MLA_HEADS:
                m_next = score_chunk(h + 1, c, 1 - slot, m_next)
            cs = slice(c * MLA_KC, (c + 1) * MLA_KC)
            p = jnp.exp2(s_refs[slot][cs, :] - m_cur)
            acc = acc + _dot(vT_ref[h * V_EXT:(h + 1) * V_EXT, cs], p)
        oT_ref[hv, :] = (acc[:MLA_V] * (1.0 / acc[MLA_V:MLA_V + 1])).astype(BF16)


def _mla_call(qT, k, vT, batch, seq):
    nq = seq // MLA_TQ
    n_tok = batch * seq
    return pl.pallas_call(
        functools.partial(_mla_kernel, seq=seq),
        grid=(batch, nq),
        in_specs=[
            pl.BlockSpec((MLA_HEADS * MLA_HEAD_PAD, MLA_TQ), lambda b, i: (0, b * nq + i)),
            pl.BlockSpec((MLA_HEADS, seq, MLA_HEAD_PAD), lambda b, i: (0, b, 0)),
            pl.BlockSpec((MLA_HEADS * V_EXT, seq), lambda b, i: (0, b)),
        ],
        out_specs=pl.BlockSpec((MLA_WIDTH, MLA_TQ), lambda b, i: (0, b * nq + i)),
        out_shape=jax.ShapeDtypeStruct((MLA_WIDTH, n_tok), BF16),
        scratch_shapes=[pltpu.VMEM((seq, MLA_TQ), BF16), pltpu.VMEM((seq, MLA_TQ), BF16)],
        compiler_params=pltpu.CompilerParams(
            dimension_semantics=("arbitrary", "arbitrary"), vmem_limit_bytes=VMEM_LIMIT),
        name="mla",
    )(qT, k, vT)


def _layer_norm(v, g, b):
    mu = jnp.mean(v, axis=-1, keepdims=True)
    c = v - mu
    var = jnp.mean(c * c, axis=-1, keepdims=True)
    return c * lax.rsqrt(var + LN_EPS) * g + b


def _merge_kernel(x_ref, oa_ref, ob_ref, g_ref, wa_ref, wb_ref, wo_ref, bg_ref, lg_ref, lb_ref,
                  o_ref, *, alpha):
    for r0 in range(0, o_ref.shape[0], SUB_TM):
        rows = slice(r0, r0 + SUB_TM)
        y_a = _dot_tn(oa_ref[:, rows], wa_ref[...])
        y_b = _dot_tn(ob_ref[:, rows], wb_ref[...])
        g = g_ref[rows, :].astype(F32) + bg_ref[...]
        merged = jax.nn.sigmoid(g[:, :D_MODEL]) * y_a + jax.nn.sigmoid(g[:, D_MODEL:]) * y_b
        mix = _dot(merged.astype(BF16), wo_ref[...])
        o_ref[rows, :] = _layer_norm(alpha * x_ref[rows, :] + mix, lg_ref[...], lb_ref[...])


def _merge_call(x2, oTa, oTb, gates, w, alpha):
    n_tok = x2.shape[0]
    tm = MERGE_TM
    tok = lambda i: (i, 0)
    tokT = lambda i: (0, i)
    return pl.pallas_call(
        functools.partial(_merge_kernel, alpha=alpha),
        grid=(n_tok // tm,),
        in_specs=[
            pl.BlockSpec((tm, D_MODEL), tok),
            pl.BlockSpec((NA_WIDTH, tm), tokT),
            pl.BlockSpec((MLA_WIDTH, tm), tokT),
            pl.BlockSpec((tm, 2 * D_MODEL), tok),
            _resident(w["wa"].shape), _resident(w["wb"].shape), _resident(w["wo"].shape),
            _resident(w["bg"].shape), _resident(w["ln1g"].shape), _resident(w["ln1b"].shape),
        ],
        out_specs=pl.BlockSpec((tm, D_MODEL), tok),
        out_shape=jax.ShapeDtypeStruct((n_tok, D_MODEL), F32),
        compiler_params=pltpu.CompilerParams(
            dimension_semantics=("arbitrary",), vmem_limit_bytes=VMEM_LIMIT),
        name="merge",
    )(x2, oTa, oTb, gates, w["wa"], w["wb"], w["wo"], w["bg"], w["ln1g"], w["ln1b"])


def _ffn_kernel(x_ref, win_ref, wdown_ref, lg_ref, lb_ref, o_ref, *, alpha):
    for r0 in range(0, o_ref.shape[0], SUB_TM):
        rows = slice(r0, r0 + SUB_TM)
        x = x_ref[rows, :]
        xb = x.astype(BF16)
        acc = alpha * x
        for c in range(FFN_HIDDEN // FFN_TF):
            cs = slice(c * FFN_TF, (c + 1) * FFN_TF)
            gate = _dot(xb, win_ref[:, cs])
            up = _dot(xb, win_ref[:, FFN_HIDDEN + c * FFN_TF:FFN_HIDDEN + (c + 1) * FFN_TF])
            act = (gate * jax.nn.sigmoid(gate) * up).astype(BF16)
            acc = acc + _dot(act, wdown_ref[cs, :])
        o_ref[rows, :] = _layer_norm(acc, lg_ref[...], lb_ref[...])


def _ffn_call(x1, w, alpha):
    n_tok = x1.shape[0]
    tm = FFN_TM
    tok = lambda i: (i, 0)
    return pl.pallas_call(
        functools.partial(_ffn_kernel, alpha=alpha),
        grid=(n_tok // tm,),
        in_specs=[
            pl.BlockSpec((tm, D_MODEL), tok),
            _resident(w["wffn_in"].shape), _resident(w["wdown"].shape),
            _resident(w["ln2g"].shape), _resident(w["ln2b"].shape),
        ],
        out_specs=pl.BlockSpec((tm, D_MODEL), tok),
        out_shape=jax.ShapeDtypeStruct((n_tok, D_MODEL), F32),
        compiler_params=pltpu.CompilerParams(
            dimension_semantics=("arbitrary",), vmem_limit_bytes=VMEM_LIMIT),
        name="ffn",
    )(x1, w["wffn_in"], w["wdown"], w["ln2g"], w["ln2b"])


def _rope_swap(cols):
    q = MLA_ROPE // 4
    r1, r2, c1, c2 = cols[..., :q], cols[..., q:2 * q], cols[..., 2 * q:3 * q], cols[..., 3 * q:]
    return jnp.concatenate([-r2, r1, -c2, c1], axis=-1)


def _rope_tables(seq, q_scale):
    half = MLA_ROPE // 2
    quarter = MLA_ROPE // 4
    n_rows = seq // GRID_W
    inv_freq = ROPE_THETA ** (-jnp.arange(0, half, 2, dtype=F32) / half)
    freq = jnp.tile(inv_freq, LANES // quarter)
    coord = jnp.arange(max(n_rows, GRID_W), dtype=F32)
    ang = coord[:, None] * freq[None, :]
    cos_c, sin_c = jnp.cos(ang), jnp.sin(ang)
    d = np.arange(LANES)
    by_row = (d < half)[None, None, :]
    live = (d < MLA_ROPE)[None, None, :]

    def natural(tab):
        per_tok = jnp.where(by_row, tab[:n_rows, None, :], tab[None, :GRID_W, :])
        return jnp.where(live, per_tok, 0.0).reshape(seq, LANES)

    r = d - MLA_NOPE
    by_row_t = ((r >= 0) & (r < half))[:, None, None]
    rot = ((r >= 0) & (r < MLA_ROPE))[:, None, None]

    def transposed(tab, nope_value):
        tab_t = tab.T * q_scale
        per_tok = jnp.where(by_row_t, tab_t[:, :n_rows, None], tab_t[:, None, :GRID_W])
        full = jnp.where((d < MLA_NOPE)[:, None, None], nope_value, jnp.where(rot, per_tok, 0.0))
        return full.reshape(LANES, seq).astype(F32)

    return natural(cos_c), natural(sin_c), transposed(cos_c, q_scale), transposed(sin_c, 0.0)


def _prep_layer(seq, w_in, b_gate, na_rpb, q_norm, w_uq, kv_norm, w_ukv, w_bna, w_bmla, w_out,
                ln1_g, ln1_b, w_ffn_in, w_ffn_out, ln2_g, ln2_b):
    o_q, o_k, o_v = 0, NA_WIDTH, 2 * NA_WIDTH
    o_ql = 3 * NA_WIDTH
    o_kvl = o_ql + MLA_Q_LORA
    o_kr = o_kvl + MLA_KV_LORA
    o_g = o_kr + MLA_ROPE
    w = {}
    na_scale = NA_HEAD_DIM ** -0.5 * LOG2_E
    w_t = w_in.T
    w["wqT"] = (w_t[o_q:o_k] * na_scale).astype(BF16)
    w["wkT"] = w_t[o_k:o_v].astype(BF16)
    w["wvT"] = w_t[o_v:o_ql].astype(BF16)
    k_rope_t = w_t[o_kr:o_g]
    zrows = jnp.zeros((LANES - MLA_ROPE, D_MODEL), F32)
    w["wlatT"] = jnp.concatenate(
        [w_t[o_ql:o_kr], k_rope_t, zrows, _rope_swap(k_rope_t.T).T, zrows], axis=0).astype(BF16)
    w["wgT"] = w_t[o_g:].astype(BF16)
    w["qg"] = q_norm.reshape(1, MLA_Q_LORA)
    w["kvg"] = kv_norm.reshape(1, MLA_KV_LORA)

    uq = w_uq.reshape(MLA_Q_LORA, MLA_HEADS, MLA_QK)
    uq_pe = uq[:, :, MLA_NOPE:]
    uq_arr = jnp.concatenate([uq[:, :, :MLA_NOPE], uq_pe, _rope_swap(uq_pe)], axis=-1)
    w["wuqT"] = uq_arr.reshape(MLA_Q_LORA, MLA_HEADS * MLA_HEAD_PAD).T.astype(BF16)
    ukv = w_ukv.reshape(MLA_KV_LORA, MLA_HEADS, MLA_NOPE + MLA_V)
    uk_arr = jnp.concatenate(
        [ukv[:, :, :MLA_NOPE], jnp.zeros((MLA_KV_LORA, MLA_HEADS, MLA_HEAD_PAD - MLA_NOPE), F32)], axis=-1)
    w["wuk"] = uk_arr.reshape(MLA_KV_LORA, MLA_HEADS * MLA_HEAD_PAD).astype(BF16)
    w["wuvT"] = ukv[:, :, MLA_NOPE:].reshape(MLA_KV_LORA, MLA_WIDTH).T.astype(BF16)
    e = np.zeros((LANES, MLA_HEADS * MLA_HEAD_PAD), np.float32)
    for h in range(MLA_HEADS):
        e[np.arange(MLA_ROPE), h * MLA_HEAD_PAD + MLA_NOPE + np.arange(MLA_ROPE)] = 1.0
    w["e"] = jnp.asarray(e, BF16)

    q_scale = MLA_QK ** -0.5 * LOG2_E
    w["cn"], w["sn"], w["ctf"], w["stf"] = _rope_tables(seq, q_scale)

    w["na_bias"] = _na_bias_blocks(na_rpb)

    w["wa"] = w_bna.astype(BF16)
    w["wb"] = w_bmla.astype(BF16)
    w["wo"] = w_out.astype(BF16)
    w["bg"] = b_gate.reshape(1, 2 * D_MODEL)
    w["ln1g"] = ln1_g.reshape(1, D_MODEL)
    w["ln1b"] = ln1_b.reshape(1, D_MODEL)
    w["wffn_in"] = w_ffn_in.astype(BF16)
    w["wdown"] = w_ffn_out.astype(BF16)
    w["ln2g"] = ln2_g.reshape(1, D_MODEL)
    w["ln2b"] = ln2_b.reshape(1, D_MODEL)
    return w


def kernel(x, w_in, b_gate, na_rpb, mla_q_norm, mla_w_uq, mla_kv_norm, mla_w_ukv, w_branch_na, w_branch_mla, w_out, ln1_g, ln1_b, w_ffn_in, w_ffn_out, ln2_g, ln2_b):
    batch, seq, d = x.shape
    depth = w_in.shape[0]
    alpha = (2.0 * depth) ** 0.25
    n_tok = batch * seq
    x2 = x.reshape(n_tok, d)
    for l in range(depth):
        w = _prep_layer(seq, w_in[l], b_gate[l], na_rpb[l], mla_q_norm[l], mla_w_uq[l], mla_kv_norm[l],
                        mla_w_ukv[l], w_branch_na[l], w_branch_mla[l], w_out[l], ln1_g[l], ln1_b[l],
                        w_ffn_in[l], w_ffn_out[l], ln2_g[l], ln2_b[l])
        qT_na, k_na, vT_na, qT_mla, k_mla, vT_mla, gates = _proj_call(x2, w, n_tok, seq)
        oT_na = _na_call(qT_na, k_na, vT_na, w["na_bias"], batch, seq)
        oT_mla = _mla_call(qT_mla, k_mla, vT_mla, batch, seq)
        x1 = _merge_call(x2, oT_na, oT_mla, gates, w, alpha)
        x2 = _ffn_call(x1, w, alpha)
    return x2.reshape(batch, seq, d)
```

```python
import functools

import numpy as np
import jax
import jax.numpy as jnp
from jax import lax
from jax.experimental import pallas as pl
from jax.experimental.pallas import tpu as pltpu

F32 = jnp.float32
BF16 = jnp.bfloat16

D_MODEL = 1024
GRID_W = 64
NA_HEADS = 8
NA_HEAD_DIM = 64
NA_WIDTH = NA_HEADS * NA_HEAD_DIM
NA_WIN_ROWS = 8
NA_WIN_COLS = 16
MLA_HEADS = 8
MLA_Q_LORA = 256
MLA_KV_LORA = 128
MLA_NOPE = 64
MLA_ROPE = 32
MLA_QK = MLA_NOPE + MLA_ROPE
MLA_V = 64
MLA_WIDTH = MLA_HEADS * MLA_V
ROPE_THETA = 10000.0
FFN_HIDDEN = 2816
LN_EPS = 1e-5
RMS_EPS = 1e-6

LANES = 128
MLA_HEAD_PAD = LANES
VMEM_LIMIT = 56 * 1024 * 1024

PROJ_TM = 512
NA_Q_ROWS = 4
NA_KEY_ROWS = 12
NA_BLOCK_ROWS = 16
NA_LOOKAHEAD = 4
MLA_TQ = 512
MLA_KC = 512
HEAD_V = 64
SUM_ROWS = 16
V_EXT = HEAD_V + SUM_ROWS
LOG2_E = 1.4426950408889634
MERGE_TM = 1024
FFN_TM = 1024
FFN_TF = 256

NEG_BIG = -1e30

NT = (((1,), (1,)), ((), ()))
TN = (((0,), (0,)), ((), ()))


def _dot(a, b):
    return jnp.dot(a, b, preferred_element_type=F32)


def _dot_nt(a, b):
    return lax.dot_general(a, b, NT, preferred_element_type=F32)


def _dot_tn(a, b):
    return lax.dot_general(a, b, TN, preferred_element_type=F32)


def _resident(shape):
    nd = len(shape)
    return pl.BlockSpec(shape, lambda *_: (0,) * nd, pipeline_mode=pl.Buffered(1))


def _rms(v, g):
    return v * lax.rsqrt(jnp.mean(v * v, axis=-1, keepdims=True) + RMS_EPS) * g


def _store_v_ext(ref, vT):
    ones = jnp.ones((SUM_ROWS, vT.shape[1]), BF16)
    for h in range(vT.shape[0] // HEAD_V):
        ref[h * V_EXT:h * V_EXT + HEAD_V, :] = vT[h * HEAD_V:(h + 1) * HEAD_V].astype(BF16)
        ref[h * V_EXT + HEAD_V:(h + 1) * V_EXT, :] = ones


def _proj_kernel(x_ref, wqT_ref, wkT_ref, wvT_ref, wlatT_ref, wgT_ref, qg_ref, kvg_ref,
                 wuqT_ref, wuk_ref, e_ref, wuvT_ref, cn_ref, sn_ref, ctf_ref, stf_ref,
                 qT_na_ref, k_na_ref, vT_na_ref, qT_mla_ref, k_mla_ref, vT_mla_ref, g_ref):
    xb = x_ref[...].astype(BF16)
    qT_na_ref[...] = _dot_nt(wqT_ref[...], xb).astype(BF16)
    k_na_ref[...] = _dot_nt(xb, wkT_ref[...]).astype(BF16)
    _store_v_ext(vT_na_ref, _dot_nt(wvT_ref[...], xb))
    g_ref[...] = _dot_nt(xb, wgT_ref[...]).astype(BF16)

    lat = _dot_nt(xb, wlatT_ref[...])
    qn = _rms(lat[:, :MLA_Q_LORA], qg_ref[...]).astype(BF16)
    kvn = _rms(lat[:, MLA_Q_LORA:MLA_Q_LORA + MLA_KV_LORA], kvg_ref[...]).astype(BF16)
    kpe = lat[:, 384:512] * cn_ref[...] + lat[:, 512:640] * sn_ref[...]
    k_mla = _dot(kvn, wuk_ref[...]) + _dot(kpe.astype(BF16), e_ref[...])
    for h in range(MLA_HEADS):
        k_mla_ref[h] = k_mla[:, h * MLA_HEAD_PAD:(h + 1) * MLA_HEAD_PAD].astype(BF16)
    _store_v_ext(vT_mla_ref, _dot_nt(wuvT_ref[...], kvn))

    qT = _dot_nt(wuqT_ref[...], qn)
    ctf = ctf_ref[...]
    stf = stf_ref[...]
    for h in range(MLA_HEADS):
        qh = qT[h * MLA_HEAD_PAD:(h + 1) * MLA_HEAD_PAD]
        sw = jnp.concatenate([qh[MLA_ROPE:], qh[:MLA_ROPE]], axis=0)
        qT_mla_ref[h * MLA_HEAD_PAD:(h + 1) * MLA_HEAD_PAD, :] = (qh * ctf + sw * stf).astype(BF16)


def _proj_call(x2, w, n_tok, seq):
    tm = PROJ_TM
    steps_per_seq = seq // tm
    tok = lambda i: (i, 0)
    tokT = lambda i: (0, i)
    tab = lambda i: (i % steps_per_seq, 0)
    tabT = lambda i: (0, i % steps_per_seq)
    in_specs = [
        pl.BlockSpec((tm, D_MODEL), tok),
        _resident(w["wqT"].shape), _resident(w["wkT"].shape), _resident(w["wvT"].shape),
        _resident(w["wlatT"].shape), _resident(w["wgT"].shape),
        _resident(w["qg"].shape), _resident(w["kvg"].shape),
        _resident(w["wuqT"].shape), _resident(w["wuk"].shape), _resident(w["e"].shape),
        _resident(w["wuvT"].shape),
        pl.BlockSpec((tm, LANES), tab), pl.BlockSpec((tm, LANES), tab),
        pl.BlockSpec((MLA_HEAD_PAD, tm), tabT), pl.BlockSpec((MLA_HEAD_PAD, tm), tabT),
    ]
    out_shape = [
        jax.ShapeDtypeStruct((NA_WIDTH, n_tok), BF16),
        jax.ShapeDtypeStruct((n_tok, NA_WIDTH), BF16),
        jax.ShapeDtypeStruct((NA_HEADS * V_EXT, n_tok), BF16),
        jax.ShapeDtypeStruct((MLA_HEADS * MLA_HEAD_PAD, n_tok), BF16),
        jax.ShapeDtypeStruct((MLA_HEADS, n_tok, MLA_HEAD_PAD), BF16),
        jax.ShapeDtypeStruct((MLA_HEADS * V_EXT, n_tok), BF16),
        jax.ShapeDtypeStruct((n_tok, 2 * D_MODEL), BF16),
    ]
    out_specs = [
        pl.BlockSpec((NA_WIDTH, tm), tokT),
        pl.BlockSpec((tm, NA_WIDTH), tok),
        pl.BlockSpec((NA_HEADS * V_EXT, tm), tokT),
        pl.BlockSpec((MLA_HEADS * MLA_HEAD_PAD, tm), tokT),
        pl.BlockSpec((MLA_HEADS, tm, MLA_HEAD_PAD), lambda i: (0, i, 0)),
        pl.BlockSpec((MLA_HEADS * V_EXT, tm), tokT),
        pl.BlockSpec((tm, 2 * D_MODEL), tok),
    ]
    return pl.pallas_call(
        _proj_kernel,
        grid=(n_tok // tm,),
        in_specs=in_specs,
        out_specs=out_specs,
        out_shape=out_shape,
        compiler_params=pltpu.CompilerParams(
            dimension_semantics=("arbitrary",), vmem_limit_bytes=VMEM_LIMIT),
        name="proj",
    )(x2, w["wqT"], w["wkT"], w["wvT"], w["wlatT"], w["wgT"], w["qg"], w["kvg"],
      w["wuqT"], w["wuk"], w["e"], w["wuvT"], w["cn"], w["sn"], w["ctf"], w["stf"])


NA_GROUP_TYPES = 3


def _na_group_geometry(rows):
    return [(0, 0), (NA_Q_ROWS, 0), (rows - NA_Q_ROWS, rows - NA_KEY_ROWS)]


NA_MASKED_BLOCK = 2 * NA_WIN_ROWS - 1


def _na_block_index(rows, kr, qi):
    out = []
    for r0, ws in _na_group_geometry(rows):
        q_row, k_row = r0 + qi, ws + kr
        r_start = min(max(q_row - NA_WIN_ROWS // 2, 0), rows - NA_WIN_ROWS)
        valid = r_start <= k_row < r_start + NA_WIN_ROWS
        out.append(k_row - q_row + NA_WIN_ROWS - 1 if valid else NA_MASKED_BLOCK)
    return out


def _na_bias_blocks(rpb):
    n_dr = 2 * NA_WIN_ROWS - 1
    n_dc = 2 * NA_WIN_COLS - 1
    w = GRID_W
    lead = w - NA_WIN_COLS
    v = jnp.pad(rpb, ((0, 0), (0, 0), (lead, 2 * w - lead - n_dc)))
    a = jnp.tile(v, (1, 1, w))[:, :, :w * (2 * w - 1)].reshape(NA_HEADS, n_dr, w, 2 * w - 1)
    toe = jnp.swapaxes(a[:, :, :, w - 1:], 2, 3)
    col = np.arange(w)
    col_start = np.clip(col - NA_WIN_COLS // 2, 0, w - NA_WIN_COLS)
    v_col = (col[:, None] >= col_start[None, :]) & (col[:, None] < col_start[None, :] + NA_WIN_COLS)
    toe = jnp.where(v_col[None, None], toe * LOG2_E, NEG_BIG)
    toe = jnp.concatenate([toe, jnp.full((NA_HEADS, 1, w, w), NEG_BIG, F32)], axis=1)
    return jnp.concatenate([toe, toe], axis=-1)


def _na_kernel(qT_ref, k_ref, vT_ref, bias_ref, oT_ref, *, rows):
    blk = pl.program_id(1)
    nk = NA_KEY_ROWS * GRID_W
    nq = NA_Q_ROWS * GRID_W
    row_id = lax.broadcasted_iota(jnp.int32, (2 * NA_HEAD_DIM, nq), 0)
    left_half = lax.broadcasted_iota(jnp.int32, (GRID_W, LANES), 1) < GRID_W
    items = [(g, h) for g in range(NA_BLOCK_ROWS // NA_Q_ROWS) for h in range(NA_HEADS)]

    def window(g):
        r0 = blk * NA_BLOCK_ROWS + g * NA_Q_ROWS
        ws = jnp.clip(r0 - NA_WIN_ROWS // 2, 0, rows - NA_KEY_ROWS)
        typ = jnp.where(r0 == 0, 0, jnp.where(r0 == rows - NA_Q_ROWS, 2, 1))
        return pl.multiple_of(ws * GRID_W, 2 * LANES), typ

    def bias_tile(h, typ):
        def block(kr, qi):
            first, interior, last = _na_block_index(rows, kr, qi)
            idx = jnp.where(typ == 0, first, jnp.where(typ == 1, interior, last))
            return bias_ref[h, idx]
        key_rows = []
        for kr in range(NA_KEY_ROWS):
            pairs = [jnp.where(left_half, block(kr, qi), block(kr, qi + 1)) for qi in range(0, NA_Q_ROWS, 2)]
            key_rows.append(jnp.concatenate(pairs, axis=1))
        return jnp.concatenate(key_rows, axis=0)

    def scores(g, h):
        tok0, typ = window(g)
        hp, sub = divmod(h, 2)
        lanes = slice(hp * LANES, (hp + 1) * LANES)
        k_pair = k_ref[pl.ds(tok0, nk), lanes]
        q_pair = qT_ref[lanes, g * nq:(g + 1) * nq]
        keep = (row_id >= sub * NA_HEAD_DIM) & (row_id < (sub + 1) * NA_HEAD_DIM)
        q_h = jnp.where(keep, q_pair, jnp.zeros_like(q_pair))
        s = (_dot(k_pair, q_h) + bias_tile(h, typ)).astype(BF16)
        return s, jnp.max(s, axis=0, keepdims=True)

    def finish(g, h, s, m):
        tok0, _ = window(g)
        hv = slice(h * NA_HEAD_DIM, (h + 1) * NA_HEAD_DIM)
        p = jnp.exp2(s - m)
        acc = _dot(vT_ref[h * V_EXT:(h + 1) * V_EXT, pl.ds(tok0, nk)], p)
        o = acc[:NA_HEAD_DIM] * (1.0 / acc[NA_HEAD_DIM:NA_HEAD_DIM + 1])
        oT_ref[hv, g * nq:(g + 1) * nq] = o.astype(BF16)

    pending = [scores(*it) for it in items[:NA_LOOKAHEAD]]
    for i, item in enumerate(items):
        if i + NA_LOOKAHEAD < len(items):
            pending.append(scores(*items[i + NA_LOOKAHEAD]))
        finish(*item, *pending.pop(0))


def _na_call(qT, k, vT, bias, batch, seq):
    rows = seq // GRID_W
    blocks = rows // NA_BLOCK_ROWS
    tq = NA_BLOCK_ROWS * GRID_W
    n_tok = batch * seq
    return pl.pallas_call(
        functools.partial(_na_kernel, rows=rows),
        grid=(batch, blocks),
        in_specs=[
            pl.BlockSpec((NA_WIDTH, tq), lambda b, i: (0, b * blocks + i)),
            pl.BlockSpec((seq, NA_WIDTH), lambda b, i: (b, 0)),
            pl.BlockSpec((NA_HEADS * V_EXT, seq), lambda b, i: (0, b)),
            _resident(bias.shape),
        ],
        out_specs=pl.BlockSpec((NA_WIDTH, tq), lambda b, i: (0, b * blocks + i)),
        out_shape=jax.ShapeDtypeStruct((NA_WIDTH, n_tok), BF16),
        compiler_params=pltpu.CompilerParams(
            dimension_semantics=("arbitrary", "arbitrary"), vmem_limit_bytes=VMEM_LIMIT),
        name="na",
    )(qT, k, vT, bias)


def _mla_kernel(qT_ref, k_ref, vT_ref, oT_ref, s_even_ref, s_odd_ref, *, seq):
    n_chunks = seq // MLA_KC
    tq = oT_ref.shape[1]
    s_refs = (s_even_ref, s_odd_ref)

    def score_chunk(h, c, slot, m):
        hk = slice(h * MLA_HEAD_PAD, (h + 1) * MLA_HEAD_PAD)
        s = _dot(k_ref[h, c * MLA_KC:(c + 1) * MLA_KC, :], qT_ref[hk, :]).astype(BF16)
        s_refs[slot][c * MLA_KC:(c + 1) * MLA_KC, :] = s
        return jnp.maximum(m, jnp.max(s, axis=0, keepdims=True))

    neg = jnp.full((1, tq), -jnp.inf, BF16)
    m_next = neg
    for c in range(n_chunks):
        m_next = score_chunk(0, c, 0, m_next)
    for h in range(MLA_HEADS):
        slot = h % 2
        m_cur, m_next = m_next, neg
        hv = slice(h * MLA_V, (h + 1) * MLA_V)
        acc = jnp.zeros((MLA_V + SUM_ROWS, tq), F32)
        for c in range(n_chunks):
            if h + 1 < MLA_HEADS:
                m_next = score_chunk(h + 1, c, 1 - slot, m_next)
            cs = slice(c * MLA_KC, (c + 1) * MLA_KC)
            p = jnp.exp2(s_refs[slot][cs, :] - m_cur)
            acc = acc + _dot(vT_ref[h * V_EXT:(h + 1) * V_EXT, cs], p)
        oT_ref[hv, :] = (acc[:MLA_V] * (1.0 / acc[MLA_V:MLA_V + 1])).astype(BF16)


def _mla_call(qT, k, vT, batch, seq):
    nq = seq // MLA_TQ
    n_tok = batch * seq
    return pl.pallas_call(
        functools.partial(_mla_kernel, seq=seq),
        grid=(batch, nq),
        in_specs=[
            pl.BlockSpec((MLA_HEADS * MLA_HEAD_PAD, MLA_TQ), lambda b, i: (0, b * nq + i)),
            pl.BlockSpec((MLA_HEADS, seq, MLA_HEAD_PAD), lambda b, i: (0, b, 0)),
            pl.BlockSpec((MLA_HEADS * V_EXT, seq), lambda b, i: (0, b)),
        ],
        out_specs=pl.BlockSpec((MLA_WIDTH, MLA_TQ), lambda b, i: (0, b * nq + i)),
        out_shape=jax.ShapeDtypeStruct((MLA_WIDTH, n_tok), BF16),
        scratch_shapes=[pltpu.VMEM((seq, MLA_TQ), BF16), pltpu.VMEM((seq, MLA_TQ), BF16)],
        compiler_params=pltpu.CompilerParams(
            dimension_semantics=("arbitrary", "arbitrary"), vmem_limit_bytes=VMEM_LIMIT),
        name="mla",
    )(qT, k, vT)


def _layer_norm(v, g, b):
    mu = jnp.mean(v, axis=-1, keepdims=True)
    c = v - mu
    var = jnp.mean(c * c, axis=-1, keepdims=True)
    return c * lax.rsqrt(var + LN_EPS) * g + b


def _merge_kernel(x_ref, oa_ref, ob_ref, g_ref, wa_ref, wb_ref, wo_ref, bg_ref, lg_ref, lb_ref,
                  o_ref, *, alpha):
    y_a = _dot_tn(oa_ref[...], wa_ref[...])
    y_b = _dot_tn(ob_ref[...], wb_ref[...])
    g = g_ref[...].astype(F32) + bg_ref[...]
    merged = jax.nn.sigmoid(g[:, :D_MODEL]) * y_a + jax.nn.sigmoid(g[:, D_MODEL:]) * y_b
    mix = _dot(merged.astype(BF16), wo_ref[...])
    o_ref[...] = _layer_norm(alpha * x_ref[...] + mix, lg_ref[...], lb_ref[...])


def _merge_call(x2, oTa, oTb, gates, w, alpha):
    n_tok = x2.shape[0]
    tm = MERGE_TM
    tok = lambda i: (i, 0)
    tokT = lambda i: (0, i)
    return pl.pallas_call(
        functools.partial(_merge_kernel, alpha=alpha),
        grid=(n_tok // tm,),
        in_specs=[
            pl.BlockSpec((tm, D_MODEL), tok),
            pl.BlockSpec((NA_WIDTH, tm), tokT),
            pl.BlockSpec((MLA_WIDTH, tm), tokT),
            pl.BlockSpec((tm, 2 * D_MODEL), tok),
            _resident(w["wa"].shape), _resident(w["wb"].shape), _resident(w["wo"].shape),
            _resident(w["bg"].shape), _resident(w["ln1g"].shape), _resident(w["ln1b"].shape),
        ],
        out_specs=pl.BlockSpec((tm, D_MODEL), tok),
        out_shape=jax.ShapeDtypeStruct((n_tok, D_MODEL), F32),
        compiler_params=pltpu.CompilerParams(
            dimension_semantics=("arbitrary",), vmem_limit_bytes=VMEM_LIMIT),
        name="merge",
    )(x2, oTa, oTb, gates, w["wa"], w["wb"], w["wo"], w["bg"], w["ln1g"], w["ln1b"])


def _ffn_kernel(x_ref, win_ref, wdown_ref, lg_ref, lb_ref, o_ref, *, alpha):
    x = x_ref[...]
    xb = x.astype(BF16)
    acc = alpha * x
    for c in range(FFN_HIDDEN // FFN_TF):
        cs = slice(c * FFN_TF, (c + 1) * FFN_TF)
        gate = _dot(xb, win_ref[:, cs])
        up = _dot(xb, win_ref[:, FFN_HIDDEN + c * FFN_TF:FFN_HIDDEN + (c + 1) * FFN_TF])
        act = (gate * jax.nn.sigmoid(gate) * up).astype(BF16)
        acc = acc + _dot(act, wdown_ref[cs, :])
    o_ref[...] = _layer_norm(acc, lg_ref[...], lb_ref[...])


def _ffn_call(x1, w, alpha):
    n_tok = x1.shape[0]
    tm = FFN_TM
    tok = lambda i: (i, 0)
    return pl.pallas_call(
        functools.partial(_ffn_kernel, alpha=alpha),
        grid=(n_tok // tm,),
        in_specs=[
            pl.BlockSpec((tm, D_MODEL), tok),
            _resident(w["wffn_in"].shape), _resident(w["wdown"].shape),
            _resident(w["ln2g"].shape), _resident(w["ln2b"].shape),
        ],
        out_specs=pl.BlockSpec((tm, D_MODEL), tok),
        out_shape=jax.ShapeDtypeStruct((n_tok, D_MODEL), F32),
        compiler_params=pltpu.CompilerParams(
            dimension_semantics=("arbitrary",), vmem_limit_bytes=VMEM_LIMIT),
        name="ffn",
    )(x1, w["wffn_in"], w["wdown"], w["ln2g"], w["ln2b"])


def _rope_swap(cols):
    q = MLA_ROPE // 4
    r1, r2, c1, c2 = cols[..., :q], cols[..., q:2 * q], cols[..., 2 * q:3 * q], cols[..., 3 * q:]
    return jnp.concatenate([-r2, r1, -c2, c1], axis=-1)


def _rope_tables(seq, q_scale):
    half = MLA_ROPE // 2
    quarter = MLA_ROPE // 4
    n_rows = seq // GRID_W
    inv_freq = ROPE_THETA ** (-jnp.arange(0, half, 2, dtype=F32) / half)
    freq = jnp.tile(inv_freq, LANES // quarter)
    coord = jnp.arange(max(n_rows, GRID_W), dtype=F32)
    ang = coord[:, None] * freq[None, :]
    cos_c, sin_c = jnp.cos(ang), jnp.sin(ang)
    d = np.arange(LANES)
    by_row = (d < half)[None, None, :]
    live = (d < MLA_ROPE)[None, None, :]

    def natural(tab):
        per_tok = jnp.where(by_row, tab[:n_rows, None, :], tab[None, :GRID_W, :])
        return jnp.where(live, per_tok, 0.0).reshape(seq, LANES)

    r = d - MLA_NOPE
    by_row_t = ((r >= 0) & (r < half))[:, None, None]
    rot = ((r >= 0) & (r < MLA_ROPE))[:, None, None]

    def transposed(tab, nope_value):
        tab_t = tab.T * q_scale
        per_tok = jnp.where(by_row_t, tab_t[:, :n_rows, None], tab_t[:, None, :GRID_W])
        full = jnp.where((d < MLA_NOPE)[:, None, None], nope_value, jnp.where(rot, per_tok, 0.0))
        return full.reshape(LANES, seq).astype(F32)

    return natural(cos_c), natural(sin_c), transposed(cos_c, q_scale), transposed(sin_c, 0.0)


def _prep_layer(seq, w_in, b_gate, na_rpb, q_norm, w_uq, kv_norm, w_ukv, w_bna, w_bmla, w_out,
                ln1_g, ln1_b, w_ffn_in, w_ffn_out, ln2_g, ln2_b):
    o_q, o_k, o_v = 0, NA_WIDTH, 2 * NA_WIDTH
    o_ql = 3 * NA_WIDTH
    o_kvl = o_ql + MLA_Q_LORA
    o_kr = o_kvl + MLA_KV_LORA
    o_g = o_kr + MLA_ROPE
    w = {}
    na_scale = NA_HEAD_DIM ** -0.5 * LOG2_E
    w_t = w_in.T
    w["wqT"] = (w_t[o_q:o_k] * na_scale).astype(BF16)
    w["wkT"] = w_t[o_k:o_v].astype(BF16)
    w["wvT"] = w_t[o_v:o_ql].astype(BF16)
    k_rope_t = w_t[o_kr:o_g]
    zrows = jnp.zeros((LANES - MLA_ROPE, D_MODEL), F32)
    w["wlatT"] = jnp.concatenate(
        [w_t[o_ql:o_kr], k_rope_t, zrows, _rope_swap(k_rope_t.T).T, zrows], axis=0).astype(BF16)
    w["wgT"] = w_t[o_g:].astype(BF16)
    w["qg"] = q_norm.reshape(1, MLA_Q_LORA)
    w["kvg"] = kv_norm.reshape(1, MLA_KV_LORA)

    uq = w_uq.reshape(MLA_Q_LORA, MLA_HEADS, MLA_QK)
    uq_pe = uq[:, :, MLA_NOPE:]
    uq_arr = jnp.concatenate([uq[:, :, :MLA_NOPE], uq_pe, _rope_swap(uq_pe)], axis=-1)
    w["wuqT"] = uq_arr.reshape(MLA_Q_LORA, MLA_HEADS * MLA_HEAD_PAD).T.astype(BF16)
    ukv = w_ukv.reshape(MLA_KV_LORA, MLA_HEADS, MLA_NOPE + MLA_V)
    uk_arr = jnp.concatenate(
        [ukv[:, :, :MLA_NOPE], jnp.zeros((MLA_KV_LORA, MLA_HEADS, MLA_HEAD_PAD - MLA_NOPE), F32)], axis=-1)
    w["wuk"] = uk_arr.reshape(MLA_KV_LORA, MLA_HEADS * MLA_HEAD_PAD).astype(BF16)
    w["wuvT"] = ukv[:, :, MLA_NOPE:].reshape(MLA_KV_LORA, MLA_WIDTH).T.astype(BF16)
    e = np.zeros((LANES, MLA_HEADS * MLA_HEAD_PAD), np.float32)
    for h in range(MLA_HEADS):
        e[np.arange(MLA_ROPE), h * MLA_HEAD_PAD + MLA_NOPE + np.arange(MLA_ROPE)] = 1.0
    w["e"] = jnp.asarray(e, BF16)

    q_scale = MLA_QK ** -0.5 * LOG2_E
    w["cn"], w["sn"], w["ctf"], w["stf"] = _rope_tables(seq, q_scale)

    w["na_bias"] = _na_bias_blocks(na_rpb)

    w["wa"] = w_bna.astype(BF16)
    w["wb"] = w_bmla.astype(BF16)
    w["wo"] = w_out.astype(BF16)
    w["bg"] = b_gate.reshape(1, 2 * D_MODEL)
    w["ln1g"] = ln1_g.reshape(1, D_MODEL)
    w["ln1b"] = ln1_b.reshape(1, D_MODEL)
    w["wffn_in"] = w_ffn_in.astype(BF16)
    w["wdown"] = w_ffn_out.astype(BF16)
    w["ln2g"] = ln2_g.reshape(1, D_MODEL)
    w["ln2b"] = ln2_b.reshape(1, D_MODEL)
    return w


def kernel(x, w_in, b_gate, na_rpb, mla_q_norm, mla_w_uq, mla_kv_norm, mla_w_ukv, w_branch_na, w_branch_mla, w_out, ln1_g, ln1_b, w_ffn_in, w_ffn_out, ln2_g, ln2_b):
    batch, seq, d = x.shape
    depth = w_in.shape[0]
    alpha = (2.0 * depth) ** 0.25
    n_tok = batch * seq
    x2 = x.reshape(n_tok, d)
    for l in range(depth):
        w = _prep_layer(seq, w_in[l], b_gate[l], na_rpb[l], mla_q_norm[l], mla_w_uq[l], mla_kv_norm[l],
                        mla_w_ukv[l], w_branch_na[l], w_branch_mla[l], w_out[l], ln1_g[l], ln1_b[l],
                        w_ffn_in[l], w_ffn_out[l], ln2_g[l], ln2_b[l])
        qT_na, k_na, vT_na, qT_mla, k_mla, vT_mla, gates = _proj_call(x2, w, n_tok, seq)
        oT_na = _na_call(qT_na, k_na, vT_na, w["na_bias"], batch, seq)
        oT_mla = _mla_call(qT_mla, k_mla, vT_mla, batch, seq)
        x1 = _merge_call(x2, oT_na, oT_mla, gates, w, alpha)
        x2 = _ffn_call(x1, w, alpha)
    return x2.reshape(batch, seq, d)
```

```python
import functools

import numpy as np
import jax
import jax.numpy as jnp
from jax import lax
from jax.experimental import pallas as pl
from jax.experimental.pallas import tpu as pltpu

F32 = jnp.float32
BF16 = jnp.bfloat16

D_MODEL = 1024
GRID_W = 64
NA_HEADS = 8
NA_HEAD_DIM = 64
NA_WIDTH = NA_HEADS * NA_HEAD_DIM
NA_WIN_ROWS = 8
NA_WIN_COLS = 16
MLA_HEADS = 8
MLA_Q_LORA = 256
MLA_KV_LORA = 128
MLA_NOPE = 64
MLA_ROPE = 32
MLA_QK = MLA_NOPE + MLA_ROPE
MLA_V = 64
MLA_WIDTH = MLA_HEADS * MLA_V
ROPE_THETA = 10000.0
FFN_HIDDEN = 2816
LN_EPS = 1e-5
RMS_EPS = 1e-6

LANES = 128
MLA_HEAD_PAD = LANES
VMEM_LIMIT = 56 * 1024 * 1024

PROJ_TM = 512
NA_Q_ROWS = 4
NA_KEY_ROWS = 12
NA_BLOCK_ROWS = 16
NA_LOOKAHEAD = 4
MLA_TQ = 512
MLA_KC = 512
HEAD_V = 64
SUM_ROWS = 16
V_EXT = HEAD_V + SUM_ROWS
LOG2_E = 1.4426950408889634
MERGE_TM = 1024
MERGE_SUB = 512
FFN_TM = 1024
FFN_SUB = 512
FFN_TF = 256

NEG_BIG = -1e30

NT = (((1,), (1,)), ((), ()))
TN = (((0,), (0,)), ((), ()))


def _dot(a, b):
    return jnp.dot(a, b, preferred_element_type=F32)


def _dot_nt(a, b):
    return lax.dot_general(a, b, NT, preferred_element_type=F32)


def _dot_tn(a, b):
    return lax.dot_general(a, b, TN, preferred_element_type=F32)


def _resident(shape):
    nd = len(shape)
    return pl.BlockSpec(shape, lambda *_: (0,) * nd, pipeline_mode=pl.Buffered(1))


def _rms(v, g):
    return v * lax.rsqrt(jnp.mean(v * v, axis=-1, keepdims=True) + RMS_EPS) * g


def _store_v_ext(ref, vT):
    ones = jnp.ones((SUM_ROWS, vT.shape[1]), BF16)
    for h in range(vT.shape[0] // HEAD_V):
        ref[h * V_EXT:h * V_EXT + HEAD_V, :] = vT[h * HEAD_V:(h + 1) * HEAD_V].astype(BF16)
        ref[h * V_EXT + HEAD_V:(h + 1) * V_EXT, :] = ones


def _proj_kernel(x_ref, wqT_ref, wkT_ref, wvT_ref, wlatT_ref, wgT_ref, qg_ref, kvg_ref,
                 wuqT_ref, wuk_ref, e_ref, wuvT_ref, cn_ref, sn_ref, ctf_ref, stf_ref,
                 qT_na_ref, k_na_ref, vT_na_ref, qT_mla_ref, k_mla_ref, vT_mla_ref, g_ref):
    xb = x_ref[...].astype(BF16)
    lat = _dot_nt(xb, wlatT_ref[...])
    qT_na_ref[...] = _dot_nt(wqT_ref[...], xb).astype(BF16)
    qn = _rms(lat[:, :MLA_Q_LORA], qg_ref[...]).astype(BF16)
    kvn = _rms(lat[:, MLA_Q_LORA:MLA_Q_LORA + MLA_KV_LORA], kvg_ref[...]).astype(BF16)
    kpe = lat[:, 384:512] * cn_ref[...] + lat[:, 512:640] * sn_ref[...]
    k_na_ref[...] = _dot_nt(xb, wkT_ref[...]).astype(BF16)

    qT = _dot_nt(wuqT_ref[...], qn)
    _store_v_ext(vT_na_ref, _dot_nt(wvT_ref[...], xb))
    ctf = ctf_ref[...]
    stf = stf_ref[...]
    for h in range(MLA_HEADS):
        qh = qT[h * MLA_HEAD_PAD:(h + 1) * MLA_HEAD_PAD]
        sw = jnp.concatenate([qh[MLA_ROPE:], qh[:MLA_ROPE]], axis=0)
        qT_mla_ref[h * MLA_HEAD_PAD:(h + 1) * MLA_HEAD_PAD, :] = (qh * ctf + sw * stf).astype(BF16)

    k_mla = _dot(kvn, wuk_ref[...]) + _dot(kpe.astype(BF16), e_ref[...])
    for h in range(MLA_HEADS):
        k_mla_ref[h] = k_mla[:, h * MLA_HEAD_PAD:(h + 1) * MLA_HEAD_PAD].astype(BF16)
    _store_v_ext(vT_mla_ref, _dot_nt(wuvT_ref[...], kvn))
    g_ref[...] = _dot_nt(xb, wgT_ref[...]).astype(BF16)


def _proj_call(x2, w, n_tok, seq):
    tm = PROJ_TM
    steps_per_seq = seq // tm
    tok = lambda i: (i, 0)
    tokT = lambda i: (0, i)
    tab = lambda i: (i % steps_per_seq, 0)
    tabT = lambda i: (0, i % steps_per_seq)
    in_specs = [
        pl.BlockSpec((tm, D_MODEL), tok),
        _resident(w["wqT"].shape), _resident(w["wkT"].shape), _resident(w["wvT"].shape),
        _resident(w["wlatT"].shape), _resident(w["wgT"].shape),
        _resident(w["qg"].shape), _resident(w["kvg"].shape),
        _resident(w["wuqT"].shape), _resident(w["wuk"].shape), _resident(w["e"].shape),
        _resident(w["wuvT"].shape),
        pl.BlockSpec((tm, LANES), tab), pl.BlockSpec((tm, LANES), tab),
        pl.BlockSpec((MLA_HEAD_PAD, tm), tabT), pl.BlockSpec((MLA_HEAD_PAD, tm), tabT),
    ]
    out_shape = [
        jax.ShapeDtypeStruct((NA_WIDTH, n_tok), BF16),
        jax.ShapeDtypeStruct((n_tok, NA_WIDTH), BF16),
        jax.ShapeDtypeStruct((NA_HEADS * V_EXT, n_tok), BF16),
        jax.ShapeDtypeStruct((MLA_HEADS * MLA_HEAD_PAD, n_tok), BF16),
        jax.ShapeDtypeStruct((MLA_HEADS, n_tok, MLA_HEAD_PAD), BF16),
        jax.ShapeDtypeStruct((MLA_HEADS * V_EXT, n_tok), BF16),
        jax.ShapeDtypeStruct((n_tok, 2 * D_MODEL), BF16),
    ]
    out_specs = [
        pl.BlockSpec((NA_WIDTH, tm), tokT),
        pl.BlockSpec((tm, NA_WIDTH), tok),
        pl.BlockSpec((NA_HEADS * V_EXT, tm), tokT),
        pl.BlockSpec((MLA_HEADS * MLA_HEAD_PAD, tm), tokT),
        pl.BlockSpec((MLA_HEADS, tm, MLA_HEAD_PAD), lambda i: (0, i, 0)),
        pl.BlockSpec((MLA_HEADS * V_EXT, tm), tokT),
        pl.BlockSpec((tm, 2 * D_MODEL), tok),
    ]
    return pl.pallas_call(
        _proj_kernel,
        grid=(n_tok // tm,),
        in_specs=in_specs,
        out_specs=out_specs,
        out_shape=out_shape,
        compiler_params=pltpu.CompilerParams(
            dimension_semantics=("arbitrary",), vmem_limit_bytes=VMEM_LIMIT),
        name="proj",
    )(x2, w["wqT"], w["wkT"], w["wvT"], w["wlatT"], w["wgT"], w["qg"], w["kvg"],
      w["wuqT"], w["wuk"], w["e"], w["wuvT"], w["cn"], w["sn"], w["ctf"], w["stf"])


NA_GROUP_TYPES = 3


def _na_group_geometry(rows):
    return [(0, 0), (NA_Q_ROWS, 0), (rows - NA_Q_ROWS, rows - NA_KEY_ROWS)]


NA_MASKED_BLOCK = 2 * NA_WIN_ROWS - 1


def _na_block_index(rows, kr, qi):
    out = []
    for r0, ws in _na_group_geometry(rows):
        q_row, k_row = r0 + qi, ws + kr
        r_start = min(max(q_row - NA_WIN_ROWS // 2, 0), rows - NA_WIN_ROWS)
        valid = r_start <= k_row < r_start + NA_WIN_ROWS
        out.append(k_row - q_row + NA_WIN_ROWS - 1 if valid else NA_MASKED_BLOCK)
    return out


def _na_bias_blocks(rpb):
    n_dr = 2 * NA_WIN_ROWS - 1
    n_dc = 2 * NA_WIN_COLS - 1
    w = GRID_W
    lead = w - NA_WIN_COLS
    v = jnp.pad(rpb, ((0, 0), (0, 0), (lead, 2 * w - lead - n_dc)))
    a = jnp.tile(v, (1, 1, w))[:, :, :w * (2 * w - 1)].reshape(NA_HEADS, n_dr, w, 2 * w - 1)
    toe = jnp.swapaxes(a[:, :, :, w - 1:], 2, 3)
    col = np.arange(w)
    col_start = np.clip(col - NA_WIN_COLS // 2, 0, w - NA_WIN_COLS)
    v_col = (col[:, None] >= col_start[None, :]) & (col[:, None] < col_start[None, :] + NA_WIN_COLS)
    toe = jnp.where(v_col[None, None], toe * LOG2_E, NEG_BIG)
    toe = jnp.concatenate([toe, jnp.full((NA_HEADS, 1, w, w), NEG_BIG, F32)], axis=1)
    return jnp.concatenate([toe, toe], axis=-1)


def _na_kernel(qT_ref, k_ref, vT_ref, bias_ref, oT_ref, *, rows):
    blk = pl.program_id(1)
    nk = NA_KEY_ROWS * GRID_W
    nq = NA_Q_ROWS * GRID_W
    row_id = lax.broadcasted_iota(jnp.int32, (2 * NA_HEAD_DIM, nq), 0)
    left_half = lax.broadcasted_iota(jnp.int32, (GRID_W, LANES), 1) < GRID_W
    items = [(g, h) for g in range(NA_BLOCK_ROWS // NA_Q_ROWS) for h in range(NA_HEADS)]

    def window(g):
        r0 = blk * NA_BLOCK_ROWS + g * NA_Q_ROWS
        ws = jnp.clip(r0 - NA_WIN_ROWS // 2, 0, rows - NA_KEY_ROWS)
        typ = jnp.where(r0 == 0, 0, jnp.where(r0 == rows - NA_Q_ROWS, 2, 1))
        return pl.multiple_of(ws * GRID_W, 2 * LANES), typ

    def bias_tile(h, typ):
        def block(kr, qi):
            first, interior, last = _na_block_index(rows, kr, qi)
            idx = jnp.where(typ == 0, first, jnp.where(typ == 1, interior, last))
            return bias_ref[h, idx]
        key_rows = []
        for kr in range(NA_KEY_ROWS):
            pairs = [jnp.where(left_half, block(kr, qi), block(kr, qi + 1)) for qi in range(0, NA_Q_ROWS, 2)]
            key_rows.append(jnp.concatenate(pairs, axis=1))
        return jnp.concatenate(key_rows, axis=0)

    def scores(g, h):
        tok0, typ = window(g)
        hp, sub = divmod(h, 2)
        lanes = slice(hp * LANES, (hp + 1) * LANES)
        k_pair = k_ref[pl.ds(tok0, nk), lanes]
        q_pair = qT_ref[lanes, g * nq:(g + 1) * nq]
        keep = (row_id >= sub * NA_HEAD_DIM) & (row_id < (sub + 1) * NA_HEAD_DIM)
        q_h = jnp.where(keep, q_pair, jnp.zeros_like(q_pair))
        s = (_dot(k_pair, q_h) + bias_tile(h, typ)).astype(BF16)
        return s, jnp.max(s, axis=0, keepdims=True)

    def finish(g, h, s, m):
        tok0, _ = window(g)
        hv = slice(h * NA_HEAD_DIM, (h + 1) * NA_HEAD_DIM)
        p = jnp.exp2(s - m)
        acc = _dot(vT_ref[h * V_EXT:(h + 1) * V_EXT, pl.ds(tok0, nk)], p)
        o = acc[:NA_HEAD_DIM] * (1.0 / acc[NA_HEAD_DIM:NA_HEAD_DIM + 1])
        oT_ref[hv, g * nq:(g + 1) * nq] = o.astype(BF16)

    pending = [scores(*it) for it in items[:NA_LOOKAHEAD]]
    for i, item in enumerate(items):
        if i + NA_LOOKAHEAD < len(items):
            pending.append(scores(*items[i + NA_LOOKAHEAD]))
        finish(*item, *pending.pop(0))


def _na_call(qT, k, vT, bias, batch, seq):
    rows = seq // GRID_W
    blocks = rows // NA_BLOCK_ROWS
    tq = NA_BLOCK_ROWS * GRID_W
    n_tok = batch * seq
    return pl.pallas_call(
        functools.partial(_na_kernel, rows=rows),
        grid=(batch, blocks),
        in_specs=[
            pl.BlockSpec((NA_WIDTH, tq), lambda b, i: (0, b * blocks + i)),
            pl.BlockSpec((seq, NA_WIDTH), lambda b, i: (b, 0)),
            pl.BlockSpec((NA_HEADS * V_EXT, seq), lambda b, i: (0, b)),
            _resident(bias.shape),
        ],
        out_specs=pl.BlockSpec((NA_WIDTH, tq), lambda b, i: (0, b * blocks + i)),
        out_shape=jax.ShapeDtypeStruct((NA_WIDTH, n_tok), BF16),
        compiler_params=pltpu.CompilerParams(
            dimension_semantics=("arbitrary", "arbitrary"), vmem_limit_bytes=VMEM_LIMIT),
        name="na",
    )(qT, k, vT, bias)


def _mla_kernel(qT_ref, k_ref, vT_ref, oT_ref, s_even_ref, s_odd_ref, *, seq):
    n_chunks = seq // MLA_KC
    tq = oT_ref.shape[1]
    s_refs = (s_even_ref, s_odd_ref)

    def score_chunk(h, c, slot, m):
        hk = slice(h * MLA_HEAD_PAD, (h + 1) * MLA_HEAD_PAD)
        s = _dot(k_ref[h, c * MLA_KC:(c + 1) * MLA_KC, :], qT_ref[hk, :]).astype(BF16)
        s_refs[slot][c * MLA_KC:(c + 1) * MLA_KC, :] = s
        return jnp.maximum(m, jnp.max(s, axis=0, keepdims=True))

    neg = jnp.full((1, tq), -jnp.inf, BF16)
    m_next = neg
    for c in range(n_chunks):
        m_next = score_chunk(0, c, 0, m_next)
    for h in range(MLA_HEADS):
        slot = h % 2
        m_cur, m_next = m_next, neg
        hv = slice(h * MLA_V, (h + 1) * MLA_V)
        acc = jnp.zeros((MLA_V + SUM_ROWS, tq), F32)
        for c in range(n_chunks):
            if h + 1 < MLA_HEADS:
                m_next = score_chunk(h + 1, c, 1 - slot, m_next)
            cs = slice(c * MLA_KC, (c + 1) * MLA_KC)
            p = jnp.exp2(s_refs[slot][cs, :] - m_cur)
            acc = acc + _dot(vT_ref[h * V_EXT:(h + 1) * V_EXT, cs], p)
        oT_ref[hv, :] = (acc[:MLA_V] * (1.0 / acc[MLA_V:MLA_V + 1])).astype(BF16)


def _mla_call(qT, k, vT, batch, seq):
    nq = seq // MLA_TQ
    n_tok = batch * seq
    return pl.pallas_call(
        functools.partial(_mla_kernel, seq=seq),
        grid=(batch, nq),
        in_specs=[
            pl.BlockSpec((MLA_HEADS * MLA_HEAD_PAD, MLA_TQ), lambda b, i: (0, b * nq + i)),
            pl.BlockSpec((MLA_HEADS, seq, MLA_HEAD_PAD), lambda b, i: (0, b, 0)),
            pl.BlockSpec((MLA_HEADS * V_EXT, seq), lambda b, i: (0, b)),
        ],
        out_specs=pl.BlockSpec((MLA_WIDTH, MLA_TQ), lambda b, i: (0, b * nq + i)),
        out_shape=jax.ShapeDtypeStruct((MLA_WIDTH, n_tok), BF16),
        scratch_shapes=[pltpu.VMEM((seq, MLA_TQ), BF16), pltpu.VMEM((seq, MLA_TQ), BF16)],
        compiler_params=pltpu.CompilerParams(
            dimension_semantics=("arbitrary", "arbitrary"), vmem_limit_bytes=VMEM_LIMIT),
        name="mla",
    )(qT, k, vT)


def _layer_norm(v, g, b):
    mu = jnp.mean(v, axis=-1, keepdims=True)
    c = v - mu
    var = jnp.mean(c * c, axis=-1, keepdims=True)
    return c * lax.rsqrt(var + LN_EPS) * g + b


def _merge_kernel(x_ref, oa_ref, ob_ref, g_ref, wa_ref, wb_ref, wo_ref, bg_ref, lg_ref, lb_ref,
                  o_ref, *, alpha):
    subs = [slice(r0, r0 + MERGE_SUB) for r0 in range(0, o_ref.shape[0], MERGE_SUB)]
    branch = [(_dot_tn(oa_ref[:, r], wa_ref[...]), _dot_tn(ob_ref[:, r], wb_ref[...])) for r in subs]
    mixes = []
    for r, (y_a, y_b) in zip(subs, branch):
        g = g_ref[r, :].astype(F32) + bg_ref[...]
        merged = jax.nn.sigmoid(g[:, :D_MODEL]) * y_a + jax.nn.sigmoid(g[:, D_MODEL:]) * y_b
        mixes.append(_dot(merged.astype(BF16), wo_ref[...]))
    for r, mix in zip(subs, mixes):
        o_ref[r, :] = _layer_norm(alpha * x_ref[r, :] + mix, lg_ref[...], lb_ref[...])


def _merge_call(x2, oTa, oTb, gates, w, alpha):
    n_tok = x2.shape[0]
    tm = MERGE_TM
    tok = lambda i: (i, 0)
    tokT = lambda i: (0, i)
    return pl.pallas_call(
        functools.partial(_merge_kernel, alpha=alpha),
        grid=(n_tok // tm,),
        in_specs=[
            pl.BlockSpec((tm, D_MODEL), tok),
            pl.BlockSpec((NA_WIDTH, tm), tokT),
            pl.BlockSpec((MLA_WIDTH, tm), tokT),
            pl.BlockSpec((tm, 2 * D_MODEL), tok),
            _resident(w["wa"].shape), _resident(w["wb"].shape), _resident(w["wo"].shape),
            _resident(w["bg"].shape), _resident(w["ln1g"].shape), _resident(w["ln1b"].shape),
        ],
        out_specs=pl.BlockSpec((tm, D_MODEL), tok),
        out_shape=jax.ShapeDtypeStruct((n_tok, D_MODEL), F32),
        compiler_params=pltpu.CompilerParams(
            dimension_semantics=("arbitrary",), vmem_limit_bytes=VMEM_LIMIT),
        name="merge",
    )(x2, oTa, oTb, gates, w["wa"], w["wb"], w["wo"], w["bg"], w["ln1g"], w["ln1b"])


def _ffn_kernel(x_ref, win_ref, wdown_ref, lg_ref, lb_ref, o_ref, *, alpha):
    subs = [slice(r0, r0 + FFN_SUB) for r0 in range(0, o_ref.shape[0], FFN_SUB)]
    xs = [x_ref[r, :] for r in subs]
    xbs = [x.astype(BF16) for x in xs]
    accs = [alpha * x for x in xs]
    for c in range(FFN_HIDDEN // FFN_TF):
        cs = slice(c * FFN_TF, (c + 1) * FFN_TF)
        for i, xb in enumerate(xbs):
            gate = _dot(xb, win_ref[:, cs])
            up = _dot(xb, win_ref[:, FFN_HIDDEN + c * FFN_TF:FFN_HIDDEN + (c + 1) * FFN_TF])
            act = (gate * jax.nn.sigmoid(gate) * up).astype(BF16)
            accs[i] = accs[i] + _dot(act, wdown_ref[cs, :])
    for r, acc in zip(subs, accs):
        o_ref[r, :] = _layer_norm(acc, lg_ref[...], lb_ref[...])


def _ffn_call(x1, w, alpha):
    n_tok = x1.shape[0]
    tm = FFN_TM
    tok = lambda i: (i, 0)
    return pl.pallas_call(
        functools.partial(_ffn_kernel, alpha=alpha),
        grid=(n_tok // tm,),
        in_specs=[
            pl.BlockSpec((tm, D_MODEL), tok),
            _resident(w["wffn_in"].shape), _resident(w["wdown"].shape),
            _resident(w["ln2g"].shape), _resident(w["ln2b"].shape),
        ],
        out_specs=pl.BlockSpec((tm, D_MODEL), tok),
        out_shape=jax.ShapeDtypeStruct((n_tok, D_MODEL), F32),
        compiler_params=pltpu.CompilerParams(
            dimension_semantics=("arbitrary",), vmem_limit_bytes=VMEM_LIMIT),
        name="ffn",
    )(x1, w["wffn_in"], w["wdown"], w["ln2g"], w["ln2b"])


def _rope_swap(cols):
    q = MLA_ROPE // 4
    r1, r2, c1, c2 = cols[..., :q], cols[..., q:2 * q], cols[..., 2 * q:3 * q], cols[..., 3 * q:]
    return jnp.concatenate([-r2, r1, -c2, c1], axis=-1)


def _rope_tables(seq, q_scale):
    half = MLA_ROPE // 2
    quarter = MLA_ROPE // 4
    n_rows = seq // GRID_W
    inv_freq = ROPE_THETA ** (-jnp.arange(0, half, 2, dtype=F32) / half)
    freq = jnp.tile(inv_freq, LANES // quarter)
    coord = jnp.arange(max(n_rows, GRID_W), dtype=F32)
    ang = coord[:, None] * freq[None, :]
    cos_c, sin_c = jnp.cos(ang), jnp.sin(ang)
    d = np.arange(LANES)
    by_row = (d < half)[None, None, :]
    live = (d < MLA_ROPE)[None, None, :]

    def natural(tab):
        per_tok = jnp.where(by_row, tab[:n_rows, None, :], tab[None, :GRID_W, :])
        return jnp.where(live, per_tok, 0.0).reshape(seq, LANES)

    r = d - MLA_NOPE
    by_row_t = ((r >= 0) & (r < half))[:, None, None]
    rot = ((r >= 0) & (r < MLA_ROPE))[:, None, None]

    def transposed(tab, nope_value):
        tab_t = tab.T * q_scale
        per_tok = jnp.where(by_row_t, tab_t[:, :n_rows, None], tab_t[:, None, :GRID_W])
        full = jnp.where((d < MLA_NOPE)[:, None, None], nope_value, jnp.where(rot, per_tok, 0.0))
        return full.reshape(LANES, seq).astype(F32)

    return natural(cos_c), natural(sin_c), transposed(cos_c, q_scale), transposed(sin_c, 0.0)


def _prep_layer(seq, w_in, b_gate, na_rpb, q_norm, w_uq, kv_norm, w_ukv, w_bna, w_bmla, w_out,
                ln1_g, ln1_b, w_ffn_in, w_ffn_out, ln2_g, ln2_b):
    o_q, o_k, o_v = 0, NA_WIDTH, 2 * NA_WIDTH
    o_ql = 3 * NA_WIDTH
    o_kvl = o_ql + MLA_Q_LORA
    o_kr = o_kvl + MLA_KV_LORA
    o_g = o_kr + MLA_ROPE
    w = {}
    na_scale = NA_HEAD_DIM ** -0.5 * LOG2_E
    w_t = w_in.T
    w["wqT"] = (w_t[o_q:o_k] * na_scale).astype(BF16)
    w["wkT"] = w_t[o_k:o_v].astype(BF16)
    w["wvT"] = w_t[o_v:o_ql].astype(BF16)
    k_rope_t = w_t[o_kr:o_g]
    zrows = jnp.zeros((LANES - MLA_ROPE, D_MODEL), F32)
    w["wlatT"] = jnp.concatenate(
        [w_t[o_ql:o_kr], k_rope_t, zrows, _rope_swap(k_rope_t.T).T, zrows], axis=0).astype(BF16)
    w["wgT"] = w_t[o_g:].astype(BF16)
    w["qg"] = q_norm.reshape(1, MLA_Q_LORA)
    w["kvg"] = kv_norm.reshape(1, MLA_KV_LORA)

    uq = w_uq.reshape(MLA_Q_LORA, MLA_HEADS, MLA_QK)
    uq_pe = uq[:, :, MLA_NOPE:]
    uq_arr = jnp.concatenate([uq[:, :, :MLA_NOPE], uq_pe, _rope_swap(uq_pe)], axis=-1)
    w["wuqT"] = uq_arr.reshape(MLA_Q_LORA, MLA_HEADS * MLA_HEAD_PAD).T.astype(BF16)
    ukv = w_ukv.reshape(MLA_KV_LORA, MLA_HEADS, MLA_NOPE + MLA_V)
    uk_arr = jnp.concatenate(
        [ukv[:, :, :MLA_NOPE], jnp.zeros((MLA_KV_LORA, MLA_HEADS, MLA_HEAD_PAD - MLA_NOPE), F32)], axis=-1)
    w["wuk"] = uk_arr.reshape(MLA_KV_LORA, MLA_HEADS * MLA_HEAD_PAD).astype(BF16)
    w["wuvT"] = ukv[:, :, MLA_NOPE:].reshape(MLA_KV_LORA, MLA_WIDTH).T.astype(BF16)
    e = np.zeros((LANES, MLA_HEADS * MLA_HEAD_PAD), np.float32)
    for h in range(MLA_HEADS):
        e[np.arange(MLA_ROPE), h * MLA_HEAD_PAD + MLA_NOPE + np.arange(MLA_ROPE)] = 1.0
    w["e"] = jnp.asarray(e, BF16)

    q_scale = MLA_QK ** -0.5 * LOG2_E
    w["cn"], w["sn"], w["ctf"], w["stf"] = _rope_tables(seq, q_scale)

    w["na_bias"] = _na_bias_blocks(na_rpb)

    w["wa"] = w_bna.astype(BF16)
    w["wb"] = w_bmla.astype(BF16)
    w["wo"] = w_out.astype(BF16)
    w["bg"] = b_gate.reshape(1, 2 * D_MODEL)
    w["ln1g"] = ln1_g.reshape(1, D_MODEL)
    w["ln1b"] = ln1_b.reshape(1, D_MODEL)
    w["wffn_in"] = w_ffn_in.astype(BF16)
    w["wdown"] = w_ffn_out.astype(BF16)
    w["ln2g"] = ln2_g.reshape(1, D_MODEL)
    w["ln2b"] = ln2_b.reshape(1, D_MODEL)
    return w


def kernel(x, w_in, b_gate, na_rpb, mla_q_norm, mla_w_uq, mla_kv_norm, mla_w_ukv, w_branch_na, w_branch_mla, w_out, ln1_g, ln1_b, w_ffn_in, w_ffn_out, ln2_g, ln2_b):
    batch, seq, d = x.shape
    depth = w_in.shape[0]
    alpha = (2.0 * depth) ** 0.25
    n_tok = batch * seq
    x2 = x.reshape(n_tok, d)
    for l in range(depth):
        w = _prep_layer(seq, w_in[l], b_gate[l], na_rpb[l], mla_q_norm[l], mla_w_uq[l], mla_kv_norm[l],
                        mla_w_ukv[l], w_branch_na[l], w_branch_mla[l], w_out[l], ln1_g[l], ln1_b[l],
                        w_ffn_in[l], w_ffn_out[l], ln2_g[l], ln2_b[l])
        qT_na, k_na, vT_na, qT_mla, k_mla, vT_mla, gates = _proj_call(x2, w, n_tok, seq)
        oT_na = _na_call(qT_na, k_na, vT_na, w["na_bias"], batch, seq)
        oT_mla = _mla_call(qT_mla, k_mla, vT_mla, batch, seq)
        x1 = _merge_call(x2, oT_na, oT_mla, gates, w, alpha)
        x2 = _ffn_call(x1, w, alpha)
    return x2.reshape(batch, seq, d)
```

```python
import functools

import numpy as np
import jax
import jax.numpy as jnp
from jax import lax
from jax.experimental import pallas as pl
from jax.experimental.pallas import tpu as pltpu

F32 = jnp.float32
BF16 = jnp.bfloat16

D_MODEL = 1024
GRID_W = 64
NA_HEADS = 8
NA_HEAD_DIM = 64
NA_WIDTH = NA_HEADS * NA_HEAD_DIM
NA_WIN_ROWS = 8
NA_WIN_COLS = 16
MLA_HEADS = 8
MLA_Q_LORA = 256
MLA_KV_LORA = 128
MLA_NOPE = 64
MLA_ROPE = 32
MLA_QK = MLA_NOPE + MLA_ROPE
MLA_V = 64
MLA_WIDTH = MLA_HEADS * MLA_V
ROPE_THETA = 10000.0
FFN_HIDDEN = 2816
LN_EPS = 1e-5
RMS_EPS = 1e-6

LANES = 128
MLA_HEAD_PAD = LANES
VMEM_LIMIT = 56 * 1024 * 1024

PROJ_TM = 512
NA_Q_ROWS = 4
NA_KEY_ROWS = 12
NA_BLOCK_ROWS = 16
NA_LOOKAHEAD = 4
MLA_TQ = 512
MLA_KC = 512
HEAD_V = 64
SUM_ROWS = 16
V_EXT = HEAD_V + SUM_ROWS
LOG2_E = 1.4426950408889634
MERGE_TM = 1024
MERGE_SUB = 256
FFN_TM = 1024
FFN_SUB = 512
FFN_TF = 256

NEG_BIG = -1e30

NT = (((1,), (1,)), ((), ()))
TN = (((0,), (0,)), ((), ()))


def _dot(a, b):
    return jnp.dot(a, b, preferred_element_type=F32)


def _dot_nt(a, b):
    return lax.dot_general(a, b, NT, preferred_element_type=F32)


def _dot_tn(a, b):
    return lax.dot_general(a, b, TN, preferred_element_type=F32)


def _resident(shape):
    nd = len(shape)
    return pl.BlockSpec(shape, lambda *_: (0,) * nd, pipeline_mode=pl.Buffered(1))


def _rms(v, g):
    return v * lax.rsqrt(jnp.mean(v * v, axis=-1, keepdims=True) + RMS_EPS) * g


def _store_v_ext(ref, vT):
    ones = jnp.ones((SUM_ROWS, vT.shape[1]), BF16)
    for h in range(vT.shape[0] // HEAD_V):
        ref[h * V_EXT:h * V_EXT + HEAD_V, :] = vT[h * HEAD_V:(h + 1) * HEAD_V].astype(BF16)
        ref[h * V_EXT + HEAD_V:(h + 1) * V_EXT, :] = ones


def _proj_kernel(x_ref, wqT_ref, wkT_ref, wvT_ref, wlatT_ref, wgT_ref, qg_ref, kvg_ref,
                 wuqT_ref, wuk_ref, e_ref, wuvT_ref, cn_ref, sn_ref, ctf_ref, stf_ref,
                 qT_na_ref, k_na_ref, vT_na_ref, qT_mla_ref, k_mla_ref, vT_mla_ref, g_ref):
    xb = x_ref[...].astype(BF16)
    lat = _dot_nt(xb, wlatT_ref[...])
    qT_na_ref[...] = _dot_nt(wqT_ref[...], xb).astype(BF16)
    qn = _rms(lat[:, :MLA_Q_LORA], qg_ref[...]).astype(BF16)
    kvn = _rms(lat[:, MLA_Q_LORA:MLA_Q_LORA + MLA_KV_LORA], kvg_ref[...]).astype(BF16)
    kpe = lat[:, 384:512] * cn_ref[...] + lat[:, 512:640] * sn_ref[...]
    k_na_ref[...] = _dot_nt(xb, wkT_ref[...]).astype(BF16)

    qT = _dot_nt(wuqT_ref[...], qn)
    _store_v_ext(vT_na_ref, _dot_nt(wvT_ref[...], xb))
    ctf = ctf_ref[...]
    stf = stf_ref[...]
    for h in range(MLA_HEADS):
        qh = qT[h * MLA_HEAD_PAD:(h + 1) * MLA_HEAD_PAD]
        sw = jnp.concatenate([qh[MLA_ROPE:], qh[:MLA_ROPE]], axis=0)
        qT_mla_ref[h * MLA_HEAD_PAD:(h + 1) * MLA_HEAD_PAD, :] = (qh * ctf + sw * stf).astype(BF16)

    k_mla = _dot(kvn, wuk_ref[...]) + _dot(kpe.astype(BF16), e_ref[...])
    for h in range(MLA_HEADS):
        k_mla_ref[h] = k_mla[:, h * MLA_HEAD_PAD:(h + 1) * MLA_HEAD_PAD].astype(BF16)
    _store_v_ext(vT_mla_ref, _dot_nt(wuvT_ref[...], kvn))
    g_ref[...] = _dot_nt(xb, wgT_ref[...]).astype(BF16)


def _proj_call(x2, w, n_tok, seq):
    tm = PROJ_TM
    steps_per_seq = seq // tm
    tok = lambda i: (i, 0)
    tokT = lambda i: (0, i)
    tab = lambda i: (i % steps_per_seq, 0)
    tabT = lambda i: (0, i % steps_per_seq)
    in_specs = [
        pl.BlockSpec((tm, D_MODEL), tok),
        _resident(w["wqT"].shape), _resident(w["wkT"].shape), _resident(w["wvT"].shape),
        _resident(w["wlatT"].shape), _resident(w["wgT"].shape),
        _resident(w["qg"].shape), _resident(w["kvg"].shape),
        _resident(w["wuqT"].shape), _resident(w["wuk"].shape), _resident(w["e"].shape),
        _resident(w["wuvT"].shape),
        pl.BlockSpec((tm, LANES), tab), pl.BlockSpec((tm, LANES), tab),
        pl.BlockSpec((MLA_HEAD_PAD, tm), tabT), pl.BlockSpec((MLA_HEAD_PAD, tm), tabT),
    ]
    out_shape = [
        jax.ShapeDtypeStruct((NA_WIDTH, n_tok), BF16),
        jax.ShapeDtypeStruct((n_tok, NA_WIDTH), BF16),
        jax.ShapeDtypeStruct((NA_HEADS * V_EXT, n_tok), BF16),
        jax.ShapeDtypeStruct((MLA_HEADS * MLA_HEAD_PAD, n_tok), BF16),
        jax.ShapeDtypeStruct((MLA_HEADS, n_tok, MLA_HEAD_PAD), BF16),
        jax.ShapeDtypeStruct((MLA_HEADS * V_EXT, n_tok), BF16),
        jax.ShapeDtypeStruct((n_tok, 2 * D_MODEL), BF16),
    ]
    out_specs = [
        pl.BlockSpec((NA_WIDTH, tm), tokT),
        pl.BlockSpec((tm, NA_WIDTH), tok),
        pl.BlockSpec((NA_HEADS * V_EXT, tm), tokT),
        pl.BlockSpec((MLA_HEADS * MLA_HEAD_PAD, tm), tokT),
        pl.BlockSpec((MLA_HEADS, tm, MLA_HEAD_PAD), lambda i: (0, i, 0)),
        pl.BlockSpec((MLA_HEADS * V_EXT, tm), tokT),
        pl.BlockSpec((tm, 2 * D_MODEL), tok),
    ]
    return pl.pallas_call(
        _proj_kernel,
        grid=(n_tok // tm,),
        in_specs=in_specs,
        out_specs=out_specs,
        out_shape=out_shape,
        compiler_params=pltpu.CompilerParams(
            dimension_semantics=("arbitrary",), vmem_limit_bytes=VMEM_LIMIT),
        name="proj",
    )(x2, w["wqT"], w["wkT"], w["wvT"], w["wlatT"], w["wgT"], w["qg"], w["kvg"],
      w["wuqT"], w["wuk"], w["e"], w["wuvT"], w["cn"], w["sn"], w["ctf"], w["stf"])


def _na_group_geometry(rows):
    return [(0, 0), (NA_Q_ROWS, 0), (rows - NA_Q_ROWS, rows - NA_KEY_ROWS)]


NA_MASKED_BLOCK = 2 * NA_WIN_ROWS - 1


def _na_block_index(rows, kr, qi):
    out = []
    for r0, ws in _na_group_geometry(rows):
        q_row, k_row = r0 + qi, ws + kr
        r_start = min(max(q_row - NA_WIN_ROWS // 2, 0), rows - NA_WIN_ROWS)
        valid = r_start <= k_row < r_start + NA_WIN_ROWS
        out.append(k_row - q_row + NA_WIN_ROWS - 1 if valid else NA_MASKED_BLOCK)
    return out


def _na_bias_blocks(rpb):
    n_dr = 2 * NA_WIN_ROWS - 1
    n_dc = 2 * NA_WIN_COLS - 1
    w = GRID_W
    lead = w - NA_WIN_COLS
    v = jnp.pad(rpb, ((0, 0), (0, 0), (lead, 2 * w - lead - n_dc)))
    a = jnp.tile(v, (1, 1, w))[:, :, :w * (2 * w - 1)].reshape(NA_HEADS, n_dr, w, 2 * w - 1)
    toe = jnp.swapaxes(a[:, :, :, w - 1:], 2, 3)
    col = np.arange(w)
    col_start = np.clip(col - NA_WIN_COLS // 2, 0, w - NA_WIN_COLS)
    v_col = (col[:, None] >= col_start[None, :]) & (col[:, None] < col_start[None, :] + NA_WIN_COLS)
    toe = jnp.where(v_col[None, None], toe * LOG2_E, NEG_BIG)
    toe = jnp.concatenate([toe, jnp.full((NA_HEADS, 1, w, w), NEG_BIG, F32)], axis=1)
    return jnp.concatenate([toe, toe], axis=-1)


def _na_kernel(qT_ref, k_ref, vT_ref, bias_ref, oT_ref, *, rows):
    blk = pl.program_id(1)
    nk = NA_KEY_ROWS * GRID_W
    nq = NA_Q_ROWS * GRID_W
    row_id = lax.broadcasted_iota(jnp.int32, (2 * NA_HEAD_DIM, nq), 0)
    left_half = lax.broadcasted_iota(jnp.int32, (GRID_W, LANES), 1) < GRID_W
    items = [(g, h) for g in range(NA_BLOCK_ROWS // NA_Q_ROWS) for h in range(NA_HEADS)]

    def window(g):
        r0 = blk * NA_BLOCK_ROWS + g * NA_Q_ROWS
        ws = jnp.clip(r0 - NA_WIN_ROWS // 2, 0, rows - NA_KEY_ROWS)
        typ = jnp.where(r0 == 0, 0, jnp.where(r0 == rows - NA_Q_ROWS, 2, 1))
        return pl.multiple_of(ws * GRID_W, 2 * LANES), typ

    def bias_tile(h, typ):
        def block(kr, qi):
            first, interior, last = _na_block_index(rows, kr, qi)
            idx = jnp.where(typ == 0, first, jnp.where(typ == 1, interior, last))
            return bias_ref[h, idx]
        key_rows = []
        for kr in range(NA_KEY_ROWS):
            pairs = [jnp.where(left_half, block(kr, qi), block(kr, qi + 1)) for qi in range(0, NA_Q_ROWS, 2)]
            key_rows.append(jnp.concatenate(pairs, axis=1))
        return jnp.concatenate(key_rows, axis=0)

    def scores(g, h):
        tok0, typ = window(g)
        hp, sub = divmod(h, 2)
        lanes = slice(hp * LANES, (hp + 1) * LANES)
        k_pair = k_ref[pl.ds(tok0, nk), lanes]
        q_pair = qT_ref[lanes, g * nq:(g + 1) * nq]
        keep = (row_id >= sub * NA_HEAD_DIM) & (row_id < (sub + 1) * NA_HEAD_DIM)
        q_h = jnp.where(keep, q_pair, jnp.zeros_like(q_pair))
        s = (_dot(k_pair, q_h) + bias_tile(h, typ)).astype(BF16)
        return s, jnp.max(s, axis=0, keepdims=True)

    def finish(g, h, s, m):
        tok0, _ = window(g)
        hv = slice(h * NA_HEAD_DIM, (h + 1) * NA_HEAD_DIM)
        p = jnp.exp2(s - m)
        acc = _dot(vT_ref[h * V_EXT:(h + 1) * V_EXT, pl.ds(tok0, nk)], p)
        o = acc[:NA_HEAD_DIM] * (1.0 / acc[NA_HEAD_DIM:NA_HEAD_DIM + 1])
        oT_ref[hv, g * nq:(g + 1) * nq] = o.astype(BF16)

    pending = [scores(*it) for it in items[:NA_LOOKAHEAD]]
    for i, item in enumerate(items):
        if i + NA_LOOKAHEAD < len(items):
            pending.append(scores(*items[i + NA_LOOKAHEAD]))
        finish(*item, *pending.pop(0))


def _na_call(qT, k, vT, bias, batch, seq):
    rows = seq // GRID_W
    blocks = rows // NA_BLOCK_ROWS
    tq = NA_BLOCK_ROWS * GRID_W
    n_tok = batch * seq
    return pl.pallas_call(
        functools.partial(_na_kernel, rows=rows),
        grid=(batch, blocks),
        in_specs=[
            pl.BlockSpec((NA_WIDTH, tq), lambda b, i: (0, b * blocks + i)),
            pl.BlockSpec((seq, NA_WIDTH), lambda b, i: (b, 0)),
            pl.BlockSpec((NA_HEADS * V_EXT, seq), lambda b, i: (0, b)),
            _resident(bias.shape),
        ],
        out_specs=pl.BlockSpec((NA_WIDTH, tq), lambda b, i: (0, b * blocks + i)),
        out_shape=jax.ShapeDtypeStruct((NA_WIDTH, n_tok), BF16),
        compiler_params=pltpu.CompilerParams(
            dimension_semantics=("arbitrary", "arbitrary"), vmem_limit_bytes=VMEM_LIMIT),
        name="na",
    )(qT, k, vT, bias)


def _mla_kernel(qT_ref, k_ref, vT_ref, oT_ref, s_even_ref, s_odd_ref, *, seq):
    n_chunks = seq // MLA_KC
    tq = oT_ref.shape[1]
    s_refs = (s_even_ref, s_odd_ref)

    def score_chunk(h, c, slot, m):
        hk = slice(h * MLA_HEAD_PAD, (h + 1) * MLA_HEAD_PAD)
        s = _dot(k_ref[h, c * MLA_KC:(c + 1) * MLA_KC, :], qT_ref[hk, :]).astype(BF16)
        s_refs[slot][c * MLA_KC:(c + 1) * MLA_KC, :] = s
        return jnp.maximum(m, jnp.max(s, axis=0, keepdims=True))

    neg = jnp.full((1, tq), -jnp.inf, BF16)
    m_next = neg
    for c in range(n_chunks):
        m_next = score_chunk(0, c, 0, m_next)
    for h in range(MLA_HEADS):
        slot = h % 2
        m_cur, m_next = m_next, neg
        hv = slice(h * MLA_V, (h + 1) * MLA_V)
        acc = jnp.zeros((MLA_V + SUM_ROWS, tq), F32)
        for c in range(n_chunks):
            if h + 1 < MLA_HEADS:
                m_next = score_chunk(h + 1, c, 1 - slot, m_next)
            cs = slice(c * MLA_KC, (c + 1) * MLA_KC)
            p = jnp.exp2(s_refs[slot][cs, :] - m_cur)
            acc = acc + _dot(vT_ref[h * V_EXT:(h + 1) * V_EXT, cs], p)
        oT_ref[hv, :] = (acc[:MLA_V] * (1.0 / acc[MLA_V:MLA_V + 1])).astype(BF16)


def _mla_call(qT, k, vT, batch, seq):
    nq = seq // MLA_TQ
    n_tok = batch * seq
    return pl.pallas_call(
        functools.partial(_mla_kernel, seq=seq),
        grid=(batch, nq),
        in_specs=[
            pl.BlockSpec((MLA_HEADS * MLA_HEAD_PAD, MLA_TQ), lambda b, i: (0, b * nq + i)),
            pl.BlockSpec((MLA_HEADS, seq, MLA_HEAD_PAD), lambda b, i: (0, b, 0)),
            pl.BlockSpec((MLA_HEADS * V_EXT, seq), lambda b, i: (0, b)),
        ],
        out_specs=pl.BlockSpec((MLA_WIDTH, MLA_TQ), lambda b, i: (0, b * nq + i)),
        out_shape=jax.ShapeDtypeStruct((MLA_WIDTH, n_tok), BF16),
        scratch_shapes=[pltpu.VMEM((seq, MLA_TQ), BF16), pltpu.VMEM((seq, MLA_TQ), BF16)],
        compiler_params=pltpu.CompilerParams(
            dimension_semantics=("arbitrary", "arbitrary"), vmem_limit_bytes=VMEM_LIMIT),
        name="mla",
    )(qT, k, vT)


def _layer_norm(v, g, b):
    mu = jnp.mean(v, axis=-1, keepdims=True)
    c = v - mu
    var = jnp.mean(c * c, axis=-1, keepdims=True)
    return c * lax.rsqrt(var + LN_EPS) * g + b


def _merge_kernel(x_ref, oa_ref, ob_ref, g_ref, wa_ref, wb_ref, wo_ref, bg_ref, lg_ref, lb_ref,
                  o_ref, *, alpha):
    subs = [slice(r0, r0 + MERGE_SUB) for r0 in range(0, o_ref.shape[0], MERGE_SUB)]
    branch = [(_dot_tn(oa_ref[:, r], wa_ref[...]), _dot_tn(ob_ref[:, r], wb_ref[...])) for r in subs]
    mixes = []
    for r, (y_a, y_b) in zip(subs, branch):
        g = g_ref[r, :].astype(F32) + bg_ref[...]
        merged = jax.nn.sigmoid(g[:, :D_MODEL]) * y_a + jax.nn.sigmoid(g[:, D_MODEL:]) * y_b
        mixes.append(_dot(merged.astype(BF16), wo_ref[...]))
    for r, mix in zip(subs, mixes):
        o_ref[r, :] = _layer_norm(alpha * x_ref[r, :] + mix, lg_ref[...], lb_ref[...])


def _merge_call(x2, oTa, oTb, gates, w, alpha):
    n_tok = x2.shape[0]
    tm = MERGE_TM
    tok = lambda i: (i, 0)
    tokT = lambda i: (0, i)
    return pl.pallas_call(
        functools.partial(_merge_kernel, alpha=alpha),
        grid=(n_tok // tm,),
        in_specs=[
            pl.BlockSpec((tm, D_MODEL), tok),
            pl.BlockSpec((NA_WIDTH, tm), tokT),
            pl.BlockSpec((MLA_WIDTH, tm), tokT),
            pl.BlockSpec((tm, 2 * D_MODEL), tok),
            _resident(w["wa"].shape), _resident(w["wb"].shape), _resident(w["wo"].shape),
            _resident(w["bg"].shape), _resident(w["ln1g"].shape), _resident(w["ln1b"].shape),
        ],
        out_specs=pl.BlockSpec((tm, D_MODEL), tok),
        out_shape=jax.ShapeDtypeStruct((n_tok, D_MODEL), F32),
        compiler_params=pltpu.CompilerParams(
            dimension_semantics=("arbitrary",), vmem_limit_bytes=VMEM_LIMIT),
        name="merge",
    )(x2, oTa, oTb, gates, w["wa"], w["wb"], w["wo"], w["bg"], w["ln1g"], w["ln1b"])


def _ffn_kernel(x_ref, win_ref, wdown_ref, lg_ref, lb_ref, o_ref, *, alpha):
    subs = [slice(r0, r0 + FFN_SUB) for r0 in range(0, o_ref.shape[0], FFN_SUB)]
    xs = [x_ref[r, :] for r in subs]
    xbs = [x.astype(BF16) for x in xs]
    accs = [alpha * x for x in xs]
    for c in range(FFN_HIDDEN // FFN_TF):
        cs = slice(c * FFN_TF, (c + 1) * FFN_TF)
        for i, xb in enumerate(xbs):
            gate = _dot(xb, win_ref[:, cs])
            up = _dot(xb, win_ref[:, FFN_HIDDEN + c * FFN_TF:FFN_HIDDEN + (c + 1) * FFN_TF])
            act = (gate * jax.nn.sigmoid(gate) * up).astype(BF16)
            accs[i] = accs[i] + _dot(act, wdown_ref[cs, :])
    for r, acc in zip(subs, accs):
        o_ref[r, :] = _layer_norm(acc, lg_ref[...], lb_ref[...])


def _ffn_call(x1, w, alpha):
    n_tok = x1.shape[0]
    tm = FFN_TM
    tok = lambda i: (i, 0)
    return pl.pallas_call(
        functools.partial(_ffn_kernel, alpha=alpha),
        grid=(n_tok // tm,),
        in_specs=[
            pl.BlockSpec((tm, D_MODEL), tok),
            _resident(w["wffn_in"].shape), _resident(w["wdown"].shape),
            _resident(w["ln2g"].shape), _resident(w["ln2b"].shape),
        ],
        out_specs=pl.BlockSpec((tm, D_MODEL), tok),
        out_shape=jax.ShapeDtypeStruct((n_tok, D_MODEL), F32),
        compiler_params=pltpu.CompilerParams(
            dimension_semantics=("arbitrary",), vmem_limit_bytes=VMEM_LIMIT),
        name="ffn",
    )(x1, w["wffn_in"], w["wdown"], w["ln2g"], w["ln2b"])


def _rope_swap(cols):
    q = MLA_ROPE // 4
    r1, r2, c1, c2 = cols[..., :q], cols[..., q:2 * q], cols[..., 2 * q:3 * q], cols[..., 3 * q:]
    return jnp.concatenate([-r2, r1, -c2, c1], axis=-1)


def _rope_tables(seq, q_scale):
    half = MLA_ROPE // 2
    quarter = MLA_ROPE // 4
    n_rows = seq // GRID_W
    inv_freq = ROPE_THETA ** (-jnp.arange(0, half, 2, dtype=F32) / half)
    freq = jnp.tile(inv_freq, LANES // quarter)
    coord = jnp.arange(max(n_rows, GRID_W), dtype=F32)
    ang = coord[:, None] * freq[None, :]
    cos_c, sin_c = jnp.cos(ang), jnp.sin(ang)
    d = np.arange(LANES)
    by_row = (d < half)[None, None, :]
    live = (d < MLA_ROPE)[None, None, :]

    def natural(tab):
        per_tok = jnp.where(by_row, tab[:n_rows, None, :], tab[None, :GRID_W, :])
        return jnp.where(live, per_tok, 0.0).reshape(seq, LANES)

    r = d - MLA_NOPE
    by_row_t = ((r >= 0) & (r < half))[:, None, None]
    rot = ((r >= 0) & (r < MLA_ROPE))[:, None, None]

    def transposed(tab, nope_value):
        tab_t = tab.T * q_scale
        per_tok = jnp.where(by_row_t, tab_t[:, :n_rows, None], tab_t[:, None, :GRID_W])
        full = jnp.where((d < MLA_NOPE)[:, None, None], nope_value, jnp.where(rot, per_tok, 0.0))
        return full.reshape(LANES, seq).astype(F32)

    return natural(cos_c), natural(sin_c), transposed(cos_c, q_scale), transposed(sin_c, 0.0)


def _prep_layer(seq, w_in, b_gate, na_rpb, q_norm, w_uq, kv_norm, w_ukv, w_bna, w_bmla, w_out,
                ln1_g, ln1_b, w_ffn_in, w_ffn_out, ln2_g, ln2_b):
    o_q, o_k, o_v = 0, NA_WIDTH, 2 * NA_WIDTH
    o_ql = 3 * NA_WIDTH
    o_kvl = o_ql + MLA_Q_LORA
    o_kr = o_kvl + MLA_KV_LORA
    o_g = o_kr + MLA_ROPE
    w = {}
    na_scale = NA_HEAD_DIM ** -0.5 * LOG2_E
    w_t = w_in.T
    w["wqT"] = (w_t[o_q:o_k] * na_scale).astype(BF16)
    w["wkT"] = w_t[o_k:o_v].astype(BF16)
    w["wvT"] = w_t[o_v:o_ql].astype(BF16)
    k_rope_t = w_t[o_kr:o_g]
    zrows = jnp.zeros((LANES - MLA_ROPE, D_MODEL), F32)
    w["wlatT"] = jnp.concatenate(
        [w_t[o_ql:o_kr], k_rope_t, zrows, _rope_swap(k_rope_t.T).T, zrows], axis=0).astype(BF16)
    w["wgT"] = w_t[o_g:].astype(BF16)
    w["qg"] = q_norm.reshape(1, MLA_Q_LORA)
    w["kvg"] = kv_norm.reshape(1, MLA_KV_LORA)

    uq = w_uq.reshape(MLA_Q_LORA, MLA_HEADS, MLA_QK)
    uq_pe = uq[:, :, MLA_NOPE:]
    uq_arr = jnp.concatenate([uq[:, :, :MLA_NOPE], uq_pe, _rope_swap(uq_pe)], axis=-1)
    w["wuqT"] = uq_arr.reshape(MLA_Q_LORA, MLA_HEADS * MLA_HEAD_PAD).T.astype(BF16)
    ukv = w_ukv.reshape(MLA_KV_LORA, MLA_HEADS, MLA_NOPE + MLA_V)
    uk_arr = jnp.concatenate(
        [ukv[:, :, :MLA_NOPE], jnp.zeros((MLA_KV_LORA, MLA_HEADS, MLA_HEAD_PAD - MLA_NOPE), F32)], axis=-1)
    w["wuk"] = uk_arr.reshape(MLA_KV_LORA, MLA_HEADS * MLA_HEAD_PAD).astype(BF16)
    w["wuvT"] = ukv[:, :, MLA_NOPE:].reshape(MLA_KV_LORA, MLA_WIDTH).T.astype(BF16)
    e = np.zeros((LANES, MLA_HEADS * MLA_HEAD_PAD), np.float32)
    for h in range(MLA_HEADS):
        e[np.arange(MLA_ROPE), h * MLA_HEAD_PAD + MLA_NOPE + np.arange(MLA_ROPE)] = 1.0
    w["e"] = jnp.asarray(e, BF16)

    q_scale = MLA_QK ** -0.5 * LOG2_E
    w["cn"], w["sn"], w["ctf"], w["stf"] = _rope_tables(seq, q_scale)

    w["na_bias"] = _na_bias_blocks(na_rpb)

    w["wa"] = w_bna.astype(BF16)
    w["wb"] = w_bmla.astype(BF16)
    w["wo"] = w_out.astype(BF16)
    w["bg"] = b_gate.reshape(1, 2 * D_MODEL)
    w["ln1g"] = ln1_g.reshape(1, D_MODEL)
    w["ln1b"] = ln1_b.reshape(1, D_MODEL)
    w["wffn_in"] = w_ffn_in.astype(BF16)
    w["wdown"] = w_ffn_out.astype(BF16)
    w["ln2g"] = ln2_g.reshape(1, D_MODEL)
    w["ln2b"] = ln2_b.reshape(1, D_MODEL)
    return w


def kernel(x, w_in, b_gate, na_rpb, mla_q_norm, mla_w_uq, mla_kv_norm, mla_w_ukv, w_branch_na, w_branch_mla, w_out, ln1_g, ln1_b, w_ffn_in, w_ffn_out, ln2_g, ln2_b):
    batch, seq, d = x.shape
    depth = w_in.shape[0]
    alpha = (2.0 * depth) ** 0.25
    n_tok = batch * seq
    x2 = x.reshape(n_tok, d)
    for l in range(depth):
        w = _prep_layer(seq, w_in[l], b_gate[l], na_rpb[l], mla_q_norm[l], mla_w_uq[l], mla_kv_norm[l],
                        mla_w_ukv[l], w_branch_na[l], w_branch_mla[l], w_out[l], ln1_g[l], ln1_b[l],
                        w_ffn_in[l], w_ffn_out[l], ln2_g[l], ln2_b[l])
        qT_na, k_na, vT_na, qT_mla, k_mla, vT_mla, gates = _proj_call(x2, w, n_tok, seq)
        oT_na = _na_call(qT_na, k_na, vT_na, w["na_bias"], batch, seq)
        oT_mla = _mla_call(qT_mla, k_mla, vT_mla, batch, seq)
        x1 = _merge_call(x2, oT_na, oT_mla, gates, w, alpha)
        x2 = _ffn_call(x1, w, alpha)
    return x2.reshape(batch, seq, d)
```

```python
import functools

import numpy as np
import jax
import jax.numpy as jnp
from jax import lax
from jax.experimental import pallas as pl
from jax.experimental.pallas import tpu as pltpu

F32 = jnp.float32
BF16 = jnp.bfloat16

D_MODEL = 1024
GRID_W = 64
NA_HEADS = 8
NA_HEAD_DIM = 64
NA_WIDTH = NA_HEADS * NA_HEAD_DIM
NA_WIN_ROWS = 8
NA_WIN_COLS = 16
MLA_HEADS = 8
MLA_Q_LORA = 256
MLA_KV_LORA = 128
MLA_NOPE = 64
MLA_ROPE = 32
MLA_QK = MLA_NOPE + MLA_ROPE
MLA_V = 64
MLA_WIDTH = MLA_HEADS * MLA_V
ROPE_THETA = 10000.0
FFN_HIDDEN = 2816
LN_EPS = 1e-5
RMS_EPS = 1e-6

LANES = 128
MLA_HEAD_PAD = LANES
VMEM_LIMIT = 56 * 1024 * 1024

PROJ_TM = 512
NA_Q_ROWS = 4
NA_KEY_ROWS = 12
NA_BLOCK_ROWS = 16
NA_LOOKAHEAD = 4
MLA_TQ = 256
MLA_KC = 512
MLA_LOOKAHEAD = 4
HEAD_V = 64
SUM_ROWS = 16
V_EXT = HEAD_V + SUM_ROWS
LOG2_E = 1.4426950408889634
MERGE_TM = 1024
MERGE_SUB = 256
FFN_TM = 1024
FFN_SUB = 512
FFN_TF = 256

NEG_BIG = -1e30

NT = (((1,), (1,)), ((), ()))
TN = (((0,), (0,)), ((), ()))


def _dot(a, b):
    return jnp.dot(a, b, preferred_element_type=F32)


def _dot_nt(a, b):
    return lax.dot_general(a, b, NT, preferred_element_type=F32)


def _dot_tn(a, b):
    return lax.dot_general(a, b, TN, preferred_element_type=F32)


def _resident(shape):
    nd = len(shape)
    return pl.BlockSpec(shape, lambda *_: (0,) * nd, pipeline_mode=pl.Buffered(1))


def _rms(v, g):
    return v * lax.rsqrt(jnp.mean(v * v, axis=-1, keepdims=True) + RMS_EPS) * g


def _store_v_ext(ref, vT):
    ones = jnp.ones((SUM_ROWS, vT.shape[1]), BF16)
    for h in range(vT.shape[0] // HEAD_V):
        ref[h * V_EXT:h * V_EXT + HEAD_V, :] = vT[h * HEAD_V:(h + 1) * HEAD_V].astype(BF16)
        ref[h * V_EXT + HEAD_V:(h + 1) * V_EXT, :] = ones


def _proj_kernel(x_ref, wqT_ref, wkT_ref, wvT_ref, wlatT_ref, wgT_ref, qg_ref, kvg_ref,
                 wuqT_ref, wuk_ref, e_ref, wuvT_ref, cn_ref, sn_ref, ctf_ref, stf_ref,
                 qT_na_ref, k_na_ref, vT_na_ref, qT_mla_ref, k_mla_ref, vT_mla_ref, g_ref):
    xb = x_ref[...].astype(BF16)
    lat = _dot_nt(xb, wlatT_ref[...])
    qT_na_ref[...] = _dot_nt(wqT_ref[...], xb).astype(BF16)
    qn = _rms(lat[:, :MLA_Q_LORA], qg_ref[...]).astype(BF16)
    kvn = _rms(lat[:, MLA_Q_LORA:MLA_Q_LORA + MLA_KV_LORA], kvg_ref[...]).astype(BF16)
    kpe = lat[:, 384:512] * cn_ref[...] + lat[:, 512:640] * sn_ref[...]
    k_na_ref[...] = _dot_nt(xb, wkT_ref[...]).astype(BF16)

    qT = _dot_nt(wuqT_ref[...], qn)
    _store_v_ext(vT_na_ref, _dot_nt(wvT_ref[...], xb))
    ctf = ctf_ref[...]
    stf = stf_ref[...]
    for h in range(MLA_HEADS):
        qh = qT[h * MLA_HEAD_PAD:(h + 1) * MLA_HEAD_PAD]
        sw = jnp.concatenate([qh[MLA_ROPE:], qh[:MLA_ROPE]], axis=0)
        qT_mla_ref[h * MLA_HEAD_PAD:(h + 1) * MLA_HEAD_PAD, :] = (qh * ctf + sw * stf).astype(BF16)

    k_mla = _dot(kvn, wuk_ref[...]) + _dot(kpe.astype(BF16), e_ref[...])
    for h in range(MLA_HEADS):
        k_mla_ref[h] = k_mla[:, h * MLA_HEAD_PAD:(h + 1) * MLA_HEAD_PAD].astype(BF16)
    _store_v_ext(vT_mla_ref, _dot_nt(wuvT_ref[...], kvn))
    g_ref[...] = _dot_nt(xb, wgT_ref[...]).astype(BF16)


def _proj_call(x2, w, n_tok, seq):
    tm = PROJ_TM
    steps_per_seq = seq // tm
    tok = lambda i: (i, 0)
    tokT = lambda i: (0, i)
    tab = lambda i: (i % steps_per_seq, 0)
    tabT = lambda i: (0, i % steps_per_seq)
    in_specs = [
        pl.BlockSpec((tm, D_MODEL), tok),
        _resident(w["wqT"].shape), _resident(w["wkT"].shape), _resident(w["wvT"].shape),
        _resident(w["wlatT"].shape), _resident(w["wgT"].shape),
        _resident(w["qg"].shape), _resident(w["kvg"].shape),
        _resident(w["wuqT"].shape), _resident(w["wuk"].shape), _resident(w["e"].shape),
        _resident(w["wuvT"].shape),
        pl.BlockSpec((tm, LANES), tab), pl.BlockSpec((tm, LANES), tab),
        pl.BlockSpec((MLA_HEAD_PAD, tm), tabT), pl.BlockSpec((MLA_HEAD_PAD, tm), tabT),
    ]
    out_shape = [
        jax.ShapeDtypeStruct((NA_WIDTH, n_tok), BF16),
        jax.ShapeDtypeStruct((n_tok, NA_WIDTH), BF16),
        jax.ShapeDtypeStruct((NA_HEADS * V_EXT, n_tok), BF16),
        jax.ShapeDtypeStruct((MLA_HEADS * MLA_HEAD_PAD, n_tok), BF16),
        jax.ShapeDtypeStruct((MLA_HEADS, n_tok, MLA_HEAD_PAD), BF16),
        jax.ShapeDtypeStruct((MLA_HEADS * V_EXT, n_tok), BF16),
        jax.ShapeDtypeStruct((n_tok, 2 * D_MODEL), BF16),
    ]
    out_specs = [
        pl.BlockSpec((NA_WIDTH, tm), tokT),
        pl.BlockSpec((tm, NA_WIDTH), tok),
        pl.BlockSpec((NA_HEADS * V_EXT, tm), tokT),
        pl.BlockSpec((MLA_HEADS * MLA_HEAD_PAD, tm), tokT),
        pl.BlockSpec((MLA_HEADS, tm, MLA_HEAD_PAD), lambda i: (0, i, 0)),
        pl.BlockSpec((MLA_HEADS * V_EXT, tm), tokT),
        pl.BlockSpec((tm, 2 * D_MODEL), tok),
    ]
    return pl.pallas_call(
        _proj_kernel,
        grid=(n_tok // tm,),
        in_specs=in_specs,
        out_specs=out_specs,
        out_shape=out_shape,
        compiler_params=pltpu.CompilerParams(
            dimension_semantics=("arbitrary",), vmem_limit_bytes=VMEM_LIMIT),
        name="proj",
    )(x2, w["wqT"], w["wkT"], w["wvT"], w["wlatT"], w["wgT"], w["qg"], w["kvg"],
      w["wuqT"], w["wuk"], w["e"], w["wuvT"], w["cn"], w["sn"], w["ctf"], w["stf"])


def _na_group_geometry(rows):
    return [(0, 0), (NA_Q_ROWS, 0), (rows - NA_Q_ROWS, rows - NA_KEY_ROWS)]


NA_MASKED_BLOCK = 2 * NA_WIN_ROWS - 1


def _na_block_index(rows, kr, qi):
    out = []
    for r0, ws in _na_group_geometry(rows):
        q_row, k_row = r0 + qi, ws + kr
        r_start = min(max(q_row - NA_WIN_ROWS // 2, 0), rows - NA_WIN_ROWS)
        valid = r_start <= k_row < r_start + NA_WIN_ROWS
        out.append(k_row - q_row + NA_WIN_ROWS - 1 if valid else NA_MASKED_BLOCK)
    return out


def _na_bias_blocks(rpb):
    n_dr = 2 * NA_WIN_ROWS - 1
    n_dc = 2 * NA_WIN_COLS - 1
    w = GRID_W
    lead = w - NA_WIN_COLS
    v = jnp.pad(rpb, ((0, 0), (0, 0), (lead, 2 * w - lead - n_dc)))
    a = jnp.tile(v, (1, 1, w))[:, :, :w * (2 * w - 1)].reshape(NA_HEADS, n_dr, w, 2 * w - 1)
    toe = jnp.swapaxes(a[:, :, :, w - 1:], 2, 3)
    col = np.arange(w)
    col_start = np.clip(col - NA_WIN_COLS // 2, 0, w - NA_WIN_COLS)
    v_col = (col[:, None] >= col_start[None, :]) & (col[:, None] < col_start[None, :] + NA_WIN_COLS)
    toe = jnp.where(v_col[None, None], toe * LOG2_E, NEG_BIG)
    toe = jnp.concatenate([toe, jnp.full((NA_HEADS, 1, w, w), NEG_BIG, F32)], axis=1)
    return jnp.concatenate([toe, toe], axis=-1)


def _na_kernel(qT_ref, k_ref, vT_ref, bias_ref, oT_ref, *, rows):
    blk = pl.program_id(1)
    nk = NA_KEY_ROWS * GRID_W
    nq = NA_Q_ROWS * GRID_W
    row_id = lax.broadcasted_iota(jnp.int32, (2 * NA_HEAD_DIM, nq), 0)
    left_half = lax.broadcasted_iota(jnp.int32, (GRID_W, LANES), 1) < GRID_W
    items = [(g, h) for g in range(NA_BLOCK_ROWS // NA_Q_ROWS) for h in range(NA_HEADS)]

    def window(g):
        r0 = blk * NA_BLOCK_ROWS + g * NA_Q_ROWS
        ws = jnp.clip(r0 - NA_WIN_ROWS // 2, 0, rows - NA_KEY_ROWS)
        typ = jnp.where(r0 == 0, 0, jnp.where(r0 == rows - NA_Q_ROWS, 2, 1))
        return pl.multiple_of(ws * GRID_W, 2 * LANES), typ

    def bias_tile(h, typ):
        def block(kr, qi):
            first, interior, last = _na_block_index(rows, kr, qi)
            idx = jnp.where(typ == 0, first, jnp.where(typ == 1, interior, last))
            return bias_ref[h, idx]
        key_rows = []
        for kr in range(NA_KEY_ROWS):
            pairs = [jnp.where(left_half, block(kr, qi), block(kr, qi + 1)) for qi in range(0, NA_Q_ROWS, 2)]
            key_rows.append(jnp.concatenate(pairs, axis=1))
        return jnp.concatenate(key_rows, axis=0)

    def scores(g, h):
        tok0, typ = window(g)
        hp, sub = divmod(h, 2)
        lanes = slice(hp * LANES, (hp + 1) * LANES)
        k_pair = k_ref[pl.ds(tok0, nk), lanes]
        q_pair = qT_ref[lanes, g * nq:(g + 1) * nq]
        keep = (row_id >= sub * NA_HEAD_DIM) & (row_id < (sub + 1) * NA_HEAD_DIM)
        q_h = jnp.where(keep, q_pair, jnp.zeros_like(q_pair))
        s = (_dot(k_pair, q_h) + bias_tile(h, typ)).astype(BF16)
        return s, jnp.max(s, axis=0, keepdims=True)

    def finish(g, h, s, m):
        tok0, _ = window(g)
        hv = slice(h * NA_HEAD_DIM, (h + 1) * NA_HEAD_DIM)
        p = jnp.exp2(s - m)
        acc = _dot(vT_ref[h * V_EXT:(h + 1) * V_EXT, pl.ds(tok0, nk)], p)
        o = acc[:NA_HEAD_DIM] * (1.0 / acc[NA_HEAD_DIM:NA_HEAD_DIM + 1])
        oT_ref[hv, g * nq:(g + 1) * nq] = o.astype(BF16)

    pending = [scores(*it) for it in items[:NA_LOOKAHEAD]]
    for i, item in enumerate(items):
        if i + NA_LOOKAHEAD < len(items):
            pending.append(scores(*items[i + NA_LOOKAHEAD]))
        finish(*item, *pending.pop(0))


def _na_call(qT, k, vT, bias, batch, seq):
    rows = seq // GRID_W
    blocks = rows // NA_BLOCK_ROWS
    tq = NA_BLOCK_ROWS * GRID_W
    n_tok = batch * seq
    return pl.pallas_call(
        functools.partial(_na_kernel, rows=rows),
        grid=(batch, blocks),
        in_specs=[
            pl.BlockSpec((NA_WIDTH, tq), lambda b, i: (0, b * blocks + i)),
            pl.BlockSpec((seq, NA_WIDTH), lambda b, i: (b, 0)),
            pl.BlockSpec((NA_HEADS * V_EXT, seq), lambda b, i: (0, b)),
            _resident(bias.shape),
        ],
        out_specs=pl.BlockSpec((NA_WIDTH, tq), lambda b, i: (0, b * blocks + i)),
        out_shape=jax.ShapeDtypeStruct((NA_WIDTH, n_tok), BF16),
        compiler_params=pltpu.CompilerParams(
            dimension_semantics=("arbitrary", "arbitrary"), vmem_limit_bytes=VMEM_LIMIT),
        name="na",
    )(qT, k, vT, bias)


def _mla_kernel(qT_ref, k_ref, vT_ref, oT_ref, *, seq):
    n_chunks = seq // MLA_KC
    tq = oT_ref.shape[1]
    items = [(h, c) for h in range(MLA_HEADS) for c in range(n_chunks)]

    def scores(h, c):
        hk = slice(h * MLA_HEAD_PAD, (h + 1) * MLA_HEAD_PAD)
        s = _dot(k_ref[h, c * MLA_KC:(c + 1) * MLA_KC, :], qT_ref[hk, :]).astype(BF16)
        return s, jnp.max(s, axis=0, keepdims=True)

    m = acc = None
    pending = [scores(*it) for it in items[:MLA_LOOKAHEAD]]
    for i, (h, c) in enumerate(items):
        if i + MLA_LOOKAHEAD < len(items):
            pending.append(scores(*items[i + MLA_LOOKAHEAD]))
        s, chunk_max = pending.pop(0)
        cs = slice(c * MLA_KC, (c + 1) * MLA_KC)
        if c == 0:
            m = chunk_max
            acc = _dot(vT_ref[h * V_EXT:(h + 1) * V_EXT, cs], jnp.exp2(s - m))
        else:
            m_new = jnp.maximum(m, chunk_max)
            rescale = jnp.exp2((m - m_new).astype(F32))
            acc = rescale * acc + _dot(vT_ref[h * V_EXT:(h + 1) * V_EXT, cs], jnp.exp2(s - m_new))
            m = m_new
        if c == n_chunks - 1:
            hv = slice(h * MLA_V, (h + 1) * MLA_V)
            oT_ref[hv, :] = (acc[:MLA_V] * (1.0 / acc[MLA_V:MLA_V + 1])).astype(BF16)


def _mla_call(qT, k, vT, batch, seq):
    nq = seq // MLA_TQ
    n_tok = batch * seq
    return pl.pallas_call(
        functools.partial(_mla_kernel, seq=seq),
        grid=(batch, nq),
        in_specs=[
            pl.BlockSpec((MLA_HEADS * MLA_HEAD_PAD, MLA_TQ), lambda b, i: (0, b * nq + i)),
            pl.BlockSpec((MLA_HEADS, seq, MLA_HEAD_PAD), lambda b, i: (0, b, 0)),
            pl.BlockSpec((MLA_HEADS * V_EXT, seq), lambda b, i: (0, b)),
        ],
        out_specs=pl.BlockSpec((MLA_WIDTH, MLA_TQ), lambda b, i: (0, b * nq + i)),
        out_shape=jax.ShapeDtypeStruct((MLA_WIDTH, n_tok), BF16),
        compiler_params=pltpu.CompilerParams(
            dimension_semantics=("arbitrary", "arbitrary"), vmem_limit_bytes=VMEM_LIMIT),
        name="mla",
    )(qT, k, vT)


def _layer_norm(v, g, b):
    mu = jnp.mean(v, axis=-1, keepdims=True)
    c = v - mu
    var = jnp.mean(c * c, axis=-1, keepdims=True)
    return c * lax.rsqrt(var + LN_EPS) * g + b


def _merge_kernel(x_ref, oa_ref, ob_ref, g_ref, wa_ref, wb_ref, wo_ref, bg_ref, lg_ref, lb_ref,
                  o_ref, *, alpha):
    subs = [slice(r0, r0 + MERGE_SUB) for r0 in range(0, o_ref.shape[0], MERGE_SUB)]
    branch = [(_dot_tn(oa_ref[:, r], wa_ref[...]), _dot_tn(ob_ref[:, r], wb_ref[...])) for r in subs]
    mixes = []
    for r, (y_a, y_b) in zip(subs, branch):
        g = g_ref[r, :].astype(F32) + bg_ref[...]
        merged = jax.nn.sigmoid(g[:, :D_MODEL]) * y_a + jax.nn.sigmoid(g[:, D_MODEL:]) * y_b
        mixes.append(_dot(merged.astype(BF16), wo_ref[...]))
    for r, mix in zip(subs, mixes):
        o_ref[r, :] = _layer_norm(alpha * x_ref[r, :] + mix, lg_ref[...], lb_ref[...])


def _merge_call(x2, oTa, oTb, gates, w, alpha):
    n_tok = x2.shape[0]
    tm = MERGE_TM
    tok = lambda i: (i, 0)
    tokT = lambda i: (0, i)
    return pl.pallas_call(
        functools.partial(_merge_kernel, alpha=alpha),
        grid=(n_tok // tm,),
        in_specs=[
            pl.BlockSpec((tm, D_MODEL), tok),
            pl.BlockSpec((NA_WIDTH, tm), tokT),
            pl.BlockSpec((MLA_WIDTH, tm), tokT),
            pl.BlockSpec((tm, 2 * D_MODEL), tok),
            _resident(w["wa"].shape), _resident(w["wb"].shape), _resident(w["wo"].shape),
            _resident(w["bg"].shape), _resident(w["ln1g"].shape), _resident(w["ln1b"].shape),
        ],
        out_specs=pl.BlockSpec((tm, D_MODEL), tok),
        out_shape=jax.ShapeDtypeStruct((n_tok, D_MODEL), F32),
        compiler_params=pltpu.CompilerParams(
            dimension_semantics=("arbitrary",), vmem_limit_bytes=VMEM_LIMIT),
        name="merge",
    )(x2, oTa, oTb, gates, w["wa"], w["wb"], w["wo"], w["bg"], w["ln1g"], w["ln1b"])


def _ffn_kernel(x_ref, win_ref, wdown_ref, lg_ref, lb_ref, o_ref, *, alpha):
    subs = [slice(r0, r0 + FFN_SUB) for r0 in range(0, o_ref.shape[0], FFN_SUB)]
    xs = [x_ref[r, :] for r in subs]
    xbs = [x.astype(BF16) for x in xs]
    accs = [alpha * x for x in xs]
    for c in range(FFN_HIDDEN // FFN_TF):
        cs = slice(c * FFN_TF, (c + 1) * FFN_TF)
        for i, xb in enumerate(xbs):
            gate = _dot(xb, win_ref[:, cs])
            up = _dot(xb, win_ref[:, FFN_HIDDEN + c * FFN_TF:FFN_HIDDEN + (c + 1) * FFN_TF])
            act = (gate * jax.nn.sigmoid(gate) * up).astype(BF16)
            accs[i] = accs[i] + _dot(act, wdown_ref[cs, :])
    for r, acc in zip(subs, accs):
        o_ref[r, :] = _layer_norm(acc, lg_ref[...], lb_ref[...])


def _ffn_call(x1, w, alpha):
    n_tok = x1.shape[0]
    tm = FFN_TM
    tok = lambda i: (i, 0)
    return pl.pallas_call(
        functools.partial(_ffn_kernel, alpha=alpha),
        grid=(n_tok // tm,),
        in_specs=[
            pl.BlockSpec((tm, D_MODEL), tok),
            _resident(w["wffn_in"].shape), _resident(w["wdown"].shape),
            _resident(w["ln2g"].shape), _resident(w["ln2b"].shape),
        ],
        out_specs=pl.BlockSpec((tm, D_MODEL), tok),
        out_shape=jax.ShapeDtypeStruct((n_tok, D_MODEL), F32),
        compiler_params=pltpu.CompilerParams(
            dimension_semantics=("arbitrary",), vmem_limit_bytes=VMEM_LIMIT),
        name="ffn",
    )(x1, w["wffn_in"], w["wdown"], w["ln2g"], w["ln2b"])


def _rope_swap(cols):
    q = MLA_ROPE // 4
    r1, r2, c1, c2 = cols[..., :q], cols[..., q:2 * q], cols[..., 2 * q:3 * q], cols[..., 3 * q:]
    return jnp.concatenate([-r2, r1, -c2, c1], axis=-1)


def _rope_tables(seq, q_scale):
    half = MLA_ROPE // 2
    quarter = MLA_ROPE // 4
    n_rows = seq // GRID_W
    inv_freq = ROPE_THETA ** (-jnp.arange(0, half, 2, dtype=F32) / half)
    freq = jnp.tile(inv_freq, LANES // quarter)
    coord = jnp.arange(max(n_rows, GRID_W), dtype=F32)
    ang = coord[:, None] * freq[None, :]
    cos_c, sin_c = jnp.cos(ang), jnp.sin(ang)
    d = np.arange(LANES)
    by_row = (d < half)[None, None, :]
    live = (d < MLA_ROPE)[None, None, :]

    def natural(tab):
        per_tok = jnp.where(by_row, tab[:n_rows, None, :], tab[None, :GRID_W, :])
        return jnp.where(live, per_tok, 0.0).reshape(seq, LANES)

    r = d - MLA_NOPE
    by_row_t = ((r >= 0) & (r < half))[:, None, None]
    rot = ((r >= 0) & (r < MLA_ROPE))[:, None, None]

    def transposed(tab, nope_value):
        tab_t = tab.T * q_scale
        per_tok = jnp.where(by_row_t, tab_t[:, :n_rows, None], tab_t[:, None, :GRID_W])
        full = jnp.where((d < MLA_NOPE)[:, None, None], nope_value, jnp.where(rot, per_tok, 0.0))
        return full.reshape(LANES, seq).astype(F32)

    return natural(cos_c), natural(sin_c), transposed(cos_c, q_scale), transposed(sin_c, 0.0)


def _prep_layer(seq, w_in, b_gate, na_rpb, q_norm, w_uq, kv_norm, w_ukv, w_bna, w_bmla, w_out,
                ln1_g, ln1_b, w_ffn_in, w_ffn_out, ln2_g, ln2_b):
    o_q, o_k, o_v = 0, NA_WIDTH, 2 * NA_WIDTH
    o_ql = 3 * NA_WIDTH
    o_kvl = o_ql + MLA_Q_LORA
    o_kr = o_kvl + MLA_KV_LORA
    o_g = o_kr + MLA_ROPE
    w = {}
    na_scale = NA_HEAD_DIM ** -0.5 * LOG2_E
    w_t = w_in.T
    w["wqT"] = (w_t[o_q:o_k] * na_scale).astype(BF16)
    w["wkT"] = w_t[o_k:o_v].astype(BF16)
    w["wvT"] = w_t[o_v:o_ql].astype(BF16)
    k_rope_t = w_t[o_kr:o_g]
    zrows = jnp.zeros((LANES - MLA_ROPE, D_MODEL), F32)
    w["wlatT"] = jnp.concatenate(
        [w_t[o_ql:o_kr], k_rope_t, zrows, _rope_swap(k_rope_t.T).T, zrows], axis=0).astype(BF16)
    w["wgT"] = w_t[o_g:].astype(BF16)
    w["qg"] = q_norm.reshape(1, MLA_Q_LORA)
    w["kvg"] = kv_norm.reshape(1, MLA_KV_LORA)

    uq = w_uq.reshape(MLA_Q_LORA, MLA_HEADS, MLA_QK)
    uq_pe = uq[:, :, MLA_NOPE:]
    uq_arr = jnp.concatenate([uq[:, :, :MLA_NOPE], uq_pe, _rope_swap(uq_pe)], axis=-1)
    w["wuqT"] = uq_arr.reshape(MLA_Q_LORA, MLA_HEADS * MLA_HEAD_PAD).T.astype(BF16)
    ukv = w_ukv.reshape(MLA_KV_LORA, MLA_HEADS, MLA_NOPE + MLA_V)
    uk_arr = jnp.concatenate(
        [ukv[:, :, :MLA_NOPE], jnp.zeros((MLA_KV_LORA, MLA_HEADS, MLA_HEAD_PAD - MLA_NOPE), F32)], axis=-1)
    w["wuk"] = uk_arr.reshape(MLA_KV_LORA, MLA_HEADS * MLA_HEAD_PAD).astype(BF16)
    w["wuvT"] = ukv[:, :, MLA_NOPE:].reshape(MLA_KV_LORA, MLA_WIDTH).T.astype(BF16)
    e = np.zeros((LANES, MLA_HEADS * MLA_HEAD_PAD), np.float32)
    for h in range(MLA_HEADS):
        e[np.arange(MLA_ROPE), h * MLA_HEAD_PAD + MLA_NOPE + np.arange(MLA_ROPE)] = 1.0
    w["e"] = jnp.asarray(e, BF16)

    q_scale = MLA_QK ** -0.5 * LOG2_E
    w["cn"], w["sn"], w["ctf"], w["stf"] = _rope_tables(seq, q_scale)

    w["na_bias"] = _na_bias_blocks(na_rpb)

    w["wa"] = w_bna.astype(BF16)
    w["wb"] = w_bmla.astype(BF16)
    w["wo"] = w_out.astype(BF16)
    w["bg"] = b_gate.reshape(1, 2 * D_MODEL)
    w["ln1g"] = ln1_g.reshape(1, D_MODEL)
    w["ln1b"] = ln1_b.reshape(1, D_MODEL)
    w["wffn_in"] = w_ffn_in.astype(BF16)
    w["wdown"] = w_ffn_out.astype(BF16)
    w["ln2g"] = ln2_g.reshape(1, D_MODEL)
    w["ln2b"] = ln2_b.reshape(1, D_MODEL)
    return w


def kernel(x, w_in, b_gate, na_rpb, mla_q_norm, mla_w_uq, mla_kv_norm, mla_w_ukv, w_branch_na, w_branch_mla, w_out, ln1_g, ln1_b, w_ffn_in, w_ffn_out, ln2_g, ln2_b):
    batch, seq, d = x.shape
    depth = w_in.shape[0]
    alpha = (2.0 * depth) ** 0.25
    n_tok = batch * seq
    x2 = x.reshape(n_tok, d)
    for l in range(depth):
        w = _prep_layer(seq, w_in[l], b_gate[l], na_rpb[l], mla_q_norm[l], mla_w_uq[l], mla_kv_norm[l],
                        mla_w_ukv[l], w_branch_na[l], w_branch_mla[l], w_out[l], ln1_g[l], ln1_b[l],
                        w_ffn_in[l], w_ffn_out[l], ln2_g[l], ln2_b[l])
        qT_na, k_na, vT_na, qT_mla, k_mla, vT_mla, gates = _proj_call(x2, w, n_tok, seq)
        oT_na = _na_call(qT_na, k_na, vT_na, w["na_bias"], batch, seq)
        oT_mla = _mla_call(qT_mla, k_mla, vT_mla, batch, seq)
        x1 = _merge_call(x2, oT_na, oT_mla, gates, w, alpha)
        x2 = _ffn_call(x1, w, alpha)
    return x2.reshape(batch, seq, d)
```
